```python
import jax
import jax.numpy as jnp
from jax import lax
import numpy as np

D_MODEL = 1024
BATCH = 2
SEQ = 8192
DEPTH = 2
DEC_BATCH = 128
DEC_SEQ = 4
PAST_LEN = 2048
PAGE_SIZE = 128

F32 = jnp.float32
N_MIXERS = 4
GROUP_W = D_MODEL // N_MIXERS
HEAD_DIM = 64
GROUP_HEADS = GROUP_W // HEAD_DIM
MIX_W = N_MIXERS * GROUP_W
D_FF = 4 * D_MODEL
CONV_W = 4
LRU_C = 8.0
ROPE_THETA = 500000.0
ROPE_DIM = HEAD_DIM // 4
Q_BLOCK = 128
MOBA_BLOCK = 256
MOBA_TOPK = 3
CMP_LEN = 32
CMP_STRIDE = 16
SEL_BLOCK = 64
SEL_TOPK = 16
WINDOW = 512
EPS = 1e-6
SCALE = HEAD_DIM ** -0.5
IN_SPLITS = (GROUP_W, 2 * GROUP_W, 3 * GROUP_W, 4 * GROUP_W, 5 * GROUP_W,
             5 * GROUP_W + GROUP_HEADS, 6 * GROUP_W + GROUP_HEADS, 7 * GROUP_W + GROUP_HEADS,
             8 * GROUP_W + GROUP_HEADS, 9 * GROUP_W + GROUP_HEADS,
             9 * GROUP_W + GROUP_HEADS + 6 * HEAD_DIM)
IN_COLS = 9 * GROUP_W + GROUP_HEADS + 6 * HEAD_DIM + 3 * GROUP_HEADS

kernel_name = 'hymba_style_lru_fox_moba_nsa_decoder_step'


def rms_norm(x, g):
    xf = x.astype(F32)
    y = xf * lax.rsqrt(jnp.mean(xf * xf, axis=-1, keepdims=True) + EPS)
    return (y * g.astype(F32)).astype(x.dtype)


def rope_partial(x, pos):
    half = ROPE_DIM // 2
    inv = jnp.power(ROPE_THETA, -jnp.arange(half, dtype=F32) / half)
    ang = pos.astype(F32)[:, None] * inv[None, :]
    cos = jnp.cos(ang)[:, None, :]
    sin = jnp.sin(ang)[:, None, :]
    x1 = x[..., :half].astype(F32)
    x2 = x[..., half:ROPE_DIM].astype(F32)
    rot = jnp.concatenate([x1 * cos - x2 * sin, x2 * cos + x1 * sin], axis=-1).astype(x.dtype)
    return jnp.concatenate([rot, x[..., ROPE_DIM:]], axis=-1)


def masked_softmax(logits, mask):
    l = jnp.where(mask, logits, -jnp.inf)
    m = jnp.max(l, axis=-1, keepdims=True)
    m = jnp.where(jnp.isfinite(m), m, 0.0)
    p = jnp.exp(l - m)
    return p / jnp.maximum(jnp.sum(p, axis=-1, keepdims=True), 1e-30)


def pad_axis1(a, mult):
    extra = (-a.shape[1]) % mult
    return jnp.pad(a, [(0, 0), (0, extra)] + [(0, 0)] * (a.ndim - 2))


def sweep_queries(block_fn, q_list, P, T):
    qb = min(Q_BLOCK, T)
    nqb = -(-T // qb)
    pad = nqb * qb - T

    def split_blocks(a):
        a = jnp.pad(a, [(0, 0), (0, 0), (0, pad)] + [(0, 0)] * (a.ndim - 3))
        a = a.reshape(a.shape[:2] + (nqb, qb) + a.shape[3:])
        return jnp.moveaxis(a, 2, 0)

    ps = (P + jnp.arange(nqb * qb, dtype=jnp.int32)).reshape(nqb, qb)
    qs = tuple(split_blocks(a) for a in q_list)
    out = lax.map(lambda args: block_fn(args[0], *args[1]), (ps, qs))
    out = jnp.moveaxis(out, 0, 2)
    return out.reshape(out.shape[:2] + (nqb * qb,) + out.shape[4:])[:, :, :T]


def rglru(xa, conv_buf, h0, p, l):
    B, T, W = xa.shape
    xp = jnp.concatenate([conv_buf, xa], axis=1)
    w = p['lru_conv_w'][l]
    u = p['lru_conv_b'][l]
    for k in range(CONV_W):
        u = u + xp[:, k:k + T] * w[k]
    new_buf = xp[:, T:]
    ub = u.reshape(B, T, GROUP_HEADS, HEAD_DIM)
    r = jax.nn.sigmoid((jnp.einsum('btnc,ncd->btnd', ub, p['lru_wa'][l]).reshape(B, T, W) + p['lru_ba'][l]).astype(F32))
    i = jax.nn.sigmoid((jnp.einsum('btnc,ncd->btnd', ub, p['lru_wx'][l]).reshape(B, T, W) + p['lru_bx'][l]).astype(F32))
    log_a = -LRU_C * r * jax.nn.softplus(-p['lru_lambda'][l].astype(F32))
    b = jnp.sqrt(-jnp.expm1(2.0 * log_a)) * i * u.astype(F32)

    def step(hc, ab):
        a_t, b_t = ab
        hc = a_t * hc + b_t
        return hc, hc

    hT, hs = lax.scan(step, h0.astype(F32), (jnp.transpose(jnp.exp(log_a), (1, 0, 2)), jnp.transpose(b, (1, 0, 2))))
    return jnp.transpose(hs, (1, 0, 2)).astype(xa.dtype), hT.astype(h0.dtype), new_buf


def mixer(h, P, past, p, l):
    fox_kv_p, fox_logf_p, moba_kv_p, nsa_kv_p, nsa_win_p, lru_h_p, lru_conv_p = past
    B, T, _ = h.shape
    pos = P + jnp.arange(T, dtype=jnp.int32)
    z = h @ p['w_in'][l]
    xa, ga, fq, fk, fv, ff, mq, mk, mv, nq, nkv, ng = jnp.split(z, IN_SPLITS, axis=-1)
    heads = lambda a: a.reshape(B, T, GROUP_HEADS, HEAD_DIM)
    to_bhtd = lambda a: jnp.transpose(a, (0, 2, 1, 3))

    y_lru, lru_h_new, lru_conv_new = rglru(xa, lru_conv_p, lru_h_p, p, l)
    o_lru = y_lru * jax.nn.gelu(ga)

    g = p['fox_qk_g'][l]
    qf = rms_norm(heads(fq), g[0])
    kf = rms_norm(heads(fk), g[1])
    vf = heads(fv)
    logf = jax.nn.log_sigmoid(ff.astype(F32) + p['fox_bf'][l].astype(F32))
    fox_rows = jnp.stack([kf, vf], axis=2)
    fox_logf_rows = logf.astype(h.dtype)
    fkv = jnp.concatenate([fox_kv_p, fox_rows], axis=1)
    fk_all = to_bhtd(fkv[:, :, 0])
    fv_all = to_bhtd(fkv[:, :, 1])
    Fc = jnp.cumsum(jnp.concatenate([fox_logf_p.astype(F32), logf], axis=1), axis=1)
    Fc = jnp.transpose(Fc, (0, 2, 1))
    Tk = fkv.shape[1]
    fpos = jnp.arange(Tk, dtype=jnp.int32)

    def fox_block(pb, q):
        Fq = jnp.take(Fc, jnp.minimum(pb, Tk - 1), axis=2)
        logits = (jnp.einsum('bhqd,bhkd->bhqk', q, fk_all, preferred_element_type=F32) * SCALE
                  + Fq[..., None] - Fc[:, :, None, :])
        w = masked_softmax(logits, fpos[None, :] <= pb[:, None])
        return jnp.einsum('bhqk,bhkd->bhqd', w.astype(fv_all.dtype), fv_all)

    o_fox = to_bhtd(sweep_queries(fox_block, (to_bhtd(qf),), P, T)).reshape(B, T, GROUP_W)

    g = p['moba_qk_g'][l]
    qm = rope_partial(rms_norm(heads(mq), g[0]), pos)
    km = rope_partial(rms_norm(heads(mk), g[1]), pos)
    vm = heads(mv)
    moba_rows = jnp.stack([km, vm], axis=2)
    mkv = pad_axis1(jnp.concatenate([moba_kv_p, moba_rows], axis=1), MOBA_BLOCK)
    NB = mkv.shape[1] // MOBA_BLOCK
    kb = to_bhtd(mkv[:, :, 0]).reshape(B, GROUP_HEADS, NB, MOBA_BLOCK, HEAD_DIM)
    vb = to_bhtd(mkv[:, :, 1]).reshape(B, GROUP_HEADS, NB, MOBA_BLOCK, HEAD_DIM)
    kmean = jnp.mean(kb.astype(F32), axis=3)
    n_top = min(MOBA_TOPK, NB)
    gather2 = jax.vmap(jax.vmap(lambda blocks, idx: blocks[idx]))

    def moba_block(pb, q):
        cur = jnp.minimum(pb // MOBA_BLOCK, NB - 1)
        s = jnp.einsum('bhqd,bhnd->bhqn', q.astype(F32), kmean)
        s = jnp.where(jnp.arange(NB)[None, :] < cur[:, None], s, -jnp.inf)
        _, top = lax.top_k(s, n_top)
        own = jnp.broadcast_to(cur[:, None], top.shape[:3] + (1,))
        idx = jnp.concatenate([top, own], axis=-1)
        ok = jnp.concatenate([top < cur[:, None], jnp.ones(own.shape, bool)], axis=-1)
        kg = gather2(kb, idx)
        vg = gather2(vb, idx)
        kp = idx[..., None] * MOBA_BLOCK + jnp.arange(MOBA_BLOCK)
        mask = ok[..., None] & (kp <= pb[:, None, None])
        logits = jnp.einsum('bhqd,bhqnmd->bhqnm', q, kg, preferred_element_type=F32) * SCALE
        shp = logits.shape[:3] + (-1,)
        w = masked_softmax(logits.reshape(shp), mask.reshape(shp))
        return jnp.einsum('bhqnm,bhqnmd->bhqd', w.reshape(logits.shape).astype(vg.dtype), vg)

    o_moba = to_bhtd(sweep_queries(moba_block, (to_bhtd(qm),), P, T)).reshape(B, T, GROUP_W)

    g = p['nsa_qk_g'][l]
    nqn = rms_norm(heads(nq), g[0])
    nqr = rope_partial(nqn, pos)
    kc, vc, ks_, vs_, kw, vw = jnp.split(nkv, 6, axis=-1)
    ks_ = rope_partial(rms_norm(ks_, g[2])[:, :, None], pos)[:, :, 0]
    kw = rope_partial(rms_norm(kw, g[3])[:, :, None], pos)[:, :, 0]
    nsa_rows = jnp.stack([kc, vc, ks_, vs_], axis=2)
    nall = pad_axis1(jnp.concatenate([nsa_kv_p, nsa_rows], axis=1), MOBA_BLOCK)
    Tkp = nall.shape[1]
    NC = (Tkp - CMP_LEN) // CMP_STRIDE + 1
    cidx = jnp.arange(NC)[:, None] * CMP_STRIDE + jnp.arange(CMP_LEN)[None, :]
    cpos = p['nsa_cmp_pos'][l]
    cw = p['nsa_cmp_w'][l]
    k_cmp = rms_norm((nall[:, :, 0][:, cidx] + cpos[0]).reshape(B, NC, CMP_LEN * HEAD_DIM) @ cw[0], g[1])
    v_cmp = (nall[:, :, 1][:, cidx] + cpos[1]).reshape(B, NC, CMP_LEN * HEAD_DIM) @ cw[1]
    NS = Tkp // SEL_BLOCK
    ksb = nall[:, :, 2].reshape(B, NS, SEL_BLOCK, HEAD_DIM)
    vsb = nall[:, :, 3].reshape(B, NS, SEL_BLOCK, HEAD_DIM)
    cstart = jnp.arange(NC) * CMP_STRIDE
    cend = cstart + CMP_LEN - 1
    bstart = jnp.arange(NS) * SEL_BLOCK
    overlap = ((cstart[:, None] < bstart[None, :] + SEL_BLOCK)
               & (cstart[:, None] + CMP_LEN > bstart[None, :])).astype(F32)
    n_sel = min(SEL_TOPK, NS)
    win_all = jnp.concatenate([nsa_win_p, jnp.stack([kw, vw], axis=2)], axis=1)
    Wb = nsa_win_p.shape[1]
    keep = min(WINDOW, Wb + T)
    nsa_win_new = win_all[:, Wb + T - keep:]
    win_pad = jnp.pad(win_all, ((0, 0), (WINDOW, Q_BLOCK), (0, 0), (0, 0)))
    gates = to_bhtd(jax.nn.sigmoid(ng.astype(F32)).reshape(B, T, GROUP_HEADS, 3)).astype(h.dtype)
    gather1 = jax.vmap(lambda blocks, idx: blocks[idx])

    def nsa_block(pb, qn, qr, gt):
        qb_ = pb.shape[0]
        lc = jnp.einsum('bhqd,bcd->bhqc', qn, k_cmp, preferred_element_type=F32) * SCALE
        pc = masked_softmax(lc, cend[None, :] <= pb[:, None])
        o_c = jnp.einsum('bhqc,bcd->bhqd', pc.astype(v_cmp.dtype), v_cmp)
        imp = jnp.einsum('bhqc,cn->bqn', pc, overlap)
        cur = pb // SEL_BLOCK
        j = jnp.arange(NS)[None, :]
        forced = (j == 0) | (j == cur[:, None]) | (j == cur[:, None] - 1)
        imp = jnp.where(forced, jnp.inf, imp)
        imp = jnp.where(j * SEL_BLOCK > pb[:, None], -jnp.inf, imp)
        _, sel = lax.top_k(imp, n_sel)
        kg = gather1(ksb, sel)
        vg = gather1(vsb, sel)
        kp = sel[..., None] * SEL_BLOCK + jnp.arange(SEL_BLOCK)
        smask = (kp <= pb[:, None, None]).reshape(B, qb_, -1)[:, None]
        ls = jnp.einsum('bhqd,bqnmd->bhqnm', qr, kg, preferred_element_type=F32) * SCALE
        ps_ = masked_softmax(ls.reshape(ls.shape[:3] + (-1,)), smask)
        o_s = jnp.einsum('bhqk,bqkd->bhqd', ps_.astype(vg.dtype), vg.reshape(B, qb_, -1, HEAD_DIM))
        i0 = pb[0] - P
        wkv = lax.dynamic_slice_in_dim(win_pad, i0 + Wb, WINDOW + qb_, axis=1)
        wpos = pb[0] - WINDOW + jnp.arange(WINDOW + qb_)
        wmask = ((wpos[None, :] >= P - Wb) & (wpos[None, :] <= pb[:, None])
                 & (wpos[None, :] >= pb[:, None] - WINDOW))
        lw = jnp.einsum('bhqd,bkd->bhqk', qr, wkv[:, :, 0], preferred_element_type=F32) * SCALE
        pw = masked_softmax(lw, wmask)
        o_w = jnp.einsum('bhqk,bkd->bhqd', pw.astype(wkv.dtype), wkv[:, :, 1])
        return gt[..., 0:1] * o_c + gt[..., 1:2] * o_s + gt[..., 2:3] * o_w

    o_nsa = to_bhtd(sweep_queries(nsa_block, (to_bhtd(nqn), to_bhtd(nqr), gates), P, T)).reshape(B, T, GROUP_W)

    og = p['out_g'][l]
    o_cat = jnp.concatenate([rms_norm(o_lru, og[0]), rms_norm(o_fox, og[1]),
                             rms_norm(o_moba, og[2]), rms_norm(o_nsa, og[3])], axis=-1)
    out = o_cat @ p['w_out'][l]
    new_state = (fox_rows, fox_logf_rows, moba_rows, nsa_rows, nsa_win_new, lru_h_new, lru_conv_new)
    return out, new_state


def layer(x, c, P, past, p, l):
    mod = jax.nn.silu(c) @ p['w_ada'][l] + p['b_ada'][l]
    sh1, sc1, gt1, sh2, sc2, gt2 = jnp.split(mod[:, None, :], 6, axis=-1)
    hm = rms_norm(x, p['norm_g'][l, 0]) * (1.0 + sc1) + sh1
    mix, new_state = mixer(hm, P, past, p, l)
    x = x + gt1 * mix
    hf = rms_norm(x, p['norm_g'][l, 1]) * (1.0 + sc2) + sh2
    ffo = jnp.square(jax.nn.relu(hf @ p['w_up'][l])) @ p['w_down'][l]
    return x + gt2 * ffo, new_state


def run_trunk(x, c, P, past_fn, p):
    states = []
    for l in range(DEPTH):
        x, st = layer(x, c, P, past_fn(l), p, l)
        states.append(st)
    stacked = [jnp.stack([s[i] for s in states], axis=0) for i in range(len(states[0]))]
    return x, stacked


def setup_inputs(seed: int = 0) -> dict:
    key = jax.random.key(seed)
    keys = iter(jax.random.split(key, 64))

    def nrm(shape, scale=1.0):
        return scale * jax.random.normal(next(keys), shape, F32)

    n_pages = PAST_LEN // PAGE_SIZE
    n_used = DEC_BATCH * n_pages
    n_phys = n_used + -(-n_used // 4)
    win_buf = min(WINDOW, PAST_LEN)
    x_prompt = nrm((BATCH, SEQ, D_MODEL))
    x_sample = nrm((DEC_BATCH, DEC_SEQ, D_MODEL))
    cache_fox_kv = nrm((DEPTH, n_phys, PAGE_SIZE, 2, GROUP_HEADS, HEAD_DIM))
    cache_fox_logf = jax.nn.log_sigmoid(3.0 + nrm((DEPTH, n_phys, PAGE_SIZE, GROUP_HEADS), 0.5))
    cache_moba_kv = nrm((DEPTH, n_phys, PAGE_SIZE, 2, GROUP_HEADS, HEAD_DIM))
    cache_nsa_kv = nrm((DEPTH, n_phys, PAGE_SIZE, 4, HEAD_DIM))
    state_nsa_win = nrm((DEPTH, DEC_BATCH, win_buf, 2, HEAD_DIM))
    state_lru_h = nrm((DEPTH, DEC_BATCH, GROUP_W), 0.5)
    state_lru_conv = nrm((DEPTH, DEC_BATCH, CONV_W - 1, GROUP_W))
    page_table = jax.random.permutation(next(keys), n_phys)[:n_used].reshape(DEC_BATCH, n_pages).astype(jnp.int32)
    c_prompt = nrm((BATCH, D_MODEL))
    c_sample = nrm((DEC_BATCH, D_MODEL))
    return {
        'x_prompt': x_prompt, 'x_sample': x_sample,
        'cache_fox_kv': cache_fox_kv, 'cache_fox_logf': cache_fox_logf,
        'cache_moba_kv': cache_moba_kv, 'cache_nsa_kv': cache_nsa_kv,
        'state_nsa_win': state_nsa_win, 'state_lru_h': state_lru_h, 'state_lru_conv': state_lru_conv,
        'page_table': page_table, 'c_prompt': c_prompt, 'c_sample': c_sample,
        'norm_g': 1.0 + nrm((DEPTH, 2, D_MODEL), 0.02),
        'w_ada': nrm((DEPTH, D_MODEL, 6 * D_MODEL), 0.5 * D_MODEL ** -0.5),
        'b_ada': nrm((DEPTH, 6 * D_MODEL), 0.02),
        'w_in': nrm((DEPTH, D_MODEL, IN_COLS), D_MODEL ** -0.5),
        'lru_conv_w': nrm((DEPTH, CONV_W, GROUP_W), CONV_W ** -0.5),
        'lru_conv_b': nrm((DEPTH, GROUP_W), 0.02),
        'lru_wa': nrm((DEPTH, GROUP_HEADS, HEAD_DIM, HEAD_DIM), HEAD_DIM ** -0.5),
        'lru_ba': nrm((DEPTH, GROUP_W), 0.02),
        'lru_wx': nrm((DEPTH, GROUP_HEADS, HEAD_DIM, HEAD_DIM), HEAD_DIM ** -0.5),
        'lru_bx': nrm((DEPTH, GROUP_W), 0.02),
        'lru_lambda': jax.random.uniform(next(keys), (DEPTH, GROUP_W), F32, 4.3, 9.0),
        'fox_bf': 3.0 + nrm((DEPTH, GROUP_HEADS), 0.5),
        'fox_qk_g': 1.0 + nrm((DEPTH, 2, HEAD_DIM), 0.02),
        'moba_qk_g': 1.0 + nrm((DEPTH, 2, HEAD_DIM), 0.02),
        'nsa_qk_g': 1.0 + nrm((DEPTH, 4, HEAD_DIM), 0.02),
        'nsa_cmp_pos': nrm((DEPTH, 2, CMP_LEN, HEAD_DIM), 0.02),
        'nsa_cmp_w': nrm((DEPTH, 2, CMP_LEN * HEAD_DIM, HEAD_DIM), (CMP_LEN * HEAD_DIM) ** -0.5),
        'out_g': 1.0 + nrm((DEPTH, N_MIXERS, GROUP_W), 0.02),
        'w_out': nrm((DEPTH, MIX_W, D_MODEL), MIX_W ** -0.5),
        'w_up': nrm((DEPTH, D_MODEL, D_FF), D_MODEL ** -0.5),
        'w_down': nrm((DEPTH, D_FF, D_MODEL), D_FF ** -0.5),
    }


def reference(x_prompt, x_sample, cache_fox_kv, cache_fox_logf, cache_moba_kv, cache_nsa_kv,
              state_nsa_win, state_lru_h, state_lru_conv, page_table, c_prompt, c_sample,
              norm_g, w_ada, b_ada, w_in, lru_conv_w, lru_conv_b, lru_wa, lru_ba, lru_wx, lru_bx,
              lru_lambda, fox_bf, fox_qk_g, moba_qk_g, nsa_qk_g, nsa_cmp_pos, nsa_cmp_w, out_g,
              w_out, w_up, w_down):
    p = {'norm_g': norm_g, 'w_ada': w_ada, 'b_ada': b_ada, 'w_in': w_in,
         'lru_conv_w': lru_conv_w, 'lru_conv_b': lru_conv_b, 'lru_wa': lru_wa, 'lru_ba': lru_ba,
         'lru_wx': lru_wx, 'lru_bx': lru_bx, 'lru_lambda': lru_lambda, 'fox_bf': fox_bf,
         'fox_qk_g': fox_qk_g, 'moba_qk_g': moba_qk_g, 'nsa_qk_g': nsa_qk_g,
         'nsa_cmp_pos': nsa_cmp_pos, 'nsa_cmp_w': nsa_cmp_w, 'out_g': out_g,
         'w_out': w_out, 'w_up': w_up, 'w_down': w_down}

    B = x_prompt.shape[0]
    dt = x_prompt.dtype
    empty = (jnp.zeros((B, 0, 2, GROUP_HEADS, HEAD_DIM), dt), jnp.zeros((B, 0, GROUP_HEADS), dt),
             jnp.zeros((B, 0, 2, GROUP_HEADS, HEAD_DIM), dt), jnp.zeros((B, 0, 4, HEAD_DIM), dt),
             jnp.zeros((B, 0, 2, HEAD_DIM), dt), jnp.zeros((B, GROUP_W), dt),
             jnp.zeros((B, CONV_W - 1, GROUP_W), dt))
    y_prompt, st_p = run_trunk(x_prompt, c_prompt, 0, lambda l: empty, p)

    P = page_table.shape[1] * cache_fox_kv.shape[2]

    def paged(pool):
        rows = pool[page_table]
        return rows.reshape((rows.shape[0], P) + pool.shape[2:])

    def sample_past(l):
        return (paged(cache_fox_kv[l]), paged(cache_fox_logf[l]), paged(cache_moba_kv[l]),
                paged(cache_nsa_kv[l]), state_nsa_win[l], state_lru_h[l], state_lru_conv[l])

    y_sample, st_s = run_trunk(x_sample, c_sample, P, sample_past, p)
    p_fox_kv, p_fox_logf, p_moba_kv, p_nsa_kv, p_nsa_win, p_lru_h, p_lru_conv = st_p
    s_fox_kv, s_fox_logf, s_moba_kv, s_nsa_kv, s_nsa_win, s_lru_h, s_lru_conv = st_s
    return (y_prompt, y_sample, p_fox_kv, p_fox_logf, p_moba_kv, p_nsa_kv, p_nsa_win, p_lru_h, p_lru_conv,
            s_fox_kv, s_fox_logf, s_moba_kv, s_nsa_kv, s_nsa_win, s_lru_h, s_lru_conv)
```

```python
import functools

import numpy as np
import jax
import jax.numpy as jnp
from jax import lax
from jax.experimental import pallas as pl
from jax.experimental.pallas import tpu as pltpu

F32 = jnp.float32
BF16 = jnp.bfloat16

D_MODEL = 1024
N_MIXERS = 4
GROUP_W = 256
HEAD_DIM = 64
GROUP_HEADS = 4
D_FF = 4 * D_MODEL
CONV_W = 4
LRU_C = 8.0
ROPE_THETA = 500000.0
ROPE_DIM = HEAD_DIM // 4
ROPE_HALF = ROPE_DIM // 2
MOBA_BLOCK = 256
MOBA_TOPK = 3
CMP_LEN = 32
CMP_STRIDE = 16
SEL_BLOCK = 64
SEL_TOPK = 16
WINDOW = 512
EPS = 1e-6
SCALE = HEAD_DIM ** -0.5
LANES = 128
IN_W = 22 * LANES
NEG_INF = float("-inf")
VMEM_LIMIT = 56 * 1024 * 1024

TS = 8


def _cparams(sem):
    return pltpu.CompilerParams(dimension_semantics=sem, vmem_limit_bytes=VMEM_LIMIT)


def _dot(a, b):
    return jnp.dot(a, b, preferred_element_type=F32)


def _dot_nt(a, b):
    return lax.dot_general(a, b, (((1,), (1,)), ((), ())), preferred_element_type=F32)


def _split2(x):
    hi = x.astype(BF16)
    lo = (x - hi.astype(F32)).astype(BF16)
    return hi, lo


def _split3(x):
    hi = x.astype(BF16)
    r = x - hi.astype(F32)
    mid = r.astype(BF16)
    lo = (r - mid.astype(F32)).astype(BF16)
    return hi, mid, lo


def _dot_x3(x, m):
    hi, mid, lo = _split3(x)
    return _dot(hi, m) + _dot(mid, m) + _dot(lo, m)


def _log_sigmoid(x):
    return jnp.minimum(x, 0.0) - jnp.log1p(jnp.exp(-jnp.abs(x)))


def _softplus(x):
    return jnp.maximum(x, 0.0) + jnp.log1p(jnp.exp(-jnp.abs(x)))


def _gelu_tanh(x):
    return 0.5 * x * (1.0 + jnp.tanh(np.sqrt(2.0 / np.pi).astype(np.float32) * (x + 0.044715 * (x * x * x))))


def _rms_rows(x, g):
    return x * lax.rsqrt(jnp.mean(x * x, axis=-1, keepdims=True) + EPS) * g


def _head_lane(width=GROUP_W):
    return lax.broadcasted_iota(jnp.int32, (1, width), 1) // HEAD_DIM


def _stack_heads(q):
    hl = _head_lane()
    zero = jnp.zeros_like(q)
    return jnp.concatenate([jnp.where(hl == h, q, zero) for h in range(GROUP_HEADS)], axis=0)


def _unstack_heads(o, t):
    hl = _head_lane()
    out = jnp.zeros((t, GROUP_W), o.dtype)
    for h in range(GROUP_HEADS):
        out = out + jnp.where(hl == h, o[h * t:(h + 1) * t], 0.0)
    return out


def _rep4(a):
    return jnp.concatenate([a] * GROUP_HEADS, axis=0)


def _masked_softmax(l, mask):
    l = jnp.where(mask, l, NEG_INF)
    m = jnp.max(l, axis=-1, keepdims=True)
    m = jnp.where(m > NEG_INF, m, 0.0)
    p = jnp.exp(l - m)
    return p / jnp.maximum(jnp.sum(p, axis=-1, keepdims=True), 1e-30)


def _online_update(s, v, m_ref, l_ref, acc_ref):
    m_old = m_ref[...]
    m_new = jnp.maximum(m_old, jnp.max(s, axis=-1, keepdims=True))
    m_safe = jnp.where(m_new > NEG_INF, m_new, 0.0)
    alpha = jnp.exp(m_old - m_safe)
    p = jnp.exp(s - m_safe)
    l_ref[...] = alpha * l_ref[...] + jnp.sum(p, axis=-1, keepdims=True)
    acc_ref[...] = alpha * acc_ref[...] + _dot(p.astype(BF16), v)
    m_ref[...] = m_new


def _top_select(score, n_pick):
    width = score.shape[-1]
    j = lax.broadcasted_iota(jnp.int32, (1, width), 1)
    sel = jnp.zeros(score.shape, F32)
    work = score
    for _ in range(n_pick):
        mx = jnp.max(work, axis=-1, keepdims=True)
        cand = (work == mx) & (mx > NEG_INF)
        idx = jnp.min(jnp.where(cand, j, width), axis=-1, keepdims=True)
        pick = j == idx
        sel = jnp.where(pick, 1.0, sel)
        work = jnp.where(pick, NEG_INF, work)
    return sel


def _ada_kernel(c_ref, w_ref, b_ref, o_ref):
    c = c_ref[...]
    a = (c * jax.nn.sigmoid(c)).astype(BF16)
    o_ref[0] = _dot(a, w_ref[0].astype(BF16)) + b_ref[0]


def _ada(c_all, w_ada, b_ada):
    depth, _, n = w_ada.shape
    rows = c_all.shape[0]
    tn = 1024
    return pl.pallas_call(
        _ada_kernel,
        grid=(depth, n // tn),
        in_specs=[pl.BlockSpec((rows, D_MODEL), lambda l, j: (0, 0)),
                  pl.BlockSpec((1, D_MODEL, tn), lambda l, j: (l, 0, j)),
                  pl.BlockSpec((1, 1, tn), lambda l, j: (l, 0, j))],
        out_specs=pl.BlockSpec((1, rows, tn), lambda l, j: (l, 0, j)),
        out_shape=jax.ShapeDtypeStruct((depth, rows, n), F32),
        compiler_params=_cparams(("arbitrary", "arbitrary")),
        name="ada_mod",
    )(c_all, w_ada, b_ada.reshape(depth, 1, n))


def _in_perm():
    gw, hd = GROUP_W, HEAD_DIM
    nkv0 = 9 * gw + GROUP_HEADS
    cols = list(range(0, 5 * gw))
    cols += list(range(5 * gw + 4, 8 * gw + 4))
    cols += list(range(8 * gw + 4, 9 * gw + 4))
    part = lambda k: list(range(nkv0 + k * hd, nkv0 + (k + 1) * hd))
    cols += part(0) + part(1)
    cols += part(2) + part(3)
    cols += part(4) + part(5)
    cols += list(range(5 * gw, 5 * gw + 4))
    cols += list(range(nkv0 + 6 * hd, nkv0 + 6 * hd + 12))
    return np.asarray(cols, np.int32)


C_LRU, C_FQ, C_FK, C_FV, C_MQ, C_MK, C_MV, C_NQ, C_KCVC, C_SEL, C_WIN, C_SMALL = (
    0, 512, 768, 1024, 1280, 1536, 1792, 2048, 2304, 2432, 2560, 2688)


def _block_diag_mean():
    m = np.zeros((GROUP_W, GROUP_W), np.float32)
    for h in range(GROUP_HEADS):
        m[h * HEAD_DIM:(h + 1) * HEAD_DIM, h * HEAD_DIM:(h + 1) * HEAD_DIM] = 1.0 / HEAD_DIM
    return m


def _rep_matrix():
    m = np.zeros((LANES, 2 * GROUP_W), np.float32)
    for j in range(HEAD_DIM):
        for k in range(GROUP_HEADS):
            m[j, k * HEAD_DIM + j] = 1.0
            m[HEAD_DIM + j, GROUP_W + k * HEAD_DIM + j] = 1.0
    return m


def _rope_tables(pos):
    inv = jnp.power(ROPE_THETA, -jnp.arange(ROPE_HALF, dtype=F32) / ROPE_HALF)
    ang = pos.astype(F32)[:, None] * inv[None, :]
    cos, sin = jnp.cos(ang), jnp.sin(ang)
    n = pos.shape[0]
    ones = jnp.ones((n, HEAD_DIM - ROPE_DIM), F32)
    zeros = jnp.zeros((n, HEAD_DIM - ROPE_DIM), F32)
    z8 = jnp.zeros((n, ROPE_HALF), F32)
    c = jnp.concatenate([cos, cos, ones], axis=1)
    sa = jnp.concatenate([z8, sin, zeros], axis=1)
    sb = jnp.concatenate([-sin, z8, zeros], axis=1)
    two = lambda a: jnp.concatenate([a, a], axis=1)
    return two(c), two(sa), two(sb)


def _in_proj_kernel(*refs, tiles_per_seq, prompt, tm):
    (x_ref, sc_ref, sh_ref, g_ref, w_ref, tc_ref, ta_ref, tb_ref, gains_ref, bf_ref, bd_ref,
     rep_ref) = refs[:12]
    k = 12
    if prompt:
        tri_ref = refs[k]
        k += 1
    (zl_ref, fq_ref, fkv_ref, fkv16_ref, small_ref, mq_ref, mkv_ref, mkv16_ref,
     nqn_ref, nqr_ref, nkv_ref, nsel16_ref, nwin_ref, nwin16_ref, kcvc_ref) = refs[k:k + 15]
    k += 15
    if prompt:
        fc_ref, kmean_ref, carry_ref = refs[k:k + 3]

    x = x_ref[...]
    xn = x * lax.rsqrt(jnp.mean(x * x, axis=-1, keepdims=True) + EPS) * g_ref[...]
    hm = (xn * (1.0 + sc_ref[0]) + sh_ref[0]).astype(BF16)

    def proj(c0, width):
        return _dot(hm, w_ref[:, c0:c0 + width])

    bd = bd_ref[...]
    tc, ta, tb = tc_ref[...], ta_ref[...], tb_ref[...]

    def hnorm(z, row):
        hi, lo = _split2(z * z)
        ms = _dot(hi, bd) + _dot(lo, bd)
        return z * lax.rsqrt(ms + EPS) * gains_ref[row:row + 1, :]

    def rope128(y, c, a, b):
        return y * c + pltpu.roll(y, 8, axis=1) * a + pltpu.roll(y, LANES - 8, axis=1) * b

    def rope(y):
        return jnp.concatenate([rope128(y[:, :LANES], tc, ta, tb), rope128(y[:, LANES:], tc, ta, tb)], axis=1)

    zl_ref[...] = proj(C_LRU, 512)

    fq_ref[...] = (hnorm(proj(C_FQ, 256), 0) * SCALE).astype(fq_ref.dtype)
    fk = hnorm(proj(C_FK, 256), 1)
    fv = proj(C_FV, 256)
    fkv = jnp.concatenate([fk, fv], axis=1)
    fkv_ref[...] = fkv
    fkv16_ref[...] = fkv.astype(BF16)

    mq_ref[...] = rope(hnorm(proj(C_MQ, 256), 2)) * SCALE
    mk = rope(hnorm(proj(C_MK, 256), 3))
    mv = proj(C_MV, 256)
    mkv = jnp.concatenate([mk, mv], axis=1)
    mkv_ref[...] = mkv
    mkv16_ref[...] = mkv.astype(BF16)

    nqn = hnorm(proj(C_NQ, 256), 4)
    nqn_ref[...] = (nqn * SCALE).astype(nqn_ref.dtype)
    nqr_ref[...] = (rope(nqn) * SCALE).astype(nqr_ref.dtype)

    kcvc = proj(C_KCVC, LANES)
    kcvc_ref[...] = kcvc
    sw = proj(C_SEL, 2 * LANES)
    first = (lax.broadcasted_iota(jnp.int32, (1, 2 * LANES), 1) % LANES) < HEAD_DIM
    swn = jnp.where(first, rope(hnorm(sw, 5)), sw)
    sel_rows = swn[:, :LANES]
    win_rows = swn[:, LANES:]
    nkv_ref[...] = jnp.concatenate([kcvc, sel_rows], axis=1)
    nwin_ref[...] = win_rows
    rep = rep_ref[...]
    nsel16_ref[...] = _dot(sel_rows.astype(BF16), rep).astype(BF16)
    nwin16_ref[...] = _dot(win_rows.astype(BF16), rep).astype(BF16)

    zs = proj(C_SMALL, LANES)
    lane = lax.broadcasted_iota(jnp.int32, (1, LANES), 1)
    logf = _log_sigmoid(zs + bf_ref[...])
    gate = jax.nn.sigmoid(zs)
    small_ref[...] = jnp.where(lane < GROUP_HEADS, logf, jnp.where(lane < 16, gate, 0.0))

    if prompt:
        i = pl.program_id(0)

        @pl.when(i % tiles_per_seq == 0)
        def _():
            carry_ref[...] = jnp.zeros_like(carry_ref)

        lf = jnp.where(lane < GROUP_HEADS, logf, 0.0)
        hi, mid, lo = _split3(lf)
        tri = tri_ref[...]
        fc = _dot(tri, hi) + _dot(tri, mid) + _dot(tri, lo) + carry_ref[...]
        fc_ref[...] = fc
        carry_ref[...] = fc[tm - 1:tm, :]
        nblk = tm // MOBA_BLOCK
        kmean_ref[0] = jnp.sum(mk.reshape(nblk, MOBA_BLOCK, GROUP_W), axis=1) * (1.0 / MOBA_BLOCK)


def _in_proj(x, sc, sh, g, w, tabs, gains, bf_row, *, tiles_per_seq, prompt, tm):
    rows = x.shape[0]
    n_tiles = rows // tm
    tb = sc.shape[1]
    n_tab_tiles = tabs[0].shape[0] // tm
    mod_map = (lambda i: (i // tiles_per_seq, 0, 0)) if tb == 1 else (lambda i: (i, 0, 0))
    tab_map = lambda i: (i % n_tab_tiles, 0)
    const = lambda i: (0, 0)
    row_map = lambda i: (i, 0)
    bd = jnp.asarray(_block_diag_mean(), BF16)
    rep = jnp.asarray(_rep_matrix(), BF16)
    in_specs = [pl.BlockSpec((tm, D_MODEL), row_map),
                pl.BlockSpec((1, tb, D_MODEL), mod_map),
                pl.BlockSpec((1, tb, D_MODEL), mod_map),
                pl.BlockSpec((1, D_MODEL), const),
                pl.BlockSpec((D_MODEL, IN_W), const),
                pl.BlockSpec((tm, LANES), tab_map),
                pl.BlockSpec((tm, LANES), tab_map),
                pl.BlockSpec((tm, LANES), tab_map),
                pl.BlockSpec((8, GROUP_W), const),
                pl.BlockSpec((1, LANES), const),
                pl.BlockSpec((GROUP_W, GROUP_W), const),
                pl.BlockSpec((LANES, 2 * GROUP_W), const)]
    args = [x, sc, sh, g, w, tabs[0], tabs[1], tabs[2], gains, bf_row, bd, rep]
    if prompt:
        tri = jnp.asarray(np.tril(np.ones((tm, tm), np.float32)), BF16)
        in_specs.append(pl.BlockSpec((tm, tm), const))
        args.append(tri)

    def o(width, dtype):
        return jax.ShapeDtypeStruct((rows, width), dtype), pl.BlockSpec((tm, width), row_map)

    q_dtype = BF16 if prompt else F32
    outs = [o(512, F32), o(256, q_dtype), o(512, F32), o(512, BF16), o(LANES, F32),
            o(256, F32), o(512, F32), o(512, BF16),
            o(256, q_dtype), o(256, q_dtype), o(256, F32), o(512, BF16), o(LANES, F32), o(512, BF16),
            o(LANES, F32)]
    scratch = []
    if prompt:
        outs.append(o(LANES, F32))
        nblk = tm // MOBA_BLOCK
        outs.append((jax.ShapeDtypeStruct((n_tiles, nblk, GROUP_W), F32),
                     pl.BlockSpec((1, nblk, GROUP_W), lambda i: (i, 0, 0))))
        scratch.append(pltpu.VMEM((1, LANES), F32))
    res = pl.pallas_call(
        functools.partial(_in_proj_kernel, tiles_per_seq=tiles_per_seq, prompt=prompt, tm=tm),
        grid=(n_tiles,),
        in_specs=in_specs,
        out_specs=[s for _, s in outs],
        out_shape=[s for s, _ in outs],
        scratch_shapes=scratch,
        compiler_params=_cparams(("arbitrary",)),
        name="in_proj_prompt" if prompt else "in_proj_sample",
    )(*args)
    names = ["zl", "fq", "fkv", "fkv16", "small", "mq", "mkv", "mkv16", "nqn", "nqr", "nkv",
             "nsel16", "nwin", "nwin16", "kcvc"] + (["fc", "kmean"] if prompt else [])
    return dict(zip(names, res))


def _lru_coeffs(u, wa_ref, wx_ref, ba_ref, bx_ref, lam_ref):
    u16 = u.astype(BF16)
    r = jax.nn.sigmoid(_dot(u16, wa_ref[...]) + ba_ref[...])
    i = jax.nn.sigmoid(_dot(u16, wx_ref[...]) + bx_ref[...])
    log_a = -LRU_C * r * _softplus(-lam_ref[...])
    a = jnp.exp(log_a)
    b = jnp.sqrt(-jnp.tanh(log_a) * (a * a + 1.0)) * i * u
    return a, b


def _lru_prompt_kernel(zl_ref, cw_ref, cb_ref, wa_ref, wx_ref, ba_ref, bx_ref, lam_ref, og_ref,
                       o_ref, ht_ref, cnew_ref, xp_s, a_s, b_s, hs_s, h_s, *, tm):
    t = pl.program_id(1)

    @pl.when(t == 0)
    def _():
        h_s[...] = jnp.zeros_like(h_s)
        xp_s[0:8, :] = jnp.zeros((8, GROUP_W), F32)

    xa = zl_ref[:, 0:GROUP_W]
    ga = zl_ref[:, GROUP_W:2 * GROUP_W]
    xp_s[8:8 + tm, :] = xa
    u = (cb_ref[...] + cw_ref[0:1, :] * xp_s[5:5 + tm, :] + cw_ref[1:2, :] * xp_s[6:6 + tm, :]
         + cw_ref[2:3, :] * xp_s[7:7 + tm, :] + cw_ref[3:4, :] * xa)
    tail = xp_s[tm:tm + 8, :]
    xp_s[0:8, :] = tail
    cnew_ref[0] = tail
    a, b = _lru_coeffs(u, wa_ref, wx_ref, ba_ref, bx_ref, lam_ref)
    a_s[...] = a
    b_s[...] = b

    def body(j, h):
        r0 = pl.multiple_of(j * 8, 8)
        a8 = a_s[pl.ds(r0, 8), :]
        b8 = b_s[pl.ds(r0, 8), :]
        rows = []
        for k in range(8):
            h = a8[k:k + 1, :] * h + b8[k:k + 1, :]
            rows.append(h)
        hs_s[pl.ds(r0, 8), :] = jnp.concatenate(rows, axis=0)
        return h

    h = lax.fori_loop(0, tm // 8, body, h_s[...])
    h_s[...] = h
    ht_ref[0] = h
    o = hs_s[...] * _gelu_tanh(ga)
    o_ref[...] = _rms_rows(o, og_ref[...]).astype(BF16)


def _lru_prompt(zl, lw, batch, seq, tm):
    nt = seq // tm
    const = lambda b, t: (0, 0)
    vec = pl.BlockSpec((1, GROUP_W), const)
    mat = pl.BlockSpec((GROUP_W, GROUP_W), const)
    return pl.pallas_call(
        functools.partial(_lru_prompt_kernel, tm=tm),
        grid=(batch, nt),
        in_specs=[pl.BlockSpec((tm, 2 * GROUP_W), lambda b, t: (b * nt + t, 0)),
                  pl.BlockSpec((CONV_W, GROUP_W), const), vec, mat, mat, vec, vec, vec, vec],
        out_specs=[pl.BlockSpec((tm, GROUP_W), lambda b, t: (b * nt + t, 0)),
                   pl.BlockSpec((1, 1, GROUP_W), lambda b, t: (b, 0, 0)),
                   pl.BlockSpec((1, 8, GROUP_W), lambda b, t: (b, 0, 0))],
        out_shape=[jax.ShapeDtypeStruct((batch * seq, GROUP_W), BF16),
                   jax.ShapeDtypeStruct((batch, 1, GROUP_W), F32),
                   jax.ShapeDtypeStruct((batch, 8, GROUP_W), F32)],
        scratch_shapes=[pltpu.VMEM((tm + 8, GROUP_W), F32), pltpu.VMEM((tm, GROUP_W), F32),
                        pltpu.VMEM((tm, GROUP_W), F32), pltpu.VMEM((tm, GROUP_W), F32),
                        pltpu.VMEM((1, GROUP_W), F32)],
        compiler_params=_cparams(("arbitrary", "arbitrary")),
        name="lru_prompt",
    )(zl, lw["cw"], lw["cb"], lw["wa"], lw["wx"], lw["ba"], lw["bx"], lw["lam"], lw["og"])


def _fox_prompt_kernel(q_ref, fc_ref, fct_ref, kv_ref, og_ref, o_ref, m_s, l_s, acc_s, *, tq, tk):
    qi = pl.program_id(1)
    q0 = qi * tq
    qs = _stack_heads(q_ref[...])
    pos = q0 + lax.broadcasted_iota(jnp.int32, (tq, 1), 0)
    pos_s = _rep4(pos)
    fc = fc_ref[...]
    fq_s = jnp.concatenate([fc[:, h:h + 1] for h in range(GROUP_HEADS)], axis=0)
    m_s[...] = jnp.full_like(m_s, NEG_INF)
    l_s[...] = jnp.zeros_like(l_s)
    acc_s[...] = jnp.zeros_like(acc_s)
    nk = (q0 + tq + tk - 1) // tk

    def body(kt, carry):
        k0 = pl.multiple_of(kt * tk, tk)
        k = kv_ref[pl.ds(k0, tk), 0:GROUP_W]
        v = kv_ref[pl.ds(k0, tk), GROUP_W:2 * GROUP_W]
        s = _dot_nt(qs, k)
        fct = fct_ref[0, kt]
        fc_rows = jnp.concatenate(
            [jnp.broadcast_to(fct[h:h + 1, :], (tq, tk)) for h in range(GROUP_HEADS)], axis=0)
        kpos = k0 + lax.broadcasted_iota(jnp.int32, (1, tk), 1)
        s = jnp.where(kpos <= pos_s, s + fq_s - fc_rows, NEG_INF)
        _online_update(s, v, m_s, l_s, acc_s)
        return carry

    lax.fori_loop(0, nk, body, 0)
    o = _unstack_heads(acc_s[...] * (1.0 / jnp.maximum(l_s[...], 1e-30)), tq)
    o_ref[...] = _rms_rows(o, og_ref[...]).astype(BF16)


def _fox_prompt(fq, fc, fct, fkv16, og, batch, seq, tq, tk):
    nq = seq // tq
    return pl.pallas_call(
        functools.partial(_fox_prompt_kernel, tq=tq, tk=tk),
        grid=(batch, nq),
        in_specs=[pl.BlockSpec((tq, GROUP_W), lambda b, i: (b * nq + i, 0)),
                  pl.BlockSpec((tq, LANES), lambda b, i: (b * nq + i, 0)),
                  pl.BlockSpec((1, seq // tk, GROUP_HEADS, tk), lambda b, i: (b, 0, 0, 0)),
                  pl.BlockSpec((seq, 2 * GROUP_W), lambda b, i: (b, 0)),
                  pl.BlockSpec((1, GROUP_W), lambda b, i: (0, 0))],
        out_specs=pl.BlockSpec((tq, GROUP_W), lambda b, i: (b * nq + i, 0)),
        out_shape=jax.ShapeDtypeStruct((batch * seq, GROUP_W), BF16),
        scratch_shapes=[pltpu.VMEM((GROUP_HEADS * tq, 1), F32), pltpu.VMEM((GROUP_HEADS * tq, 1), F32),
                        pltpu.VMEM((GROUP_HEADS * tq, GROUP_W), F32)],
        compiler_params=_cparams(("arbitrary", "arbitrary")),
        name="fox_prompt",
    )(fq, fc, fct, fkv16, og)


def _moba_scores(qs32, kmean):
    qh, ql = _split2(qs32)
    kh, kl = _split2(kmean)
    return _dot_nt(qh, kh) + _dot_nt(qh, kl) + _dot_nt(ql, kh)


def _moba_prompt_kernel(q_ref, kmean_ref, kv_ref, og_ref, o_ref, m_s, l_s, acc_s, *, tq, nb):
    qi = pl.program_id(1)
    q0 = qi * tq
    cur = q0 // MOBA_BLOCK
    q32 = q_ref[...]
    qs32 = _stack_heads(q32)
    qs = qs32.astype(BF16)
    pos_s = _rep4(q0 + lax.broadcasted_iota(jnp.int32, (tq, 1), 0))
    n_iota = lax.broadcasted_iota(jnp.int32, (1, nb), 1)
    sc = jnp.where(n_iota < cur, _moba_scores(qs32, kmean_ref[0]), NEG_INF)
    sel = _top_select(sc, min(MOBA_TOPK, nb))
    m_s[...] = jnp.full_like(m_s, NEG_INF)
    l_s[...] = jnp.zeros_like(l_s)
    acc_s[...] = jnp.zeros_like(acc_s)

    def body(n, carry):
        k0 = pl.multiple_of(n * MOBA_BLOCK, MOBA_BLOCK)
        k = kv_ref[pl.ds(k0, MOBA_BLOCK), 0:GROUP_W]
        v = kv_ref[pl.ds(k0, MOBA_BLOCK), GROUP_W:2 * GROUP_W]
        s = _dot_nt(qs, k)
        selcol = jnp.max(jnp.where(n_iota == n, sel, 0.0), axis=-1, keepdims=True)
        s = jnp.where(selcol > 0.5, s, NEG_INF)
        _online_update(s, v, m_s, l_s, acc_s)
        return carry

    lax.fori_loop(0, cur, body, 0)
    k0 = pl.multiple_of(cur * MOBA_BLOCK, MOBA_BLOCK)
    k = kv_ref[pl.ds(k0, MOBA_BLOCK), 0:GROUP_W]
    v = kv_ref[pl.ds(k0, MOBA_BLOCK), GROUP_W:2 * GROUP_W]
    kpos = k0 + lax.broadcasted_iota(jnp.int32, (1, MOBA_BLOCK), 1)
    s = jnp.where(kpos <= pos_s, _dot_nt(qs, k), NEG_INF)
    _online_update(s, v, m_s, l_s, acc_s)
    o = _unstack_heads(acc_s[...] * (1.0 / jnp.maximum(l_s[...], 1e-30)), tq)
    o_ref[...] = _rms_rows(o, og_ref[...]).astype(BF16)


def _moba_prompt(mq, kmean, mkv16, og, batch, seq, tq):
    nq = seq // tq
    nb = seq // MOBA_BLOCK
    return pl.pallas_call(
        functools.partial(_moba_prompt_kernel, tq=tq, nb=nb),
        grid=(batch, nq),
        in_specs=[pl.BlockSpec((tq, GROUP_W), lambda b, i: (b * nq + i, 0)),
                  pl.BlockSpec((1, nb, GROUP_W), lambda b, i: (b, 0, 0)),
                  pl.BlockSpec((seq, 2 * GROUP_W), lambda b, i: (b, 0)),
                  pl.BlockSpec((1, GROUP_W), lambda b, i: (0, 0))],
        out_specs=pl.BlockSpec((tq, GROUP_W), lambda b, i: (b * nq + i, 0)),
        out_shape=jax.ShapeDtypeStruct((batch * seq, GROUP_W), BF16),
        scratch_shapes=[pltpu.VMEM((GROUP_HEADS * tq, 1), F32), pltpu.VMEM((GROUP_HEADS * tq, 1), F32),
                        pltpu.VMEM((GROUP_HEADS * tq, GROUP_W), F32)],
        compiler_params=_cparams(("arbitrary", "arbitrary")),
        name="moba_prompt",
    )(mq, kmean, mkv16, og)


def _cmp_weights(cw):
    w = cw.reshape(2, CMP_LEN, HEAD_DIM, HEAD_DIM)
    zero = jnp.zeros((CMP_STRIDE, HEAD_DIM, HEAD_DIM), cw.dtype)

    def half(j0):
        wk = w[0, j0:j0 + CMP_STRIDE]
        wv = w[1, j0:j0 + CMP_STRIDE]
        top = jnp.concatenate([wk, zero], axis=2)
        bot = jnp.concatenate([zero, wv], axis=2)
        return jnp.concatenate([top, bot], axis=1).reshape(CMP_STRIDE * LANES, LANES)

    return half(0).astype(BF16), half(CMP_STRIDE).astype(BF16)


def _cmp_pos(cpos):
    p = jnp.concatenate([cpos[0], cpos[1]], axis=1)
    return p[:CMP_STRIDE].reshape(1, -1), p[CMP_STRIDE:].reshape(1, -1)


def _cmp_finish(y_lo, y_hi, g_ref, rep_ref, n_valid):
    rows = y_lo.shape[0]
    y = y_lo + pltpu.roll(y_hi, rows - 1, axis=0)
    lane = lax.broadcasted_iota(jnp.int32, (1, LANES), 1)
    is_k = lane < HEAD_DIM
    ms = jnp.sum(jnp.where(is_k, y * y, 0.0), axis=-1, keepdims=True) * (1.0 / HEAD_DIM)
    y = jnp.where(is_k, y * lax.rsqrt(ms + EPS) * g_ref[...], y)
    row = lax.broadcasted_iota(jnp.int32, (rows, 1), 0)
    y = jnp.where(row < n_valid, y, 0.0)
    return _dot(y.astype(BF16), rep_ref[...]).astype(BF16)


def _cmp_prompt_kernel(x_ref, plo_ref, phi_ref, wlo_ref, whi_ref, g_ref, rep_ref, o_ref, *, n_valid):
    x = x_ref[...]
    y_lo = _dot((x + plo_ref[...]).astype(BF16), wlo_ref[...])
    y_hi = _dot((x + phi_ref[...]).astype(BF16), whi_ref[...])
    o_ref[...] = _cmp_finish(y_lo, y_hi, g_ref, rep_ref, n_valid)


def _cmp_prompt(kcvc, nw, batch, seq):
    rows = seq // CMP_STRIDE
    width = CMP_STRIDE * LANES
    x2 = kcvc.reshape(batch * rows, width)
    n_valid = (seq - CMP_LEN) // CMP_STRIDE + 1
    const = lambda b: (0, 0)
    return pl.pallas_call(
        functools.partial(_cmp_prompt_kernel, n_valid=n_valid),
        grid=(batch,),
        in_specs=[pl.BlockSpec((rows, width), lambda b: (b, 0)),
                  pl.BlockSpec((1, width), const), pl.BlockSpec((1, width), const),
                  pl.BlockSpec((width, LANES), const), pl.BlockSpec((width, LANES), const),
                  pl.BlockSpec((1, LANES), const), pl.BlockSpec((LANES, 2 * GROUP_W), const)],
        out_specs=pl.BlockSpec((rows, 2 * GROUP_W), lambda b: (b, 0)),
        out_shape=jax.ShapeDtypeStruct((batch * rows, 2 * GROUP_W), BF16),
        compiler_params=_cparams(("arbitrary",)),
        name="nsa_cmp_prompt",
    )(x2, nw["plo"], nw["phi"], nw["wlo"], nw["whi"], nw["gcmp"], nw["rep"])


def _overlap_matrix(n_rows, n_valid, ns, ns_pad):
    c = np.arange(n_rows)[:, None] * CMP_STRIDE
    b = np.arange(ns_pad)[None, :] * SEL_BLOCK
    m = (c < b + SEL_BLOCK) & (c + CMP_LEN > b)
    m &= (np.arange(n_rows)[:, None] < n_valid) & (np.arange(ns_pad)[None, :] < ns)
    return m.astype(np.float32)


def _gate_expand_matrix():
    m = np.zeros((LANES, 3 * GROUP_W), np.float32)
    for h in range(GROUP_HEADS):
        for g in range(3):
            m[4 + h * 3 + g, g * GROUP_W + h * HEAD_DIM:g * GROUP_W + (h + 1) * HEAD_DIM] = 1.0
    return m


def _nsa_select(pc, ovl_ref, pos, tq, ns, n_sel):
    pcs = pc[0:tq] + pc[tq:2 * tq] + pc[2 * tq:3 * tq] + pc[3 * tq:4 * tq]
    imp = _dot_x3(pcs, ovl_ref[...])
    ns_pad = imp.shape[-1]
    j = lax.broadcasted_iota(jnp.int32, (1, ns_pad), 1)
    cur = pos // SEL_BLOCK
    forced = (j == 0) | (j == cur) | (j == cur - 1)
    imp = jnp.where(forced, jnp.inf, imp)
    imp = jnp.where((j * SEL_BLOCK > pos) | (j >= ns), NEG_INF, imp)
    return _top_select(imp, n_sel)


def _nsa_finish(o_c, o_s, o_w, small, gexp_ref, og_ref, tq):
    gx = _dot_x3(small, gexp_ref[...])
    o = (gx[:, 0:GROUP_W] * _unstack_heads(o_c, tq) + gx[:, GROUP_W:2 * GROUP_W] * _unstack_heads(o_s, tq)
         + gx[:, 2 * GROUP_W:3 * GROUP_W] * _unstack_heads(o_w, tq))
    return _rms_rows(o, og_ref[...]).astype(BF16)


def _nsa_prompt_kernel(qn_ref, qr_ref, small_ref, cmp_ref, ovl_ref, gexp_ref, sel_ref,
                       w0_ref, w1_ref, w2_ref, w3_ref, w4_ref, og_ref, o_ref,
                       m_s, l_s, acc_s, *, tq, tk, ns, n_sel):
    qi = pl.program_id(1)
    q0 = qi * tq
    pos = q0 + lax.broadcasted_iota(jnp.int32, (tq, 1), 0)
    pos_s = _rep4(pos)
    qn_s = _stack_heads(qn_ref[...])
    qr_s = _stack_heads(qr_ref[...])

    nc = cmp_ref.shape[0]
    lc = _dot_nt(qn_s, cmp_ref[:, 0:GROUP_W])
    cend = lax.broadcasted_iota(jnp.int32, (1, nc), 1) * CMP_STRIDE + (CMP_LEN - 1)
    pc = _masked_softmax(lc, cend <= pos_s)
    o_c = _dot(pc.astype(BF16), cmp_ref[:, GROUP_W:2 * GROUP_W])

    sel = _nsa_select(pc, ovl_ref, pos, tq, ns, n_sel)
    sel16 = sel.astype(BF16)
    ns_pad = sel.shape[-1]

    m_s[...] = jnp.full_like(m_s, NEG_INF)
    l_s[...] = jnp.zeros_like(l_s)
    acc_s[...] = jnp.zeros_like(acc_s)
    nk = (q0 + tq + tk - 1) // tk
    blk_iota = lax.broadcasted_iota(jnp.int32, (ns_pad, 1), 0)
    key_blk = lax.broadcasted_iota(jnp.int32, (1, tk), 1) // SEL_BLOCK

    def body(kt, carry):
        k0 = pl.multiple_of(kt * tk, tk)
        k = sel_ref[pl.ds(k0, tk), 0:GROUP_W]
        v = sel_ref[pl.ds(k0, tk), GROUP_W:2 * GROUP_W]
        s = _dot_nt(qr_s, k)
        expand = (blk_iota == kt * (tk // SEL_BLOCK) + key_blk).astype(BF16)
        chosen = _dot(sel16, expand)
        kpos = k0 + lax.broadcasted_iota(jnp.int32, (1, tk), 1)
        mask = (chosen > 0.5) & (kpos <= pos)
        s = jnp.where(_rep4(mask), s, NEG_INF)
        _online_update(s, v, m_s, l_s, acc_s)
        return carry

    lax.fori_loop(0, nk, body, 0)
    o_s = acc_s[...] * (1.0 / jnp.maximum(l_s[...], 1e-30))

    wrefs = (w0_ref, w1_ref, w2_ref, w3_ref, w4_ref)
    logits, masks = [], []
    for j, wr in enumerate(wrefs):
        kpos = q0 + (j - 4) * tq + lax.broadcasted_iota(jnp.int32, (1, tq), 1)
        logits.append(_dot_nt(qr_s, wr[:, 0:GROUP_W]))
        masks.append((kpos >= 0) & (kpos <= pos_s) & (kpos >= pos_s - WINDOW))
    pw = _masked_softmax(jnp.concatenate(logits, axis=1), jnp.concatenate(masks, axis=1)).astype(BF16)
    o_w = jnp.zeros((GROUP_HEADS * tq, GROUP_W), F32)
    for j, wr in enumerate(wrefs):
        o_w = o_w + _dot(pw[:, j * tq:(j + 1) * tq], wr[:, GROUP_W:2 * GROUP_W])

    o_ref[...] = _nsa_finish(o_c, o_s, o_w, small_ref[...], gexp_ref, og_ref, tq)


def _nsa_prompt(r, cmp_rep, nw, og, batch, seq, tq, tk):
    assert tq == LANES and WINDOW == 4 * tq
    nq = seq // tq
    nc = seq // CMP_STRIDE
    ns = seq // SEL_BLOCK
    ns_pad = -(-ns // LANES) * LANES
    n_valid = (seq - CMP_LEN) // CMP_STRIDE + 1
    ovl = jnp.asarray(_overlap_matrix(nc, n_valid, ns, ns_pad), BF16)
    gexp = jnp.asarray(_gate_expand_matrix(), BF16)
    const = lambda b, i: (0, 0)
    row = lambda b, i: (b * nq + i, 0)

    def win_spec(j):
        return pl.BlockSpec((tq, 2 * GROUP_W), lambda b, i: (b * nq + jnp.maximum(i + j - 4, 0), 0))

    return pl.pallas_call(
        functools.partial(_nsa_prompt_kernel, tq=tq, tk=tk, ns=ns, n_sel=min(SEL_TOPK, ns)),
        grid=(batch, nq),
        in_specs=[pl.BlockSpec((tq, GROUP_W), row), pl.BlockSpec((tq, GROUP_W), row),
                  pl.BlockSpec((tq, LANES), row),
                  pl.BlockSpec((nc, 2 * GROUP_W), lambda b, i: (b, 0)),
                  pl.BlockSpec((nc, ns_pad), const),
                  pl.BlockSpec((LANES, 3 * GROUP_W), const),
                  pl.BlockSpec((seq, 2 * GROUP_W), lambda b, i: (b, 0))]
                 + [win_spec(j) for j in range(5)]
                 + [pl.BlockSpec((1, GROUP_W), const)],
        out_specs=pl.BlockSpec((tq, GROUP_W), row),
        out_shape=jax.ShapeDtypeStruct((batch * seq, GROUP_W), BF16),
        scratch_shapes=[pltpu.VMEM((GROUP_HEADS * tq, 1), F32), pltpu.VMEM((GROUP_HEADS * tq, 1), F32),
                        pltpu.VMEM((GROUP_HEADS * tq, GROUP_W), F32)],
        compiler_params=_cparams(("arbitrary", "arbitrary")),
        name="nsa_prompt",
    )(r["nqn"], r["nqr"], r["small"], cmp_rep, ovl, gexp, r["nsel16"],
      r["nwin16"], r["nwin16"], r["nwin16"], r["nwin16"], r["nwin16"], og)


def _out_proj_kernel(x_ref, o0_ref, o1_ref, o2_ref, o3_ref, w_ref, gt_ref, sc_ref, sh_ref, g_ref,
                     x1_ref, hf_ref):
    o_cat = jnp.concatenate([o0_ref[...], o1_ref[...], o2_ref[...], o3_ref[...]], axis=1).astype(BF16)
    x1 = x_ref[...] + gt_ref[0] * _dot(o_cat, w_ref[...])
    x1_ref[...] = x1
    hf_ref[...] = (_rms_rows(x1, g_ref[...]) * (1.0 + sc_ref[0]) + sh_ref[0]).astype(BF16)


def _out_proj(x, outs, w_out, gt1, sc2, sh2, g2, tiles_per_seq, tm):
    rows = x.shape[0]
    tb = gt1.shape[1]
    mod_map = (lambda i: (i // tiles_per_seq, 0, 0)) if tb == 1 else (lambda i: (i, 0, 0))
    row = lambda i: (i, 0)
    const = lambda i: (0, 0)
    mod = pl.BlockSpec((1, tb, D_MODEL), mod_map)
    return pl.pallas_call(
        _out_proj_kernel,
        grid=(rows // tm,),
        in_specs=[pl.BlockSpec((tm, D_MODEL), row)] + [pl.BlockSpec((tm, GROUP_W), row)] * 4
                 + [pl.BlockSpec((D_MODEL, D_MODEL), const), mod, mod, mod, pl.BlockSpec((1, D_MODEL), const)],
        out_specs=[pl.BlockSpec((tm, D_MODEL), row), pl.BlockSpec((tm, D_MODEL), row)],
        out_shape=[jax.ShapeDtypeStruct((rows, D_MODEL), F32), jax.ShapeDtypeStruct((rows, D_MODEL), BF16)],
        compiler_params=_cparams(("arbitrary",)),
        name="out_proj",
    )(x, *outs, w_out, gt1, sc2, sh2, g2)


def _mlp_kernel(x1_ref, hf_ref, wu_ref, wd_ref, gt_ref, y_ref, acc_s):
    f = pl.program_id(1)

    @pl.when(f == 0)
    def _():
        acc_s[...] = jnp.zeros_like(acc_s)

    up = jnp.maximum(_dot(hf_ref[...], wu_ref[...]), 0.0)
    acc_s[...] += _dot((up * up).astype(BF16), wd_ref[...])

    @pl.when(f == pl.num_programs(1) - 1)
    def _():
        y_ref[...] = x1_ref[...] + gt_ref[0] * acc_s[...]


def _mlp(x1, hf, w_up, w_down, gt2, tiles_per_seq, tm, tf):
    rows = x1.shape[0]
    tb = gt2.shape[1]
    mod_map = (lambda i, f: (i // tiles_per_seq, 0, 0)) if tb == 1 else (lambda i, f: (i, 0, 0))
    row = lambda i, f: (i, 0)
    return pl.pallas_call(
        _mlp_kernel,
        grid=(rows // tm, D_FF // tf),
        in_specs=[pl.BlockSpec((tm, D_MODEL), row), pl.BlockSpec((tm, D_MODEL), row),
                  pl.BlockSpec((D_MODEL, tf), lambda i, f: (0, f)),
                  pl.BlockSpec((tf, D_MODEL), lambda i, f: (f, 0)),
                  pl.BlockSpec((1, tb, D_MODEL), mod_map)],
        out_specs=pl.BlockSpec((tm, D_MODEL), row),
        out_shape=jax.ShapeDtypeStruct((rows, D_MODEL), F32),
        scratch_shapes=[pltpu.VMEM((tm, D_MODEL), F32)],
        compiler_params=_cparams(("arbitrary", "arbitrary")),
        name="mlp",
    )(x1, hf, w_up, w_down, gt2)


def _block_diag(w):
    out = jnp.zeros((GROUP_W, GROUP_W), w.dtype)
    for h in range(GROUP_HEADS):
        out = out.at[h * HEAD_DIM:(h + 1) * HEAD_DIM, h * HEAD_DIM:(h + 1) * HEAD_DIM].set(w[h])
    return out


def _layer_weights(p, l):
    tile4 = lambda g: jnp.tile(g, GROUP_HEADS)
    ones = jnp.ones((HEAD_DIM,), F32)
    w_in = jnp.concatenate([p["w_in"][l][:, _in_perm()],
                            jnp.zeros((D_MODEL, IN_W - _in_perm().shape[0]), F32)], axis=1).astype(BF16)
    ng = p["nsa_qk_g"][l]
    gains = jnp.stack([tile4(p["fox_qk_g"][l, 0]), tile4(p["fox_qk_g"][l, 1]),
                       tile4(p["moba_qk_g"][l, 0]), tile4(p["moba_qk_g"][l, 1]),
                       tile4(ng[0]), jnp.concatenate([ng[2], ones, ng[3], ones]),
                       jnp.ones((GROUP_W,), F32), jnp.ones((GROUP_W,), F32)])
    bf_row = jnp.zeros((1, LANES), F32).at[0, :GROUP_HEADS].set(p["fox_bf"][l])
    row = lambda a: a.reshape(1, -1)
    lru = dict(cw=p["lru_conv_w"][l], cb=row(p["lru_conv_b"][l]),
               wa=_block_diag(p["lru_wa"][l]).astype(BF16), wx=_block_diag(p["lru_wx"][l]).astype(BF16),
               ba=row(p["lru_ba"][l]), bx=row(p["lru_bx"][l]), lam=row(p["lru_lambda"][l]),
               og=row(p["out_g"][l, 0]))
    wlo, whi = _cmp_weights(p["nsa_cmp_w"][l])
    plo, phi = _cmp_pos(p["nsa_cmp_pos"][l])
    gcmp = jnp.concatenate([ng[1], ones]).reshape(1, LANES)
    nsa = dict(wlo=wlo, whi=whi, plo=plo, phi=phi, gcmp=gcmp, rep=jnp.asarray(_rep_matrix(), BF16))
    return dict(w_in=w_in, gains=gains, bf_row=bf_row, lru=lru, nsa=nsa,
                g1=row(p["norm_g"][l, 0]), g2=row(p["norm_g"][l, 1]),
                og=[row(p["out_g"][l, k]) for k in range(N_MIXERS)],
                w_out=p["w_out"][l].astype(BF16), w_up=p["w_up"][l].astype(BF16),
                w_down=p["w_down"][l].astype(BF16))


def _prompt_layer(x, mod, lw, batch, seq, tabs):
    tm = min(512, seq)
    tq, tk = LANES, 256
    tps = seq // tm
    sh1, sc1, gt1, sh2, sc2, gt2 = [m[:, None, :] for m in jnp.split(mod, 6, axis=-1)]
    r = _in_proj(x, sc1, sh1, lw["g1"], lw["w_in"], tabs, lw["gains"], lw["bf_row"],
                 tiles_per_seq=tps, prompt=True, tm=tm)
    o_lru, h_t, conv_t = _lru_prompt(r["zl"], lw["lru"], batch, seq, tm)
    fct = r["fc"][:, :GROUP_HEADS].reshape(batch, seq // tk, tk, GROUP_HEADS).transpose(0, 1, 3, 2)
    o_fox = _fox_prompt(r["fq"], r["fc"], fct, r["fkv16"], lw["og"][1], batch, seq, tq, tk)
    kmean = r["kmean"].reshape(batch, seq // MOBA_BLOCK, GROUP_W)
    o_moba = _moba_prompt(r["mq"], kmean, r["mkv16"], lw["og"][2], batch, seq, tq)
    cmp_rep = _cmp_prompt(r["kcvc"], lw["nsa"], batch, seq)
    o_nsa = _nsa_prompt(r, cmp_rep, lw["nsa"], lw["og"][3], batch, seq, tq, tk)
    x1, hf = _out_proj(x, (o_lru, o_fox, o_moba, o_nsa), lw["w_out"], gt1, sc2, sh2, lw["g2"], tps, tm)
    y = _mlp(x1, hf, lw["w_up"], lw["w_down"], gt2, tps, tm, 1024)
    keep = min(WINDOW, seq)
    state = (r["fkv"].reshape(batch, seq, 2, GROUP_HEADS, HEAD_DIM),
             r["small"][:, :GROUP_HEADS].reshape(batch, seq, GROUP_HEADS),
             r["mkv"].reshape(batch, seq, 2, GROUP_HEADS, HEAD_DIM),
             r["nkv"].reshape(batch, seq, 4, HEAD_DIM),
             r["nwin"].reshape(batch, seq, 2, HEAD_DIM)[:, seq - keep:],
             h_t.reshape(batch, GROUP_W),
             conv_t[:, 8 - (CONV_W - 1):, :])
    return y, state


def _prompt_trunk(x_prompt, mods, lws):
    batch, seq, _ = x_prompt.shape
    tabs = _rope_tables(jnp.arange(seq, dtype=jnp.int32))
    x = x_prompt.reshape(batch * seq, D_MODEL)
    states = []
    for l, lw in enumerate(lws):
        x, st = _prompt_layer(x, mods[l], lw, batch, seq, tabs)
        states.append(st)
    stacked = [jnp.stack([s[i] for s in states], axis=0) for i in range(len(states[0]))]
    return x.reshape(batch, seq, D_MODEL), stacked


def _lru_sample_kernel(zl_ref, h0_ref, cbuf_ref, cw_ref, cb_ref, wa_ref, wx_ref, ba_ref, bx_ref, lam_ref,
                       og_ref, o_ref, ht_ref, cnew_ref, *, n_seq, t_real):
    o_ref[...] = jnp.zeros_like(o_ref)
    xs = [cbuf_ref[k] for k in range(CONV_W - 1)] + [zl_ref[t, :, 0:GROUP_W] for t in range(t_real)]
    h = h0_ref[...]
    for t in range(t_real):
        u = cb_ref[...]
        for k in range(CONV_W):
            u = u + xs[t + k] * cw_ref[k:k + 1, :]
        a, b = _lru_coeffs(u, wa_ref, wx_ref, ba_ref, bx_ref, lam_ref)
        h = a * h + b
        o = h * _gelu_tanh(zl_ref[t, :, GROUP_W:2 * GROUP_W])
        o_ref[t] = _rms_rows(o, og_ref[...])
    ht_ref[...] = h
    for k in range(CONV_W - 1):
        cnew_ref[k] = xs[t_real + k]


def _lru_sample(zl, h0, cbuf, lw, n_seq, t_real):
    assert t_real >= CONV_W - 1
    zl_t = jnp.transpose(zl.reshape(n_seq, TS, 2 * GROUP_W), (1, 0, 2))
    o_t, h_t, conv_t = pl.pallas_call(
        functools.partial(_lru_sample_kernel, n_seq=n_seq, t_real=t_real),
        out_shape=[jax.ShapeDtypeStruct((TS, n_seq, GROUP_W), F32),
                   jax.ShapeDtypeStruct((n_seq, GROUP_W), F32),
                   jax.ShapeDtypeStruct((CONV_W - 1, n_seq, GROUP_W), F32)],
        compiler_params=pltpu.CompilerParams(vmem_limit_bytes=VMEM_LIMIT),
        name="lru_sample",
    )(zl_t, h0, cbuf, lw["cw"], lw["cb"], lw["wa"], lw["wx"], lw["ba"], lw["bx"], lw["lam"], lw["og"])
    return jnp.transpose(o_t, (1, 0, 2)).reshape(n_seq * TS, GROUP_W), h_t, conv_t


def _page_specs(tail, n_pages):
    zeros = (0,) * len(tail)
    return [pl.BlockSpec((1,) + tail, functools.partial(lambda j, s, pt: (pt[s, j],) + zeros, j))
            for j in range(n_pages)]


def _seq_spec(width, rows=TS):
    return pl.BlockSpec((rows, width), lambda s, pt: (s, 0))


def _const_spec(shape):
    zeros = (0,) * len(shape)
    return pl.BlockSpec(shape, lambda s, pt: zeros)


def _pad_rows(a, rows):
    return jnp.concatenate([a, jnp.zeros((rows - a.shape[0], a.shape[1]), a.dtype)], axis=0)


def _rep_rows(a, t):
    return jnp.concatenate([jnp.broadcast_to(a[h:h + 1, :], (t, a.shape[1])) for h in range(GROUP_HEADS)], axis=0)


def _sample_pos(past):
    return _rep4(past + lax.broadcasted_iota(jnp.int32, (TS, 1), 0))


def _fox_sample_kernel(pt_ref, q_ref, kvn_ref, lfn_ref, u_ref, og_ref, *refs, n_pages, page):
    kv_refs = refs[:n_pages]
    lf_refs = refs[n_pages:2 * n_pages]
    o_ref = refs[2 * n_pages]
    past = n_pages * page
    qs = _stack_heads(q_ref[...]).astype(BF16)
    u = u_ref[...]
    off = jnp.zeros((GROUP_HEADS, 1), F32)
    logits, fcs, vals = [], [], []
    for j in range(n_pages):
        c = _dot_x3(lf_refs[j][0], u) + off
        off = c[:, page - 1:page]
        fcs.append(c)
        kv = kv_refs[j][0].astype(BF16)
        logits.append(_dot_nt(qs, kv[:, 0:GROUP_W]))
        vals.append(kv[:, GROUP_W:2 * GROUP_W])
    cn = _dot_x3(lfn_ref[0], u) + off
    fcs.append(cn)
    kvn = _pad_rows(kvn_ref[...], LANES).astype(BF16)
    logits.append(_dot_nt(qs, kvn[:, 0:GROUP_W]))
    vals.append(kvn[:, GROUP_W:2 * GROUP_W])

    lane = lax.broadcasted_iota(jnp.int32, (1, LANES), 1)
    t_row = lax.broadcasted_iota(jnp.int32, (GROUP_HEADS * TS, 1), 0) % TS
    fq = jnp.sum(jnp.where(lane == t_row, _rep_rows(cn, TS), 0.0), axis=-1, keepdims=True)
    s = jnp.concatenate(logits, axis=1) + fq - jnp.concatenate([_rep_rows(c, TS) for c in fcs], axis=1)
    kpos = lax.broadcasted_iota(jnp.int32, (1, s.shape[1]), 1)
    p = _masked_softmax(s, kpos <= _sample_pos(past)).astype(BF16)
    o = jnp.zeros((GROUP_HEADS * TS, GROUP_W), F32)
    for j, v in enumerate(vals):
        o = o + _dot(p[:, j * LANES:(j + 1) * LANES], v)
    o_ref[...] = _rms_rows(_unstack_heads(o, TS), og_ref[...])


def _fox_sample(page_table, fq, fkv_new, lf_new_t, pool_kv, pool_lf_t, og):
    n_seq, n_pages = page_table.shape
    page = pool_kv.shape[1]
    assert page == LANES
    u = jnp.asarray(np.triu(np.ones((page, page), np.float32)), BF16)
    grid_spec = pltpu.PrefetchScalarGridSpec(
        num_scalar_prefetch=1, grid=(n_seq,),
        in_specs=[_seq_spec(GROUP_W), _seq_spec(2 * GROUP_W),
                  pl.BlockSpec((1, GROUP_HEADS, LANES), lambda s, pt: (s, 0, 0)),
                  _const_spec((page, page)), _const_spec((1, GROUP_W))]
                 + _page_specs((page, 2 * GROUP_W), n_pages) + _page_specs((GROUP_HEADS, page), n_pages),
        out_specs=_seq_spec(GROUP_W))
    return pl.pallas_call(
        functools.partial(_fox_sample_kernel, n_pages=n_pages, page=page),
        grid_spec=grid_spec,
        out_shape=jax.ShapeDtypeStruct((n_seq * TS, GROUP_W), F32),
        compiler_params=_cparams(("arbitrary",)),
        name="fox_sample",
    )(page_table, fq, fkv_new, lf_new_t, u, og, *([pool_kv] * n_pages), *([pool_lf_t] * n_pages))


def _moba_sample_kernel(pt_ref, q_ref, kvn_ref, og_ref, *refs, n_pages, page):
    kv_refs = refs[:n_pages]
    o_ref = refs[n_pages]
    past = n_pages * page
    per_blk = MOBA_BLOCK // page
    nb = n_pages // per_blk
    qs32 = _stack_heads(q_ref[...])
    qs = qs32.astype(BF16)
    logits, vals, means = [], [], []
    for n in range(nb):
        ksum = jnp.zeros((1, GROUP_W), F32)
        for j in range(n * per_blk, (n + 1) * per_blk):
            kv = kv_refs[j][0]
            ksum = ksum + jnp.sum(kv[:, 0:GROUP_W], axis=0, keepdims=True)
            kv16 = kv.astype(BF16)
            logits.append(_dot_nt(qs, kv16[:, 0:GROUP_W]))
            vals.append(kv16[:, GROUP_W:2 * GROUP_W])
        means.append(ksum * (1.0 / MOBA_BLOCK))
    kmean = _pad_rows(jnp.concatenate(means, axis=0), LANES)
    n_iota = lax.broadcasted_iota(jnp.int32, (1, LANES), 1)
    sc = jnp.where(n_iota < nb, _moba_scores(qs32, kmean), NEG_INF)
    sel = _top_select(sc, min(MOBA_TOPK, nb + 1))
    masks = []
    for j in range(n_pages):
        n = j // per_blk
        masks.append(jnp.broadcast_to(sel[:, n:n + 1] > 0.5, (GROUP_HEADS * TS, page)))
    kvn = _pad_rows(kvn_ref[...], LANES).astype(BF16)
    logits.append(_dot_nt(qs, kvn[:, 0:GROUP_W]))
    vals.append(kvn[:, GROUP_W:2 * GROUP_W])
    kpos_new = past + lax.broadcasted_iota(jnp.int32, (1, LANES), 1)
    masks.append(kpos_new <= _sample_pos(past))
    p = _masked_softmax(jnp.concatenate(logits, axis=1), jnp.concatenate(masks, axis=1)).astype(BF16)
    o = jnp.zeros((GROUP_HEADS * TS, GROUP_W), F32)
    for j, v in enumerate(vals):
        o = o + _dot(p[:, j * LANES:(j + 1) * LANES], v)
    o_ref[...] = _rms_rows(_unstack_heads(o, TS), og_ref[...])


def _moba_sample(page_table, mq, mkv_new, pool_kv, og):
    n_seq, n_pages = page_table.shape
    page = pool_kv.shape[1]
    assert page == LANES and (n_pages * page) % MOBA_BLOCK == 0 and TS <= MOBA_BLOCK
    grid_spec = pltpu.PrefetchScalarGridSpec(
        num_scalar_prefetch=1, grid=(n_seq,),
        in_specs=[_seq_spec(GROUP_W), _seq_spec(2 * GROUP_W), _const_spec((1, GROUP_W))]
                 + _page_specs((page, 2 * GROUP_W), n_pages),
        out_specs=_seq_spec(GROUP_W))
    return pl.pallas_call(
        functools.partial(_moba_sample_kernel, n_pages=n_pages, page=page),
        grid_spec=grid_spec,
        out_shape=jax.ShapeDtypeStruct((n_seq * TS, GROUP_W), F32),
        compiler_params=_cparams(("arbitrary",)),
        name="moba_sample",
    )(page_table, mq, mkv_new, og, *([pool_kv] * n_pages))


def _nsa_sample_kernel(pt_ref, qn_ref, qr_ref, small_ref, kvn_ref, winn_ref, win_ref, plo_ref, phi_ref, wlo_ref,
                       whi_ref, g_ref, rep_ref, ovl_ref, gexp_ref, og_ref, *refs, n_pages, page, ns, n_sel):
    pg_refs = refs[:n_pages]
    o_ref = refs[n_pages]
    x2_s = refs[n_pages + 1]
    past = n_pages * page
    per_pg = page // CMP_STRIDE
    pos = past + lax.broadcasted_iota(jnp.int32, (TS, 1), 0)
    pos_s = _rep4(pos)
    qn_s = _stack_heads(qn_ref[...]).astype(BF16)
    qr_s = _stack_heads(qr_ref[...]).astype(BF16)
    rep = rep_ref[...]

    for p in range(n_pages):
        for j in range(CMP_STRIDE):
            x2_s[p * per_pg:(p + 1) * per_pg, j * LANES:(j + 1) * LANES] = (
                pg_refs[p][0, pl.ds(2 * j, per_pg, stride=2 * CMP_STRIDE), :])
    x = x2_s[...]
    y_lo = _dot((x + plo_ref[...]).astype(BF16), wlo_ref[...])
    y_hi = _dot((x + phi_ref[...]).astype(BF16), whi_ref[...])
    n_chunks = n_pages * per_pg
    cmp_rep = _cmp_finish(y_lo, y_hi, g_ref, rep_ref, n_chunks - 1)

    lc = _dot_nt(qn_s, cmp_rep[:, 0:GROUP_W])
    cend = lax.broadcasted_iota(jnp.int32, (1, n_chunks), 1) * CMP_STRIDE + (CMP_LEN - 1)
    pc = _masked_softmax(lc, cend <= pos_s)
    o_c = _dot(pc.astype(BF16), cmp_rep[:, GROUP_W:2 * GROUP_W])
    sel = _nsa_select(pc, ovl_ref, pos, TS, ns, n_sel)

    lane = lax.broadcasted_iota(jnp.int32, (1, page), 1)
    per_blk = page // SEL_BLOCK
    logits, masks, vals = [], [], []
    for p in range(n_pages):
        kv = _dot(pg_refs[p][0, pl.ds(1, page, stride=2), :].astype(BF16), rep).astype(BF16)
        logits.append(_dot_nt(qr_s, kv[:, 0:GROUP_W]))
        vals.append(kv[:, GROUP_W:2 * GROUP_W])
        chosen = jnp.zeros((TS, page), F32)
        for b in range(per_blk):
            n = p * per_blk + b
            chosen = jnp.where(lane // SEL_BLOCK == b, sel[:, n:n + 1], chosen)
        masks.append(_rep4(chosen > 0.5))
    kvn = _dot(_pad_rows(kvn_ref[:, LANES:2 * LANES], LANES).astype(BF16), rep).astype(BF16)
    logits.append(_dot_nt(qr_s, kvn[:, 0:GROUP_W]))
    vals.append(kvn[:, GROUP_W:2 * GROUP_W])
    kpos_new = past + lax.broadcasted_iota(jnp.int32, (1, LANES), 1)
    n_new = past // SEL_BLOCK
    chosen_new = (kpos_new // SEL_BLOCK == n_new) & (sel[:, n_new:n_new + 1] > 0.5)
    for b in range(1, -(-TS // SEL_BLOCK)):
        chosen_new = chosen_new | ((kpos_new // SEL_BLOCK == n_new + b) & (sel[:, n_new + b:n_new + b + 1] > 0.5))
    masks.append(_rep4(chosen_new) & (kpos_new <= pos_s))
    ps = _masked_softmax(jnp.concatenate(logits, axis=1), jnp.concatenate(masks, axis=1)).astype(BF16)
    o_s = jnp.zeros((GROUP_HEADS * TS, GROUP_W), F32)
    for j, v in enumerate(vals):
        o_s = o_s + _dot(ps[:, j * LANES:(j + 1) * LANES], v)

    wb = win_ref.shape[1]
    kvw = _dot(win_ref[0].astype(BF16), rep).astype(BF16)
    kvwn = _dot(_pad_rows(winn_ref[...], LANES).astype(BF16), rep).astype(BF16)
    kpos = jnp.concatenate([past - wb + lax.broadcasted_iota(jnp.int32, (1, wb), 1), kpos_new], axis=1)
    lw = jnp.concatenate([_dot_nt(qr_s, kvw[:, 0:GROUP_W]), _dot_nt(qr_s, kvwn[:, 0:GROUP_W])], axis=1)
    wmask = (kpos >= past - wb) & (kpos <= pos_s) & (kpos >= pos_s - WINDOW)
    pw = _masked_softmax(lw, wmask).astype(BF16)
    o_w = _dot(pw[:, 0:wb], kvw[:, GROUP_W:2 * GROUP_W]) + _dot(pw[:, wb:wb + LANES], kvwn[:, GROUP_W:2 * GROUP_W])

    o_ref[...] = _nsa_finish(o_c, o_s, o_w, small_ref[...], gexp_ref, og_ref, TS).astype(F32)


def _nsa_sample(page_table, r, win_state, pool_kv, nw, og):
    n_seq, n_pages = page_table.shape
    page = pool_kv.shape[1] // 2
    past = n_pages * page
    assert page == LANES and TS < CMP_STRIDE and past % MOBA_BLOCK == 0
    n_chunks = past // CMP_STRIDE
    ns = (-(-(past + TS) // MOBA_BLOCK) * MOBA_BLOCK) // SEL_BLOCK
    ns_pad = -(-ns // LANES) * LANES
    ovl = jnp.asarray(_overlap_matrix(n_chunks, n_chunks - 1, ns, ns_pad), BF16)
    gexp = jnp.asarray(_gate_expand_matrix(), BF16)
    width = CMP_STRIDE * LANES
    wb = win_state.shape[1]
    grid_spec = pltpu.PrefetchScalarGridSpec(
        num_scalar_prefetch=1, grid=(n_seq,),
        in_specs=[_seq_spec(GROUP_W), _seq_spec(GROUP_W), _seq_spec(LANES), _seq_spec(GROUP_W), _seq_spec(LANES),
                  pl.BlockSpec((1, wb, LANES), lambda s, pt: (s, 0, 0)),
                  _const_spec((1, width)), _const_spec((1, width)),
                  _const_spec((width, LANES)), _const_spec((width, LANES)),
                  _const_spec((1, LANES)), _const_spec((LANES, 2 * GROUP_W)),
                  _const_spec((n_chunks, ns_pad)), _const_spec((LANES, 3 * GROUP_W)), _const_spec((1, GROUP_W))]
                 + _page_specs((2 * page, LANES), n_pages),
        out_specs=_seq_spec(GROUP_W),
        scratch_shapes=[pltpu.VMEM((n_chunks, width), F32)])
    return pl.pallas_call(
        functools.partial(_nsa_sample_kernel, n_pages=n_pages, page=page, ns=ns, n_sel=min(SEL_TOPK, ns)),
        grid_spec=grid_spec,
        out_shape=jax.ShapeDtypeStruct((n_seq * TS, GROUP_W), F32),
        compiler_params=_cparams(("arbitrary",)),
        name="nsa_sample",
    )(page_table, r["nqn"], r["nqr"], r["small"], r["nkv"], r["nwin"], win_state,
      nw["plo"], nw["phi"], nw["wlo"], nw["whi"], nw["gcmp"], nw["rep"], ovl, gexp, og, *([pool_kv] * n_pages))


def _sample_layer(x, mod_rows, lw, caches, page_table, n_seq, t_real, tabs):
    pool_fox, pool_lf_t, pool_moba, pool_nsa, win_state, h0, cbuf = caches
    rows = n_seq * TS
    tm = min(512, rows)
    sh1, sc1, gt1, sh2, sc2, gt2 = [m.reshape(rows // tm, tm, D_MODEL) for m in jnp.split(mod_rows, 6, axis=-1)]
    r = _in_proj(x, sc1, sh1, lw["g1"], lw["w_in"], tabs, lw["gains"], lw["bf_row"],
                 tiles_per_seq=1, prompt=False, tm=tm)
    o_lru, h_t, conv_t = _lru_sample(r["zl"], h0, cbuf, lw["lru"], n_seq, t_real)
    lf_new = r["small"][:, :GROUP_HEADS].reshape(n_seq, TS, GROUP_HEADS)
    lf_new_t = jnp.pad(jnp.transpose(lf_new, (0, 2, 1)), ((0, 0), (0, 0), (0, LANES - TS)))
    o_fox = _fox_sample(page_table, r["fq"], r["fkv"], lf_new_t, pool_fox, pool_lf_t, lw["og"][1])
    o_moba = _moba_sample(page_table, r["mq"], r["mkv"], pool_moba, lw["og"][2])
    o_nsa = _nsa_sample(page_table, r, win_state, pool_nsa, lw["nsa"], lw["og"][3])
    x1, hf = _out_proj(x, (o_lru, o_fox, o_moba, o_nsa), lw["w_out"], gt1, sc2, sh2, lw["g2"], 1, tm)
    y = _mlp(x1, hf, lw["w_up"], lw["w_down"], gt2, 1, tm, 1024)
    real = lambda a, tail: a.reshape((n_seq, TS) + tail)[:, :t_real]
    win_new = real(r["nwin"], (2, HEAD_DIM))
    win_prev = win_state.reshape(n_seq, -1, 2, HEAD_DIM)
    keep = min(WINDOW, win_prev.shape[1] + t_real)
    win_all = jnp.concatenate([win_prev, win_new], axis=1)
    state = (real(r["fkv"], (2, GROUP_HEADS, HEAD_DIM)), real(r["small"][:, :GROUP_HEADS], (GROUP_HEADS,)),
             real(r["mkv"], (2, GROUP_HEADS, HEAD_DIM)), real(r["nkv"], (4, HEAD_DIM)),
             win_all[:, win_all.shape[1] - keep:], h_t, jnp.transpose(conv_t, (1, 0, 2)))
    return y, state


def _sample_trunk(x_sample, mods, lws, caches, page_table):
    n_seq, t_real, _ = x_sample.shape
    assert t_real <= TS
    page = caches["fox_kv"].shape[2]
    past = page_table.shape[1] * page
    rows = n_seq * TS
    x = jnp.pad(x_sample, ((0, 0), (0, TS - t_real), (0, 0))).reshape(rows, D_MODEL)
    tm = min(512, rows)
    tabs = _rope_tables(past + (jnp.arange(tm, dtype=jnp.int32) % TS))
    states = []
    for l, lw in enumerate(lws):
        n_phys = caches["fox_kv"].shape[1]
        layer_caches = (caches["fox_kv"][l].reshape(n_phys, page, 2 * GROUP_W),
                        jnp.transpose(caches["fox_logf"][l], (0, 2, 1)),
                        caches["moba_kv"][l].reshape(n_phys, page, 2 * GROUP_W),
                        caches["nsa_kv"][l].reshape(n_phys, 2 * page, LANES),
                        caches["nsa_win"][l].reshape(n_seq, -1, 2 * HEAD_DIM),
                        caches["lru_h"][l],
                        jnp.transpose(caches["lru_conv"][l], (1, 0, 2)))
        mod_rows = jnp.repeat(mods[l], TS, axis=0)
        x, st = _sample_layer(x, mod_rows, lw, layer_caches, page_table, n_seq, t_real, tabs)
        states.append(st)
    stacked = [jnp.stack([s[i] for s in states], axis=0) for i in range(len(states[0]))]
    return x.reshape(n_seq, TS, D_MODEL)[:, :t_real], stacked


def kernel(x_prompt, x_sample, cache_fox_kv, cache_fox_logf, cache_moba_kv, cache_nsa_kv, state_nsa_win,
           state_lru_h, state_lru_conv, page_table, c_prompt, c_sample, norm_g, w_ada, b_ada, w_in,
           lru_conv_w, lru_conv_b, lru_wa, lru_ba, lru_wx, lru_bx, lru_lambda, fox_bf, fox_qk_g, moba_qk_g,
           nsa_qk_g, nsa_cmp_pos, nsa_cmp_w, out_g, w_out, w_up, w_down):
    p = dict(norm_g=norm_g, w_in=w_in, lru_conv_w=lru_conv_w, lru_conv_b=lru_conv_b, lru_wa=lru_wa,
             lru_ba=lru_ba, lru_wx=lru_wx, lru_bx=lru_bx, lru_lambda=lru_lambda, fox_bf=fox_bf,
             fox_qk_g=fox_qk_g, moba_qk_g=moba_qk_g, nsa_qk_g=nsa_qk_g, nsa_cmp_pos=nsa_cmp_pos,
             nsa_cmp_w=nsa_cmp_w, out_g=out_g, w_out=w_out, w_up=w_up, w_down=w_down)
    depth = w_in.shape[0]
    batch = x_prompt.shape[0]
    dec_batch = x_sample.shape[0]
    n_c = batch + dec_batch
    n_c_pad = -(-n_c // 8) * 8
    c_all = jnp.concatenate([c_prompt, c_sample, jnp.zeros((n_c_pad - n_c, D_MODEL), F32)], axis=0)
    mods = _ada(c_all, w_ada, b_ada)
    lws = [_layer_weights(p, l) for l in range(depth)]
    y_prompt, st_p = _prompt_trunk(x_prompt, [mods[l, :batch] for l in range(depth)], lws)
    caches = dict(fox_kv=cache_fox_kv, fox_logf=cache_fox_logf, moba_kv=cache_moba_kv, nsa_kv=cache_nsa_kv,
                  nsa_win=state_nsa_win, lru_h=state_lru_h, lru_conv=state_lru_conv)
    y_sample, st_s = _sample_trunk(x_sample, [mods[l, batch:n_c] for l in range(depth)], lws, caches, page_table)
    return (y_prompt, y_sample) + tuple(st_p) + tuple(st_s)
```

```python
import functools

import numpy as np
import jax
import jax.numpy as jnp
from jax import lax
from jax.experimental import pallas as pl
from jax.experimental.pallas import tpu as pltpu

F32 = jnp.float32
BF16 = jnp.bfloat16

D_MODEL = 1024
N_MIXERS = 4
GROUP_W = 256
HEAD_DIM = 64
GROUP_HEADS = 4
D_FF = 4 * D_MODEL
CONV_W = 4
LRU_C = 8.0
ROPE_THETA = 500000.0
ROPE_DIM = HEAD_DIM // 4
ROPE_HALF = ROPE_DIM // 2
MOBA_BLOCK = 256
MOBA_TOPK = 3
CMP_LEN = 32
CMP_STRIDE = 16
SEL_BLOCK = 64
SEL_TOPK = 16
WINDOW = 512
EPS = 1e-6
SCALE = HEAD_DIM ** -0.5
LANES = 128
IN_W = 22 * LANES
NEG_INF = float("-inf")
VMEM_LIMIT = 56 * 1024 * 1024

TS = 8
TQ = LANES
TK = MOBA_BLOCK


def _cparams(sem):
    return pltpu.CompilerParams(dimension_semantics=sem, vmem_limit_bytes=VMEM_LIMIT)


def _dot(a, b):
    return jnp.dot(a, b, preferred_element_type=F32)


def _dot_nt(a, b):
    return lax.dot_general(a, b, (((1,), (1,)), ((), ())), preferred_element_type=F32)


def _split2(x):
    hi = x.astype(BF16)
    lo = (x - hi.astype(F32)).astype(BF16)
    return hi, lo


def _split3(x):
    hi = x.astype(BF16)
    r = x - hi.astype(F32)
    mid = r.astype(BF16)
    lo = (r - mid.astype(F32)).astype(BF16)
    return hi, mid, lo


def _dot_x3(x, m):
    hi, mid, lo = _split3(x)
    return _dot(hi, m) + _dot(mid, m) + _dot(lo, m)


def _dot_3x(m, x):
    hi, mid, lo = _split3(x)
    return _dot(m, hi) + _dot(m, mid) + _dot(m, lo)


def _log_sigmoid(x):
    return jnp.minimum(x, 0.0) - jnp.log1p(jnp.exp(-jnp.abs(x)))


def _softplus(x):
    return jnp.maximum(x, 0.0) + jnp.log1p(jnp.exp(-jnp.abs(x)))


def _gelu_tanh(x):
    return 0.5 * x * (1.0 + jnp.tanh(np.sqrt(2.0 / np.pi).astype(np.float32) * (x + 0.044715 * (x * x * x))))


def _rms_rows(x, g):
    return x * lax.rsqrt(jnp.mean(x * x, axis=-1, keepdims=True) + EPS) * g


def _head_lane(width=GROUP_W):
    return lax.broadcasted_iota(jnp.int32, (1, width), 1) // HEAD_DIM


def _stack_heads(q):
    hl = _head_lane()
    zero = jnp.zeros_like(q)
    return jnp.concatenate([jnp.where(hl == h, q, zero) for h in range(GROUP_HEADS)], axis=0)


def _unstack_heads(o, t):
    hl = _head_lane()
    out = jnp.zeros((t, GROUP_W), o.dtype)
    for h in range(GROUP_HEADS):
        out = out + jnp.where(hl == h, o[h * t:(h + 1) * t], 0.0)
    return out


def _rep4(a):
    return jnp.concatenate([a] * GROUP_HEADS, axis=0)


def _lanes4(a):
    return jnp.concatenate([a] * GROUP_HEADS, axis=1)


def _masked_softmax(l, mask, axis=-1):
    if mask is not None:
        l = jnp.where(mask, l, NEG_INF)
    m = jnp.max(l, axis=axis, keepdims=True)
    m = jnp.where(m > NEG_INF, m, 0.0)
    p = jnp.exp(l - m)
    return p / jnp.maximum(jnp.sum(p, axis=axis, keepdims=True), 1e-30)


def _top_select(score, n_pick, axis):
    width = score.shape[axis]
    shape = [1, 1]
    shape[axis] = width
    j = lax.broadcasted_iota(jnp.int32, tuple(shape), axis)
    sel = jnp.zeros(score.shape, F32)
    work = score
    for _ in range(n_pick):
        mx = jnp.max(work, axis=axis, keepdims=True)
        cand = (work == mx) & (mx > NEG_INF)
        idx = jnp.min(jnp.where(cand, j, width), axis=axis, keepdims=True)
        pick = j == idx
        sel = jnp.where(pick, 1.0, sel)
        work = jnp.where(pick, NEG_INF, work)
    return sel


def _stacked_q_t(q):
    qt = q.astype(F32).T
    row_head = lax.broadcasted_iota(jnp.int32, (GROUP_W, 1), 0) // HEAD_DIM
    return jnp.concatenate([jnp.where(row_head == h, qt, 0.0) for h in range(GROUP_HEADS)], axis=1).astype(BF16)


def _flash_init(m_s, l_s, acc_s):
    m_s[...] = jnp.full_like(m_s, NEG_INF)
    l_s[...] = jnp.zeros_like(l_s)
    acc_s[...] = jnp.zeros_like(acc_s)


def _flash_step(parts, m_s, l_s, acc_s):
    m_old = m_s[...]
    m_new = m_old
    for s, _ in parts:
        m_new = jnp.maximum(m_new, jnp.max(s, axis=0, keepdims=True))
    m_safe = jnp.where(m_new > NEG_INF, m_new, 0.0)
    alpha = jnp.exp(m_old - m_safe)
    l = alpha * l_s[...]
    acc = alpha * acc_s[...]
    for s, vt in parts:
        p = jnp.exp(s - m_safe)
        l = l + jnp.sum(p, axis=0, keepdims=True)
        acc = acc + _dot(vt, p.astype(BF16))
    l_s[...] = l
    acc_s[...] = acc
    m_s[...] = m_new


def _flash_pairs(n_tiles, tile_fn, last_fn, m_s, l_s, acc_s):
    def body(i, carry):
        _flash_step([tile_fn(2 * i), tile_fn(2 * i + 1)], m_s, l_s, acc_s)
        return carry

    lax.fori_loop(0, n_tiles // 2, body, 0)
    odd = n_tiles % 2 == 1

    @pl.when(odd)
    def _():
        tail = [tile_fn(n_tiles - 1)] + ([last_fn()] if last_fn is not None else [])
        _flash_step(tail, m_s, l_s, acc_s)

    if last_fn is not None:
        @pl.when(jnp.logical_not(odd))
        def _():
            _flash_step([last_fn()], m_s, l_s, acc_s)


def _unstack_t(acc, scale):
    a = acc * scale
    parts = [a[h * HEAD_DIM:(h + 1) * HEAD_DIM, h * TQ:(h + 1) * TQ] for h in range(GROUP_HEADS)]
    return jnp.concatenate(parts, axis=0).T


def _flash_result(l_s, acc_s):
    return _unstack_t(acc_s[...], 1.0 / jnp.maximum(l_s[...], 1e-30))


def _flash_scratch():
    return [pltpu.VMEM((1, GROUP_HEADS * TQ), F32), pltpu.VMEM((1, GROUP_HEADS * TQ), F32),
            pltpu.VMEM((GROUP_W, GROUP_HEADS * TQ), F32)]


def _resident(shape, index_map):
    return pl.BlockSpec(shape, index_map, pipeline_mode=pl.Buffered(1))


def _key_pos(k0, n):
    return k0 + lax.broadcasted_iota(jnp.int32, (n, 1), 0)


def _query_pos(q0):
    return q0 + lax.broadcasted_iota(jnp.int32, (1, TQ), 1)


def _ada_kernel(c_ref, w_ref, b_ref, o_ref):
    c = c_ref[...]
    a = (c * jax.nn.sigmoid(c)).astype(BF16)
    o_ref[0] = _dot(a, w_ref[0].astype(BF16)) + b_ref[0]


def _ada(c_all, w_ada, b_ada):
    depth, _, n = w_ada.shape
    rows = c_all.shape[0]
    tn = 1024
    return pl.pallas_call(
        _ada_kernel,
        grid=(depth, n // tn),
        in_specs=[pl.BlockSpec((rows, D_MODEL), lambda l, j: (0, 0)),
                  pl.BlockSpec((1, D_MODEL, tn), lambda l, j: (l, 0, j)),
                  pl.BlockSpec((1, 1, tn), lambda l, j: (l, 0, j))],
        out_specs=pl.BlockSpec((1, rows, tn), lambda l, j: (l, 0, j)),
        out_shape=jax.ShapeDtypeStruct((depth, rows, n), F32),
        compiler_params=_cparams(("arbitrary", "arbitrary")),
        name="ada_mod",
    )(c_all, w_ada, b_ada.reshape(depth, 1, n))


def _in_perm():
    gw, hd = GROUP_W, HEAD_DIM
    nkv0 = 9 * gw + GROUP_HEADS
    cols = list(range(0, 5 * gw))
    cols += list(range(5 * gw + 4, 8 * gw + 4))
    cols += list(range(8 * gw + 4, 9 * gw + 4))
    part = lambda k: list(range(nkv0 + k * hd, nkv0 + (k + 1) * hd))
    cols += part(0) + part(1)
    cols += part(2) + part(3)
    cols += part(4) + part(5)
    cols += list(range(5 * gw, 5 * gw + 4))
    cols += list(range(nkv0 + 6 * hd, nkv0 + 6 * hd + 12))
    return np.asarray(cols, np.int32)


C_LRU, C_FQ, C_FK, C_FV, C_MQ, C_MK, C_MV, C_NQ, C_KCVC, C_SEL, C_WIN, C_SMALL = (
    0, 512, 768, 1024, 1280, 1536, 1792, 2048, 2304, 2432, 2560, 2688)


def _block_diag_mean():
    m = np.zeros((GROUP_W, GROUP_W), np.float32)
    for h in range(GROUP_HEADS):
        m[h * HEAD_DIM:(h + 1) * HEAD_DIM, h * HEAD_DIM:(h + 1) * HEAD_DIM] = 1.0 / HEAD_DIM
    return m


def _rep_matrix():
    m = np.zeros((LANES, 2 * GROUP_W), np.float32)
    for j in range(HEAD_DIM):
        for k in range(GROUP_HEADS):
            m[j, k * HEAD_DIM + j] = 1.0
            m[HEAD_DIM + j, GROUP_W + k * HEAD_DIM + j] = 1.0
    return m


def _head_expand_matrix():
    m = np.zeros((LANES, GROUP_HEADS * LANES), np.float32)
    for h in range(GROUP_HEADS):
        m[h, h * LANES:(h + 1) * LANES] = 1.0
    return m


def _rope_tables(pos):
    inv = jnp.power(ROPE_THETA, -jnp.arange(ROPE_HALF, dtype=F32) / ROPE_HALF)
    ang = pos.astype(F32)[:, None] * inv[None, :]
    cos, sin = jnp.cos(ang), jnp.sin(ang)
    n = pos.shape[0]
    ones = jnp.ones((n, HEAD_DIM - ROPE_DIM), F32)
    zeros = jnp.zeros((n, HEAD_DIM - ROPE_DIM), F32)
    z8 = jnp.zeros((n, ROPE_HALF), F32)
    c = jnp.concatenate([cos, cos, ones], axis=1)
    sa = jnp.concatenate([z8, sin, zeros], axis=1)
    sb = jnp.concatenate([-sin, z8, zeros], axis=1)
    two = lambda a: jnp.concatenate([a, a], axis=1)
    return two(c), two(sa), two(sb)


_IN_COMMON = ("zl", "fq", "fkv", "small", "mq", "mkv", "nqn", "nqr", "nkv", "nwin")
_IN_PROMPT = ("fk16", "fvt", "fcrep", "mk16", "mvt", "kmean", "ks16", "vst", "kw16", "vwt", "kcvc")


def _in_proj_kernel(*refs, tiles_per_seq, prompt, tm):
    n_in = 14 if prompt else 12
    (x_ref, sc_ref, sh_ref, g_ref, w_ref, tc_ref, ta_ref, tb_ref, gains_ref, bf_ref, bd_ref,
     rep_ref) = refs[:12]
    names = _IN_COMMON + (_IN_PROMPT if prompt else ())
    out = dict(zip(names, refs[n_in:n_in + len(names)]))

    x = x_ref[...]
    xn = x * lax.rsqrt(jnp.mean(x * x, axis=-1, keepdims=True) + EPS) * g_ref[...]
    hm = (xn * (1.0 + sc_ref[0]) + sh_ref[0]).astype(BF16)

    def proj(c0, width):
        return _dot(hm, w_ref[:, c0:c0 + width])

    bd = bd_ref[...]
    tc, ta, tb = tc_ref[...], ta_ref[...], tb_ref[...]

    def hnorm(z, row):
        hi, lo = _split2(z * z)
        ms = _dot(hi, bd) + _dot(lo, bd)
        return z * lax.rsqrt(ms + EPS) * gains_ref[row:row + 1, :]

    def rope128(y):
        return y * tc + pltpu.roll(y, 8, axis=1) * ta + pltpu.roll(y, LANES - 8, axis=1) * tb

    def rope(y):
        return jnp.concatenate([rope128(y[:, :LANES]), rope128(y[:, LANES:])], axis=1)

    def store_t(name, a, tile):
        at = a.T
        for j in range(tm // tile):
            out[name][j] = at[:, j * tile:(j + 1) * tile].astype(BF16)

    out["zl"][...] = proj(C_LRU, 512)

    out["fq"][...] = (hnorm(proj(C_FQ, 256), 0) * SCALE).astype(out["fq"].dtype)
    fk = hnorm(proj(C_FK, 256), 1)
    fv = proj(C_FV, 256)
    out["fkv"][...] = jnp.concatenate([fk, fv], axis=1)

    out["mq"][...] = rope(hnorm(proj(C_MQ, 256), 2)) * SCALE
    mk = rope(hnorm(proj(C_MK, 256), 3))
    mv = proj(C_MV, 256)
    out["mkv"][...] = jnp.concatenate([mk, mv], axis=1)

    nqn = hnorm(proj(C_NQ, 256), 4)
    out["nqn"][...] = (nqn * SCALE).astype(out["nqn"].dtype)
    out["nqr"][...] = (rope(nqn) * SCALE).astype(out["nqr"].dtype)

    kcvc = proj(C_KCVC, LANES)
    sw = proj(C_SEL, 2 * LANES)
    first = (lax.broadcasted_iota(jnp.int32, (1, 2 * LANES), 1) % LANES) < HEAD_DIM
    swn = jnp.where(first, rope(hnorm(sw, 5)), sw)
    sel_rows = swn[:, :LANES]
    win_rows = swn[:, LANES:]
    out["nkv"][...] = jnp.concatenate([kcvc, sel_rows], axis=1)
    out["nwin"][...] = win_rows

    zs = proj(C_SMALL, LANES)
    lane = lax.broadcasted_iota(jnp.int32, (1, LANES), 1)
    logf = _log_sigmoid(zs + bf_ref[...])
    gate = jax.nn.sigmoid(zs)
    out["small"][...] = jnp.where(lane < GROUP_HEADS, logf, jnp.where(lane < 16, gate, 0.0))

    if prompt:
        tri_ref, fexp_ref = refs[12:14]
        carry_ref = refs[n_in + len(names)]
        out["fk16"][...] = fk.astype(BF16)
        store_t("fvt", fv, TK)
        out["mk16"][...] = mk.astype(BF16)
        store_t("mvt", mv, TK)
        out["kcvc"][...] = kcvc
        rep = rep_ref[...]
        sel_rep = _dot(sel_rows.astype(BF16), rep)
        win_rep = _dot(win_rows.astype(BF16), rep)
        out["ks16"][...] = sel_rep[:, :GROUP_W].astype(BF16)
        store_t("vst", sel_rep[:, GROUP_W:], TK)
        out["kw16"][...] = win_rep[:, :GROUP_W].astype(BF16)
        store_t("vwt", win_rep[:, GROUP_W:], TQ)

        i = pl.program_id(0)

        @pl.when(i % tiles_per_seq == 0)
        def _():
            carry_ref[...] = jnp.zeros_like(carry_ref)

        lf = jnp.where(lane < GROUP_HEADS, logf, 0.0)
        fc = _dot_3x(tri_ref[...], lf) + carry_ref[...]
        carry_ref[...] = fc[tm - 1:tm, :]
        out["fcrep"][...] = _dot_x3(fc, fexp_ref[...])
        nblk = tm // MOBA_BLOCK
        out["kmean"][0] = jnp.sum(mk.reshape(nblk, MOBA_BLOCK, GROUP_W), axis=1) * (1.0 / MOBA_BLOCK)


def _in_proj(x, sc, sh, g, w, tabs, gains, bf_row, *, tiles_per_seq, prompt, tm):
    rows = x.shape[0]
    n_tiles = rows // tm
    tb = sc.shape[1]
    n_tab_tiles = tabs[0].shape[0] // tm
    mod_map = (lambda i: (i // tiles_per_seq, 0, 0)) if tb == 1 else (lambda i: (i, 0, 0))
    tab_map = lambda i: (i % n_tab_tiles, 0)
    const = lambda i: (0, 0)
    row_map = lambda i: (i, 0)
    bd = jnp.asarray(_block_diag_mean(), BF16)
    rep = jnp.asarray(_rep_matrix(), BF16)
    in_specs = [pl.BlockSpec((tm, D_MODEL), row_map),
                pl.BlockSpec((1, tb, D_MODEL), mod_map),
                pl.BlockSpec((1, tb, D_MODEL), mod_map),
                pl.BlockSpec((1, D_MODEL), const),
                pl.BlockSpec((D_MODEL, IN_W), const),
                pl.BlockSpec((tm, LANES), tab_map),
                pl.BlockSpec((tm, LANES), tab_map),
                pl.BlockSpec((tm, LANES), tab_map),
                pl.BlockSpec((8, GROUP_W), const),
                pl.BlockSpec((1, LANES), const),
                pl.BlockSpec((GROUP_W, GROUP_W), const),
                pl.BlockSpec((LANES, 2 * GROUP_W), const)]
    args = [x, sc, sh, g, w, tabs[0], tabs[1], tabs[2], gains, bf_row, bd, rep]
    if prompt:
        tri = jnp.asarray(np.tril(np.ones((tm, tm), np.float32)), BF16)
        fexp = jnp.asarray(_head_expand_matrix(), BF16)
        in_specs += [pl.BlockSpec((tm, tm), const), pl.BlockSpec((LANES, GROUP_HEADS * LANES), const)]
        args += [tri, fexp]

    def o(width, dtype):
        return jax.ShapeDtypeStruct((rows, width), dtype), pl.BlockSpec((tm, width), row_map)

    def ot(tile):
        return (jax.ShapeDtypeStruct((rows // tile, GROUP_W, tile), BF16),
                pl.BlockSpec((tm // tile, GROUP_W, tile), lambda i: (i, 0, 0)))

    q_dtype = BF16 if prompt else F32
    outs = dict(zl=o(512, F32), fq=o(256, q_dtype), fkv=o(512, F32), small=o(LANES, F32), mq=o(256, F32),
                mkv=o(512, F32), nqn=o(256, q_dtype), nqr=o(256, q_dtype), nkv=o(256, F32), nwin=o(LANES, F32))
    scratch = []
    if prompt:
        nblk = tm // MOBA_BLOCK
        outs.update(fk16=o(256, BF16), fvt=ot(TK), fcrep=o(GROUP_HEADS * LANES, F32), mk16=o(256, BF16),
                    mvt=ot(TK),
                    kmean=(jax.ShapeDtypeStruct((n_tiles, nblk, GROUP_W), F32),
                           pl.BlockSpec((1, nblk, GROUP_W), lambda i: (i, 0, 0))),
                    ks16=o(256, BF16), vst=ot(TK), kw16=o(256, BF16), vwt=ot(TQ), kcvc=o(LANES, F32))
        scratch.append(pltpu.VMEM((1, LANES), F32))
    names = _IN_COMMON + (_IN_PROMPT if prompt else ())
    res = pl.pallas_call(
        functools.partial(_in_proj_kernel, tiles_per_seq=tiles_per_seq, prompt=prompt, tm=tm),
        grid=(n_tiles,),
        in_specs=in_specs,
        out_specs=[outs[n][1] for n in names],
        out_shape=[outs[n][0] for n in names],
        scratch_shapes=scratch,
        compiler_params=_cparams(("arbitrary",)),
        name="in_proj_prompt" if prompt else "in_proj_sample",
    )(*args)
    return dict(zip(names, res))


def _lru_coeffs(u, wa_ref, wx_ref, ba_ref, bx_ref, lam_ref):
    u16 = u.astype(BF16)
    r = jax.nn.sigmoid(_dot(u16, wa_ref[...]) + ba_ref[...])
    i = jax.nn.sigmoid(_dot(u16, wx_ref[...]) + bx_ref[...])
    log_a = -LRU_C * r * _softplus(-lam_ref[...])
    a = jnp.exp(log_a)
    b = jnp.sqrt(-jnp.tanh(log_a) * (a * a + 1.0)) * i * u
    return a, b


def _lru_prompt_kernel(zl_ref, cw_ref, cb_ref, wa_ref, wx_ref, ba_ref, bx_ref, lam_ref, og_ref,
                       o_ref, ht_ref, cnew_ref, xp_s, a_s, b_s, hs_s, h_s, *, tm):
    t = pl.program_id(1)

    @pl.when(t == 0)
    def _():
        h_s[...] = jnp.zeros_like(h_s)
        xp_s[0:8, :] = jnp.zeros((8, GROUP_W), F32)

    xa = zl_ref[:, 0:GROUP_W]
    ga = zl_ref[:, GROUP_W:2 * GROUP_W]
    xp_s[8:8 + tm, :] = xa
    u = (cb_ref[...] + cw_ref[0:1, :] * xp_s[5:5 + tm, :] + cw_ref[1:2, :] * xp_s[6:6 + tm, :]
         + cw_ref[2:3, :] * xp_s[7:7 + tm, :] + cw_ref[3:4, :] * xa)
    tail = xp_s[tm:tm + 8, :]
    xp_s[0:8, :] = tail
    cnew_ref[0] = tail
    a, b = _lru_coeffs(u, wa_ref, wx_ref, ba_ref, bx_ref, lam_ref)
    a_s[...] = a
    b_s[...] = b

    def body(j, h):
        r0 = pl.multiple_of(j * 8, 8)
        a8 = a_s[pl.ds(r0, 8), :]
        b8 = b_s[pl.ds(r0, 8), :]
        rows = []
        for k in range(8):
            h = a8[k:k + 1, :] * h + b8[k:k + 1, :]
            rows.append(h)
        hs_s[pl.ds(r0, 8), :] = jnp.concatenate(rows, axis=0)
        return h

    h = lax.fori_loop(0, tm // 8, body, h_s[...])
    h_s[...] = h
    ht_ref[0] = h
    o = hs_s[...] * _gelu_tanh(ga)
    o_ref[...] = _rms_rows(o, og_ref[...]).astype(BF16)


def _lru_prompt(zl, lw, batch, seq, tm):
    nt = seq // tm
    const = lambda b, t: (0, 0)
    vec = pl.BlockSpec((1, GROUP_W), const)
    mat = pl.BlockSpec((GROUP_W, GROUP_W), const)
    return pl.pallas_call(
        functools.partial(_lru_prompt_kernel, tm=tm),
        grid=(batch, nt),
        in_specs=[pl.BlockSpec((tm, 2 * GROUP_W), lambda b, t: (b * nt + t, 0)),
                  pl.BlockSpec((CONV_W, GROUP_W), const), vec, mat, mat, vec, vec, vec, vec],
        out_specs=[pl.BlockSpec((tm, GROUP_W), lambda b, t: (b * nt + t, 0)),
                   pl.BlockSpec((1, 1, GROUP_W), lambda b, t: (b, 0, 0)),
                   pl.BlockSpec((1, 8, GROUP_W), lambda b, t: (b, 0, 0))],
        out_shape=[jax.ShapeDtypeStruct((batch * seq, GROUP_W), BF16),
                   jax.ShapeDtypeStruct((batch, 1, GROUP_W), F32),
                   jax.ShapeDtypeStruct((batch, 8, GROUP_W), F32)],
        scratch_shapes=[pltpu.VMEM((tm + 8, GROUP_W), F32), pltpu.VMEM((tm, GROUP_W), F32),
                        pltpu.VMEM((tm, GROUP_W), F32), pltpu.VMEM((tm, GROUP_W), F32),
                        pltpu.VMEM((1, GROUP_W), F32)],
        compiler_params=_cparams(("arbitrary", "arbitrary")),
        name="lru_prompt",
    )(zl, lw["cw"], lw["cb"], lw["wa"], lw["wx"], lw["ba"], lw["bx"], lw["lam"], lw["og"])


def _fox_prompt_kernel(q_ref, fcq_ref, k_ref, vt_ref, fck_ref, og_ref, o_ref, m_s, l_s, acc_s):
    qi = pl.program_id(1)
    q0 = qi * TQ
    qst = _stacked_q_t(q_ref[...])
    fcq = fcq_ref[...]
    fq_row = jnp.concatenate([fcq[:, h * LANES:(h + 1) * LANES].T[0:1, :] for h in range(GROUP_HEADS)], axis=1)
    qpos = _lanes4(_query_pos(q0))
    _flash_init(m_s, l_s, acc_s)
    n_full = q0 // TK

    def tile(kt, masked):
        k0 = pl.multiple_of(kt * TK, TK)
        s = _dot(k_ref[pl.ds(k0, TK), :], qst) + (fq_row - fck_ref[pl.ds(k0, TK), :])
        if masked:
            s = jnp.where(_key_pos(k0, TK) <= qpos, s, NEG_INF)
        return s, vt_ref[kt]

    _flash_pairs(n_full, lambda kt: tile(kt, False), lambda: tile(n_full, True), m_s, l_s, acc_s)
    o_ref[...] = _rms_rows(_flash_result(l_s, acc_s), og_ref[...]).astype(BF16)


def _fox_prompt(r, og, batch, seq):
    nq = seq // TQ
    nk = seq // TK
    row = lambda b, i: (b * nq + i, 0)
    return pl.pallas_call(
        _fox_prompt_kernel,
        grid=(batch, nq),
        in_specs=[pl.BlockSpec((TQ, GROUP_W), row),
                  pl.BlockSpec((TQ, GROUP_HEADS * LANES), row),
                  _resident((seq, GROUP_W), lambda b, i: (b, 0)),
                  _resident((nk, GROUP_W, TK), lambda b, i: (b, 0, 0)),
                  _resident((seq, GROUP_HEADS * LANES), lambda b, i: (b, 0)),
                  pl.BlockSpec((1, GROUP_W), lambda b, i: (0, 0))],
        out_specs=pl.BlockSpec((TQ, GROUP_W), row),
        out_shape=jax.ShapeDtypeStruct((batch * seq, GROUP_W), BF16),
        scratch_shapes=_flash_scratch(),
        compiler_params=_cparams(("arbitrary", "arbitrary")),
        name="fox_prompt",
    )(r["fq"], r["fcrep"], r["fk16"], r["fvt"], r["fcrep"], og)


def _moba_prompt_kernel(q_ref, kmean_ref, k_ref, vt_ref, og_ref, o_ref, m_s, l_s, acc_s, sel_s, *, nb):
    qi = pl.program_id(1)
    q0 = qi * TQ
    cur = q0 // MOBA_BLOCK
    q32 = q_ref[...]
    qt = q32.T
    row_head = lax.broadcasted_iota(jnp.int32, (GROUP_W, 1), 0) // HEAD_DIM
    qst32 = jnp.concatenate([jnp.where(row_head == h, qt, 0.0) for h in range(GROUP_HEADS)], axis=1)
    qst = qst32.astype(BF16)
    qh, ql = _split2(qst32)
    kh, kl = _split2(kmean_ref[0])
    sc = _dot(kh, qh) + _dot(kh, ql) + _dot(kl, qh)
    n_col = lax.broadcasted_iota(jnp.int32, (nb, 1), 0)
    sel_s[...] = _top_select(jnp.where(n_col < cur, sc, NEG_INF), min(MOBA_TOPK, nb), 0)
    _flash_init(m_s, l_s, acc_s)

    def past_block(n):
        k0 = pl.multiple_of(n * MOBA_BLOCK, MOBA_BLOCK)
        s = _dot(k_ref[pl.ds(k0, MOBA_BLOCK), :], qst)
        return jnp.where(sel_s[pl.ds(n, 1), :] > 0.5, s, NEG_INF), vt_ref[n]

    def own_block():
        k0 = pl.multiple_of(cur * MOBA_BLOCK, MOBA_BLOCK)
        s = _dot(k_ref[pl.ds(k0, MOBA_BLOCK), :], qst)
        return jnp.where(_key_pos(k0, MOBA_BLOCK) <= _lanes4(_query_pos(q0)), s, NEG_INF), vt_ref[cur]

    _flash_pairs(cur, past_block, own_block, m_s, l_s, acc_s)
    o_ref[...] = _rms_rows(_flash_result(l_s, acc_s), og_ref[...]).astype(BF16)


def _moba_prompt(r, og, batch, seq):
    assert TK == MOBA_BLOCK
    nq = seq // TQ
    nb = seq // MOBA_BLOCK
    kmean = r["kmean"].reshape(batch, nb, GROUP_W)
    row = lambda b, i: (b * nq + i, 0)
    return pl.pallas_call(
        functools.partial(_moba_prompt_kernel, nb=nb),
        grid=(batch, nq),
        in_specs=[pl.BlockSpec((TQ, GROUP_W), row),
                  pl.BlockSpec((1, nb, GROUP_W), lambda b, i: (b, 0, 0)),
                  _resident((seq, GROUP_W), lambda b, i: (b, 0)),
                  _resident((nb, GROUP_W, TK), lambda b, i: (b, 0, 0)),
                  pl.BlockSpec((1, GROUP_W), lambda b, i: (0, 0))],
        out_specs=pl.BlockSpec((TQ, GROUP_W), row),
        out_shape=jax.ShapeDtypeStruct((batch * seq, GROUP_W), BF16),
        scratch_shapes=_flash_scratch() + [pltpu.VMEM((nb, GROUP_HEADS * TQ), F32)],
        compiler_params=_cparams(("arbitrary", "arbitrary")),
        name="moba_prompt",
    )(r["mq"], kmean, r["mk16"], r["mvt"], og)


def _cmp_weights(cw):
    w = cw.reshape(2, CMP_LEN, HEAD_DIM, HEAD_DIM)
    zero = jnp.zeros((CMP_STRIDE, HEAD_DIM, HEAD_DIM), cw.dtype)

    def half(j0):
        wk = w[0, j0:j0 + CMP_STRIDE]
        wv = w[1, j0:j0 + CMP_STRIDE]
        top = jnp.concatenate([wk, zero], axis=2)
        bot = jnp.concatenate([zero, wv], axis=2)
        return jnp.concatenate([top, bot], axis=1).reshape(CMP_STRIDE * LANES, LANES)

    return half(0).astype(BF16), half(CMP_STRIDE).astype(BF16)


def _cmp_pos(cpos):
    p = jnp.concatenate([cpos[0], cpos[1]], axis=1)
    return p[:CMP_STRIDE].reshape(1, -1), p[CMP_STRIDE:].reshape(1, -1)


def _cmp_windows(x, plo_ref, phi_ref, wlo_ref, whi_ref, g_ref, rep_ref, n_valid):
    y_lo = _dot((x + plo_ref[...]).astype(BF16), wlo_ref[...])
    y_hi = _dot((x + phi_ref[...]).astype(BF16), whi_ref[...])
    rows = y_lo.shape[0]
    y = y_lo + pltpu.roll(y_hi, rows - 1, axis=0)
    lane = lax.broadcasted_iota(jnp.int32, (1, LANES), 1)
    is_k = lane < HEAD_DIM
    ms = jnp.sum(jnp.where(is_k, y * y, 0.0), axis=-1, keepdims=True) * (1.0 / HEAD_DIM)
    y = jnp.where(is_k, y * lax.rsqrt(ms + EPS) * g_ref[...], y)
    row = lax.broadcasted_iota(jnp.int32, (rows, 1), 0)
    y = jnp.where(row < n_valid, y, 0.0)
    return _dot(y.astype(BF16), rep_ref[...])


def _cmp_prompt_kernel(x_ref, plo_ref, phi_ref, wlo_ref, whi_ref, g_ref, rep_ref, k_ref, vt_ref, *, n_valid):
    c = _cmp_windows(x_ref[...], plo_ref, phi_ref, wlo_ref, whi_ref, g_ref, rep_ref, n_valid)
    k_ref[...] = c[:, :GROUP_W].astype(BF16)
    vt_ref[0] = c[:, GROUP_W:].T.astype(BF16)


def _cmp_prompt(kcvc, nw, batch, seq):
    rows = seq // CMP_STRIDE
    width = CMP_STRIDE * LANES
    x2 = kcvc.reshape(batch * rows, width)
    n_valid = (seq - CMP_LEN) // CMP_STRIDE + 1
    const = lambda b: (0, 0)
    return pl.pallas_call(
        functools.partial(_cmp_prompt_kernel, n_valid=n_valid),
        grid=(batch,),
        in_specs=[pl.BlockSpec((rows, width), lambda b: (b, 0)),
                  pl.BlockSpec((1, width), const), pl.BlockSpec((1, width), const),
                  pl.BlockSpec((width, LANES), const), pl.BlockSpec((width, LANES), const),
                  pl.BlockSpec((1, LANES), const), pl.BlockSpec((LANES, 2 * GROUP_W), const)],
        out_specs=[pl.BlockSpec((rows, GROUP_W), lambda b: (b, 0)),
                   pl.BlockSpec((1, GROUP_W, rows), lambda b: (b, 0, 0))],
        out_shape=[jax.ShapeDtypeStruct((batch * rows, GROUP_W), BF16),
                   jax.ShapeDtypeStruct((batch, GROUP_W, rows), BF16)],
        compiler_params=_cparams(("arbitrary",)),
        name="nsa_cmp_prompt",
    )(x2, nw["plo"], nw["phi"], nw["wlo"], nw["whi"], nw["gcmp"], nw["rep"])


def _overlap_matrix(n_rows, n_valid, ns, ns_pad):
    c = np.arange(n_rows)[:, None] * CMP_STRIDE
    b = np.arange(ns_pad)[None, :] * SEL_BLOCK
    m = (c < b + SEL_BLOCK) & (c + CMP_LEN > b)
    m &= (np.arange(n_rows)[:, None] < n_valid) & (np.arange(ns_pad)[None, :] < ns)
    return m.astype(np.float32)


def _gate_expand_matrix():
    m = np.zeros((LANES, 3 * GROUP_W), np.float32)
    for h in range(GROUP_HEADS):
        for g in range(3):
            m[4 + h * 3 + g, g * GROUP_W + h * HEAD_DIM:g * GROUP_W + (h + 1) * HEAD_DIM] = 1.0
    return m


def _nsa_gate_mix(o_c, o_s, o_w, small, gexp_ref, og_ref):
    gx = _dot_x3(small, gexp_ref[...])
    o = gx[:, 0:GROUP_W] * o_c + gx[:, GROUP_W:2 * GROUP_W] * o_s + gx[:, 2 * GROUP_W:3 * GROUP_W] * o_w
    return _rms_rows(o, og_ref[...])


def _nsa_prompt_kernel(qn_ref, qr_ref, small_ref, kc_ref, vct_ref, ovlt_ref, gexp_ref, ks_ref, vst_ref,
                       kw0_ref, kw1_ref, kw2_ref, kw3_ref, kw4_ref,
                       vw0_ref, vw1_ref, vw2_ref, vw3_ref, vw4_ref, og_ref, o_ref,
                       m_s, l_s, acc_s, *, ns, n_sel):
    qi = pl.program_id(1)
    q0 = qi * TQ
    pos = _query_pos(q0)
    pos4 = _lanes4(pos)
    qnt = _stacked_q_t(qn_ref[...])
    qrt = _stacked_q_t(qr_ref[...])

    nc = kc_ref.shape[0]
    cend = lax.broadcasted_iota(jnp.int32, (nc, 1), 0) * CMP_STRIDE + (CMP_LEN - 1)
    pc = _masked_softmax(_dot(kc_ref[...], qnt), cend <= pos4, axis=0)
    o_c = _unstack_t(_dot(vct_ref[0], pc.astype(BF16)), 1.0)

    pcs = pc[:, 0:TQ] + pc[:, TQ:2 * TQ] + pc[:, 2 * TQ:3 * TQ] + pc[:, 3 * TQ:4 * TQ]
    imp = _dot_3x(ovlt_ref[...], pcs)
    ns_pad = imp.shape[0]
    j = lax.broadcasted_iota(jnp.int32, (ns_pad, 1), 0)
    cur = pos // SEL_BLOCK
    imp = jnp.where((j == 0) | (j == cur) | (j == cur - 1), jnp.inf, imp)
    imp = jnp.where((j * SEL_BLOCK > pos) | (j >= ns), NEG_INF, imp)
    sel16 = _top_select(imp, n_sel, 0).astype(BF16)

    _flash_init(m_s, l_s, acc_s)
    nk = (q0 + TQ + TK - 1) // TK
    blk_row = lax.broadcasted_iota(jnp.int32, (1, ns_pad), 1)
    key_blk = lax.broadcasted_iota(jnp.int32, (TK, 1), 0) // SEL_BLOCK

    def sel_tile(kt):
        k0 = pl.multiple_of(kt * TK, TK)
        s = _dot(ks_ref[pl.ds(k0, TK), :], qrt)
        expand = (blk_row == kt * (TK // SEL_BLOCK) + key_blk).astype(BF16)
        mask = (_dot(expand, sel16) > 0.5) & (_key_pos(k0, TK) <= pos)
        return s + _lanes4(jnp.where(mask, 0.0, NEG_INF)), vst_ref[kt]

    _flash_pairs(nk, sel_tile, None, m_s, l_s, acc_s)
    o_s = _flash_result(l_s, acc_s)

    kws = (kw0_ref, kw1_ref, kw2_ref, kw3_ref, kw4_ref)
    vws = (vw0_ref, vw1_ref, vw2_ref, vw3_ref, vw4_ref)
    logits = []
    for t, kw in enumerate(kws):
        kpos = _key_pos(q0 + (t - 4) * TQ, TQ)
        mask = (kpos >= 0) & (kpos <= pos4) & (kpos >= pos4 - WINDOW)
        logits.append(jnp.where(mask, _dot(kw[...], qrt), NEG_INF))
    pw = _masked_softmax(jnp.concatenate(logits, axis=0), None, axis=0).astype(BF16)
    o_wt = jnp.zeros((GROUP_W, GROUP_HEADS * TQ), F32)
    for t, vw in enumerate(vws):
        o_wt = o_wt + _dot(vw[0], pw[t * TQ:(t + 1) * TQ, :])
    o_w = _unstack_t(o_wt, 1.0)

    o_ref[...] = _nsa_gate_mix(o_c, o_s, o_w, small_ref[...], gexp_ref, og_ref).astype(BF16)


def _nsa_prompt(r, kc_rep, vct_rep, og, batch, seq):
    assert WINDOW == 4 * TQ
    nq = seq // TQ
    nk = seq // TK
    nc = seq // CMP_STRIDE
    ns = seq // SEL_BLOCK
    ns_pad = -(-ns // LANES) * LANES
    n_valid = (seq - CMP_LEN) // CMP_STRIDE + 1
    ovlt = jnp.asarray(_overlap_matrix(nc, n_valid, ns, ns_pad).T, BF16)
    gexp = jnp.asarray(_gate_expand_matrix(), BF16)
    const = lambda b, i: (0, 0)
    row = lambda b, i: (b * nq + i, 0)
    kw_spec = lambda t: pl.BlockSpec((TQ, GROUP_W), lambda b, i: (b * nq + jnp.maximum(i + t - 4, 0), 0))
    vw_spec = lambda t: pl.BlockSpec((1, GROUP_W, TQ), lambda b, i: (b * nq + jnp.maximum(i + t - 4, 0), 0, 0))
    return pl.pallas_call(
        functools.partial(_nsa_prompt_kernel, ns=ns, n_sel=min(SEL_TOPK, ns)),
        grid=(batch, nq),
        in_specs=[pl.BlockSpec((TQ, GROUP_W), row), pl.BlockSpec((TQ, GROUP_W), row),
                  pl.BlockSpec((TQ, LANES), row),
                  pl.BlockSpec((nc, GROUP_W), lambda b, i: (b, 0)),
                  pl.BlockSpec((1, GROUP_W, nc), lambda b, i: (b, 0, 0)),
                  pl.BlockSpec((ns_pad, nc), const),
                  pl.BlockSpec((LANES, 3 * GROUP_W), const),
                  _resident((seq, GROUP_W), lambda b, i: (b, 0)),
                  _resident((nk, GROUP_W, TK), lambda b, i: (b, 0, 0))]
                 + [kw_spec(t) for t in range(5)] + [vw_spec(t) for t in range(5)]
                 + [pl.BlockSpec((1, GROUP_W), const)],
        out_specs=pl.BlockSpec((TQ, GROUP_W), row),
        out_shape=jax.ShapeDtypeStruct((batch * seq, GROUP_W), BF16),
        scratch_shapes=_flash_scratch(),
        compiler_params=_cparams(("arbitrary", "arbitrary")),
        name="nsa_prompt",
    )(r["nqn"], r["nqr"], r["small"], kc_rep, vct_rep, ovlt, gexp, r["ks16"], r["vst"],
      *([r["kw16"]] * 5), *([r["vwt"]] * 5), og)


def _out_proj_kernel(x_ref, o0_ref, o1_ref, o2_ref, o3_ref, w_ref, gt_ref, sc_ref, sh_ref, g_ref,
                     x1_ref, hf_ref):
    o_cat = jnp.concatenate([o0_ref[...], o1_ref[...], o2_ref[...], o3_ref[...]], axis=1).astype(BF16)
    x1 = x_ref[...] + gt_ref[0] * _dot(o_cat, w_ref[...])
    x1_ref[...] = x1
    hf_ref[...] = (_rms_rows(x1, g_ref[...]) * (1.0 + sc_ref[0]) + sh_ref[0]).astype(BF16)


def _out_proj(x, outs, w_out, gt1, sc2, sh2, g2, tiles_per_seq, tm):
    rows = x.shape[0]
    tb = gt1.shape[1]
    mod_map = (lambda i: (i // tiles_per_seq, 0, 0)) if tb == 1 else (lambda i: (i, 0, 0))
    row = lambda i: (i, 0)
    const = lambda i: (0, 0)
    mod = pl.BlockSpec((1, tb, D_MODEL), mod_map)
    return pl.pallas_call(
        _out_proj_kernel,
        grid=(rows // tm,),
        in_specs=[pl.BlockSpec((tm, D_MODEL), row)] + [pl.BlockSpec((tm, GROUP_W), row)] * 4
                 + [pl.BlockSpec((D_MODEL, D_MODEL), const), mod, mod, mod, pl.BlockSpec((1, D_MODEL), const)],
        out_specs=[pl.BlockSpec((tm, D_MODEL), row), pl.BlockSpec((tm, D_MODEL), row)],
        out_shape=[jax.ShapeDtypeStruct((rows, D_MODEL), F32), jax.ShapeDtypeStruct((rows, D_MODEL), BF16)],
        compiler_params=_cparams(("arbitrary",)),
        name="out_proj",
    )(x, *outs, w_out, gt1, sc2, sh2, g2)


def _mlp_kernel(x1_ref, hf_ref, wu_ref, wd_ref, gt_ref, y_ref, acc_s):
    f = pl.program_id(1)

    @pl.when(f == 0)
    def _():
        acc_s[...] = jnp.zeros_like(acc_s)

    up = jnp.maximum(_dot(hf_ref[...], wu_ref[...]), 0.0)
    acc_s[...] += _dot((up * up).astype(BF16), wd_ref[...])

    @pl.when(f == pl.num_programs(1) - 1)
    def _():
        y_ref[...] = x1_ref[...] + gt_ref[0] * acc_s[...]


def _mlp(x1, hf, w_up, w_down, gt2, tiles_per_seq, tm, tf):
    rows = x1.shape[0]
    tb = gt2.shape[1]
    mod_map = (lambda i, f: (i // tiles_per_seq, 0, 0)) if tb == 1 else (lambda i, f: (i, 0, 0))
    row = lambda i, f: (i, 0)
    return pl.pallas_call(
        _mlp_kernel,
        grid=(rows // tm, D_FF // tf),
        in_specs=[pl.BlockSpec((tm, D_MODEL), row), pl.BlockSpec((tm, D_MODEL), row),
                  pl.BlockSpec((D_MODEL, tf), lambda i, f: (0, f)),
                  pl.BlockSpec((tf, D_MODEL), lambda i, f: (f, 0)),
                  pl.BlockSpec((1, tb, D_MODEL), mod_map)],
        out_specs=pl.BlockSpec((tm, D_MODEL), row),
        out_shape=jax.ShapeDtypeStruct((rows, D_MODEL), F32),
        scratch_shapes=[pltpu.VMEM((tm, D_MODEL), F32)],
        compiler_params=_cparams(("arbitrary", "arbitrary")),
        name="mlp",
    )(x1, hf, w_up, w_down, gt2)


def _block_diag(w):
    out = jnp.zeros((GROUP_W, GROUP_W), w.dtype)
    for h in range(GROUP_HEADS):
        out = out.at[h * HEAD_DIM:(h + 1) * HEAD_DIM, h * HEAD_DIM:(h + 1) * HEAD_DIM].set(w[h])
    return out


def _layer_weights(p, l):
    tile4 = lambda g: jnp.tile(g, GROUP_HEADS)
    ones = jnp.ones((HEAD_DIM,), F32)
    w_in = jnp.concatenate([p["w_in"][l][:, _in_perm()],
                            jnp.zeros((D_MODEL, IN_W - _in_perm().shape[0]), F32)], axis=1).astype(BF16)
    ng = p["nsa_qk_g"][l]
    gains = jnp.stack([tile4(p["fox_qk_g"][l, 0]), tile4(p["fox_qk_g"][l, 1]),
                       tile4(p["moba_qk_g"][l, 0]), tile4(p["moba_qk_g"][l, 1]),
                       tile4(ng[0]), jnp.concatenate([ng[2], ones, ng[3], ones]),
                       jnp.ones((GROUP_W,), F32), jnp.ones((GROUP_W,), F32)])
    bf_row = jnp.zeros((1, LANES), F32).at[0, :GROUP_HEADS].set(p["fox_bf"][l])
    row = lambda a: a.reshape(1, -1)
    lru = dict(cw=p["lru_conv_w"][l], cb=row(p["lru_conv_b"][l]),
               wa=_block_diag(p["lru_wa"][l]).astype(BF16), wx=_block_diag(p["lru_wx"][l]).astype(BF16),
               ba=row(p["lru_ba"][l]), bx=row(p["lru_bx"][l]), lam=row(p["lru_lambda"][l]),
               og=row(p["out_g"][l, 0]))
    wlo, whi = _cmp_weights(p["nsa_cmp_w"][l])
    plo, phi = _cmp_pos(p["nsa_cmp_pos"][l])
    gcmp = jnp.concatenate([ng[1], ones]).reshape(1, LANES)
    nsa = dict(wlo=wlo, whi=whi, plo=plo, phi=phi, gcmp=gcmp, rep=jnp.asarray(_rep_matrix(), BF16))
    return dict(w_in=w_in, gains=gains, bf_row=bf_row, lru=lru, nsa=nsa,
                g1=row(p["norm_g"][l, 0]), g2=row(p["norm_g"][l, 1]),
                og=[row(p["out_g"][l, k]) for k in range(N_MIXERS)],
                w_out=p["w_out"][l].astype(BF16), w_up=p["w_up"][l].astype(BF16),
                w_down=p["w_down"][l].astype(BF16))


def _prompt_layer(x, mod, lw, batch, seq, tabs):
    tm = min(512, seq)
    tps = seq // tm
    sh1, sc1, gt1, sh2, sc2, gt2 = [m[:, None, :] for m in jnp.split(mod, 6, axis=-1)]
    r = _in_proj(x, sc1, sh1, lw["g1"], lw["w_in"], tabs, lw["gains"], lw["bf_row"],
                 tiles_per_seq=tps, prompt=True, tm=tm)
    o_lru, h_t, conv_t = _lru_prompt(r["zl"], lw["lru"], batch, seq, tm)
    o_fox = _fox_prompt(r, lw["og"][1], batch, seq)
    o_moba = _moba_prompt(r, lw["og"][2], batch, seq)
    kc_rep, vct_rep = _cmp_prompt(r["kcvc"], lw["nsa"], batch, seq)
    o_nsa = _nsa_prompt(r, kc_rep, vct_rep, lw["og"][3], batch, seq)
    x1, hf = _out_proj(x, (o_lru, o_fox, o_moba, o_nsa), lw["w_out"], gt1, sc2, sh2, lw["g2"], tps, tm)
    y = _mlp(x1, hf, lw["w_up"], lw["w_down"], gt2, tps, tm, 1024)
    keep = min(WINDOW, seq)
    state = (r["fkv"].reshape(batch, seq, 2, GROUP_HEADS, HEAD_DIM),
             r["small"][:, :GROUP_HEADS].reshape(batch, seq, GROUP_HEADS),
             r["mkv"].reshape(batch, seq, 2, GROUP_HEADS, HEAD_DIM),
             r["nkv"].reshape(batch, seq, 4, HEAD_DIM),
             r["nwin"].reshape(batch, seq, 2, HEAD_DIM)[:, seq - keep:],
             h_t.reshape(batch, GROUP_W),
             conv_t[:, 8 - (CONV_W - 1):, :])
    return y, state


def _prompt_trunk(x_prompt, mods, lws):
    batch, seq, _ = x_prompt.shape
    assert seq % max(512, TK) == 0 or seq in (256,)
    tabs = _rope_tables(jnp.arange(seq, dtype=jnp.int32))
    x = x_prompt.reshape(batch * seq, D_MODEL)
    states = []
    for l, lw in enumerate(lws):
        x, st = _prompt_layer(x, mods[l], lw, batch, seq, tabs)
        states.append(st)
    stacked = [jnp.stack([s[i] for s in states], axis=0) for i in range(len(states[0]))]
    return x.reshape(batch, seq, D_MODEL), stacked


def _lru_sample_kernel(zl_ref, h0_ref, cbuf_ref, cw_ref, cb_ref, wa_ref, wx_ref, ba_ref, bx_ref, lam_ref,
                       og_ref, o_ref, ht_ref, cnew_ref, *, n_seq, t_real):
    o_ref[...] = jnp.zeros_like(o_ref)
    xs = [cbuf_ref[k] for k in range(CONV_W - 1)] + [zl_ref[t, :, 0:GROUP_W] for t in range(t_real)]
    h = h0_ref[...]
    for t in range(t_real):
        u = cb_ref[...]
        for k in range(CONV_W):
            u = u + xs[t + k] * cw_ref[k:k + 1, :]
        a, b = _lru_coeffs(u, wa_ref, wx_ref, ba_ref, bx_ref, lam_ref)
        h = a * h + b
        o = h * _gelu_tanh(zl_ref[t, :, GROUP_W:2 * GROUP_W])
        o_ref[t] = _rms_rows(o, og_ref[...])
    ht_ref[...] = h
    for k in range(CONV_W - 1):
        cnew_ref[k] = xs[t_real + k]


def _lru_sample(zl, h0, cbuf, lw, n_seq, t_real):
    assert t_real >= CONV_W - 1
    zl_t = jnp.transpose(zl.reshape(n_seq, TS, 2 * GROUP_W), (1, 0, 2))
    o_t, h_t, conv_t = pl.pallas_call(
        functools.partial(_lru_sample_kernel, n_seq=n_seq, t_real=t_real),
        out_shape=[jax.ShapeDtypeStruct((TS, n_seq, GROUP_W), F32),
                   jax.ShapeDtypeStruct((n_seq, GROUP_W), F32),
                   jax.ShapeDtypeStruct((CONV_W - 1, n_seq, GROUP_W), F32)],
        compiler_params=pltpu.CompilerParams(vmem_limit_bytes=VMEM_LIMIT),
        name="lru_sample",
    )(zl_t, h0, cbuf, lw["cw"], lw["cb"], lw["wa"], lw["wx"], lw["ba"], lw["bx"], lw["lam"], lw["og"])
    return jnp.transpose(o_t, (1, 0, 2)).reshape(n_seq * TS, GROUP_W), h_t, conv_t


def _page_specs(layer, tail, n_pages):
    zeros = (0,) * len(tail)
    return [pl.BlockSpec((1, 1) + tail, functools.partial(lambda j, s, pt: (layer, pt[s, j]) + zeros, j))
            for j in range(n_pages)]


def _seq_spec(width, rows=TS):
    return pl.BlockSpec((rows, width), lambda s, pt: (s, 0))


def _const_spec(shape):
    zeros = (0,) * len(shape)
    return pl.BlockSpec(shape, lambda s, pt: zeros)


def _pad_rows(a, rows):
    return jnp.concatenate([a, jnp.zeros((rows - a.shape[0], a.shape[1]), a.dtype)], axis=0)


def _rep_rows(a, t):
    return jnp.concatenate([jnp.broadcast_to(a[h:h + 1, :], (t, a.shape[1])) for h in range(GROUP_HEADS)], axis=0)


def _sample_pos(past):
    return _rep4(past + lax.broadcasted_iota(jnp.int32, (TS, 1), 0))


def _weighted_values(p, vals):
    o = jnp.zeros((p.shape[0], GROUP_W), F32)
    for j, (kind, v) in enumerate(vals):
        pj = p[:, j * LANES:(j + 1) * LANES]
        o = o + (_dot_nt(pj, v) if kind == "t" else _dot(pj, v))
    return o


def _fox_sample_kernel(pt_ref, q_ref, kvn_ref, lfn_ref, u_ref, og_ref, *refs, n_pages, page):
    kv_refs = refs[:n_pages]
    lf_refs = refs[n_pages:2 * n_pages]
    o_ref = refs[2 * n_pages]
    past = n_pages * page
    qs = _stack_heads(q_ref[...]).astype(BF16)
    u = u_ref[...]
    off = jnp.zeros((GROUP_HEADS, 1), F32)
    logits, fcs, vals = [], [], []
    for j in range(n_pages):
        c = _dot_x3(lf_refs[j][0, 0], u) + off
        off = c[:, page - 1:page]
        fcs.append(c)
        logits.append(_dot(qs, kv_refs[j][0, 0, 0:GROUP_W, :].astype(BF16)))
        vals.append(("t", kv_refs[j][0, 0, GROUP_W:2 * GROUP_W, :].astype(BF16)))
    cn = _dot_x3(lfn_ref[0], u) + off
    fcs.append(cn)
    kvn = _pad_rows(kvn_ref[...], LANES).astype(BF16)
    logits.append(_dot_nt(qs, kvn[:, 0:GROUP_W]))
    vals.append(("n", kvn[:, GROUP_W:2 * GROUP_W]))

    lane = lax.broadcasted_iota(jnp.int32, (1, LANES), 1)
    t_row = lax.broadcasted_iota(jnp.int32, (GROUP_HEADS * TS, 1), 0) % TS
    fq = jnp.sum(jnp.where(lane == t_row, _rep_rows(cn, TS), 0.0), axis=-1, keepdims=True)
    s = jnp.concatenate(logits, axis=1) + fq - jnp.concatenate([_rep_rows(c, TS) for c in fcs], axis=1)
    kpos = lax.broadcasted_iota(jnp.int32, (1, s.shape[1]), 1)
    p = _masked_softmax(s, kpos <= _sample_pos(past)).astype(BF16)
    o_ref[...] = _rms_rows(_unstack_heads(_weighted_values(p, vals), TS), og_ref[...])


def _fox_sample(page_table, layer, fq, fkv_new, lf_new_t, pool_kv_t, pool_lf_t, og):
    n_seq, n_pages = page_table.shape
    page = pool_kv_t.shape[-1]
    assert page == LANES
    u = jnp.asarray(np.triu(np.ones((page, page), np.float32)), BF16)
    grid_spec = pltpu.PrefetchScalarGridSpec(
        num_scalar_prefetch=1, grid=(n_seq,),
        in_specs=[_seq_spec(GROUP_W), _seq_spec(2 * GROUP_W),
                  pl.BlockSpec((1, GROUP_HEADS, LANES), lambda s, pt: (s, 0, 0)),
                  _const_spec((page, page)), _const_spec((1, GROUP_W))]
                 + _page_specs(layer, (2 * GROUP_W, page), n_pages)
                 + _page_specs(layer, (GROUP_HEADS, page), n_pages),
        out_specs=_seq_spec(GROUP_W))
    return pl.pallas_call(
        functools.partial(_fox_sample_kernel, n_pages=n_pages, page=page),
        grid_spec=grid_spec,
        out_shape=jax.ShapeDtypeStruct((n_seq * TS, GROUP_W), F32),
        compiler_params=_cparams(("arbitrary",)),
        name="fox_sample",
    )(page_table, fq, fkv_new, lf_new_t, u, og, *([pool_kv_t] * n_pages), *([pool_lf_t] * n_pages))


def _moba_sample_kernel(pt_ref, q_ref, kvn_ref, og_ref, *refs, n_pages, page):
    kv_refs = refs[:n_pages]
    o_ref = refs[n_pages]
    past = n_pages * page
    per_blk = MOBA_BLOCK // page
    nb = n_pages // per_blk
    qs32 = _stack_heads(q_ref[...])
    qs = qs32.astype(BF16)
    lane = lax.broadcasted_iota(jnp.int32, (1, LANES), 1)
    logits, vals = [], []
    kmean_t = jnp.zeros((GROUP_W, LANES), F32)
    for n in range(nb):
        ksum = jnp.zeros((GROUP_W, 1), F32)
        for j in range(n * per_blk, (n + 1) * per_blk):
            kt = kv_refs[j][0, 0, 0:GROUP_W, :]
            ksum = ksum + jnp.sum(kt, axis=1, keepdims=True)
            logits.append(_dot(qs, kt.astype(BF16)))
            vals.append(("t", kv_refs[j][0, 0, GROUP_W:2 * GROUP_W, :].astype(BF16)))
        kmean_t = jnp.where(lane == n, ksum * (1.0 / MOBA_BLOCK), kmean_t)
    qh, ql = _split2(qs32)
    kh, kl = _split2(kmean_t)
    sc = _dot(qh, kh) + _dot(qh, kl) + _dot(ql, kh)
    sc = jnp.where(lane < nb, sc, NEG_INF)
    sel = _top_select(sc, min(MOBA_TOPK, nb + 1), 1)
    masks = []
    for j in range(n_pages):
        n = j // per_blk
        masks.append(jnp.broadcast_to(sel[:, n:n + 1] > 0.5, (GROUP_HEADS * TS, page)))
    kvn = _pad_rows(kvn_ref[...], LANES).astype(BF16)
    logits.append(_dot_nt(qs, kvn[:, 0:GROUP_W]))
    vals.append(("n", kvn[:, GROUP_W:2 * GROUP_W]))
    kpos_new = past + lane
    masks.append(kpos_new <= _sample_pos(past))
    p = _masked_softmax(jnp.concatenate(logits, axis=1), jnp.concatenate(masks, axis=1)).astype(BF16)
    o_ref[...] = _rms_rows(_unstack_heads(_weighted_values(p, vals), TS), og_ref[...])


def _moba_sample(page_table, layer, mq, mkv_new, pool_kv_t, og):
    n_seq, n_pages = page_table.shape
    page = pool_kv_t.shape[-1]
    assert page == LANES and (n_pages * page) % MOBA_BLOCK == 0 and TS <= MOBA_BLOCK
    grid_spec = pltpu.PrefetchScalarGridSpec(
        num_scalar_prefetch=1, grid=(n_seq,),
        in_specs=[_seq_spec(GROUP_W), _seq_spec(2 * GROUP_W), _const_spec((1, GROUP_W))]
                 + _page_specs(layer, (2 * GROUP_W, page), n_pages),
        out_specs=_seq_spec(GROUP_W))
    return pl.pallas_call(
        functools.partial(_moba_sample_kernel, n_pages=n_pages, page=page),
        grid_spec=grid_spec,
        out_shape=jax.ShapeDtypeStruct((n_seq * TS, GROUP_W), F32),
        compiler_params=_cparams(("arbitrary",)),
        name="moba_sample",
    )(page_table, mq, mkv_new, og, *([pool_kv_t] * n_pages))


def _nsa_sample_kernel(pt_ref, qn_ref, qr_ref, small_ref, kvn_ref, winn_ref, win_ref, plo_ref, phi_ref, wlo_ref,
                       whi_ref, g_ref, rep_ref, ovl_ref, gexp_ref, og_ref, *refs, n_pages, page, ns, n_sel):
    pg_refs = refs[:n_pages]
    o_ref = refs[n_pages]
    tok_s, x2_s = refs[n_pages + 1:n_pages + 3]
    past = n_pages * page
    n_chunks = past // CMP_STRIDE
    pos = past + lax.broadcasted_iota(jnp.int32, (TS, 1), 0)
    pos_s = _rep4(pos)
    qn_s = _stack_heads(qn_ref[...]).astype(BF16)
    qr_s = _stack_heads(qr_ref[...]).astype(BF16)
    rep = rep_ref[...]
    rows4 = lambda a: jnp.concatenate([a] * GROUP_HEADS, axis=0)

    for p in range(n_pages):
        tok_s[p * page:(p + 1) * page, :] = pg_refs[p][0, 0, 0:LANES, :].T
    for j in range(CMP_STRIDE):
        x2_s[:, j * LANES:(j + 1) * LANES] = tok_s[pl.ds(j, n_chunks, stride=CMP_STRIDE), :]
    c = _cmp_windows(x2_s[...], plo_ref, phi_ref, wlo_ref, whi_ref, g_ref, rep_ref, n_chunks - 1)
    c16 = c.astype(BF16)

    lc = _dot_nt(qn_s, c16[:, 0:GROUP_W])
    cend = lax.broadcasted_iota(jnp.int32, (1, n_chunks), 1) * CMP_STRIDE + (CMP_LEN - 1)
    pc = _masked_softmax(lc, cend <= pos_s)
    o_c = _dot(pc.astype(BF16), c16[:, GROUP_W:2 * GROUP_W])

    pcs = pc[0:TS] + pc[TS:2 * TS] + pc[2 * TS:3 * TS] + pc[3 * TS:4 * TS]
    imp = _dot_x3(pcs, ovl_ref[...])
    jb = lax.broadcasted_iota(jnp.int32, (1, imp.shape[-1]), 1)
    cur = pos // SEL_BLOCK
    imp = jnp.where((jb == 0) | (jb == cur) | (jb == cur - 1), jnp.inf, imp)
    imp = jnp.where((jb * SEL_BLOCK > pos) | (jb >= ns), NEG_INF, imp)
    sel = _top_select(imp, n_sel, 1)

    lane = lax.broadcasted_iota(jnp.int32, (1, page), 1)
    per_blk = page // SEL_BLOCK
    logits, masks, vals = [], [], []
    for p in range(n_pages):
        logits.append(_dot(qr_s, rows4(pg_refs[p][0, 0, 2 * HEAD_DIM:3 * HEAD_DIM, :]).astype(BF16)))
        vals.append(("t", rows4(pg_refs[p][0, 0, 3 * HEAD_DIM:4 * HEAD_DIM, :]).astype(BF16)))
        chosen = jnp.zeros((TS, page), F32)
        for b in range(per_blk):
            n = p * per_blk + b
            chosen = jnp.where(lane // SEL_BLOCK == b, sel[:, n:n + 1], chosen)
        masks.append(_rep4(chosen > 0.5))
    kvn = _dot(_pad_rows(kvn_ref[:, LANES:2 * LANES], LANES).astype(BF16), rep).astype(BF16)
    logits.append(_dot_nt(qr_s, kvn[:, 0:GROUP_W]))
    vals.append(("n", kvn[:, GROUP_W:2 * GROUP_W]))
    kpos_new = past + lax.broadcasted_iota(jnp.int32, (1, LANES), 1)
    n_new = past // SEL_BLOCK
    chosen_new = (kpos_new // SEL_BLOCK == n_new) & (sel[:, n_new:n_new + 1] > 0.5)
    masks.append(_rep4(chosen_new) & (kpos_new <= pos_s))
    ps = _masked_softmax(jnp.concatenate(logits, axis=1), jnp.concatenate(masks, axis=1)).astype(BF16)
    o_s = _weighted_values(ps, vals)

    wb = win_ref.shape[-1]
    kwt = rows4(win_ref[0, 0, 0:HEAD_DIM, :]).astype(BF16)
    vwt = rows4(win_ref[0, 0, HEAD_DIM:2 * HEAD_DIM, :]).astype(BF16)
    kvwn = _dot(_pad_rows(winn_ref[...], LANES).astype(BF16), rep).astype(BF16)
    kpos = jnp.concatenate([past - wb + lax.broadcasted_iota(jnp.int32, (1, wb), 1), kpos_new], axis=1)
    lw = jnp.concatenate([_dot(qr_s, kwt), _dot_nt(qr_s, kvwn[:, 0:GROUP_W])], axis=1)
    wmask = (kpos >= past - wb) & (kpos <= pos_s) & (kpos >= pos_s - WINDOW)
    pw = _masked_softmax(lw, wmask).astype(BF16)
    o_w = _dot_nt(pw[:, 0:wb], vwt) + _dot(pw[:, wb:wb + LANES], kvwn[:, GROUP_W:2 * GROUP_W])

    o_ref[...] = _nsa_gate_mix(_unstack_heads(o_c, TS), _unstack_heads(o_s, TS), _unstack_heads(o_w, TS),
                               small_ref[...], gexp_ref, og_ref)


def _nsa_sample(page_table, layer, r, win_state_t, pool_kv_t, nw, og):
    n_seq, n_pages = page_table.shape
    page = pool_kv_t.shape[-1]
    past = n_pages * page
    assert page == LANES and TS < CMP_STRIDE and past % MOBA_BLOCK == 0 and TS <= SEL_BLOCK
    n_chunks = past // CMP_STRIDE
    ns = (-(-(past + TS) // MOBA_BLOCK) * MOBA_BLOCK) // SEL_BLOCK
    ns_pad = -(-ns // LANES) * LANES
    ovl = jnp.asarray(_overlap_matrix(n_chunks, n_chunks - 1, ns, ns_pad), BF16)
    gexp = jnp.asarray(_gate_expand_matrix(), BF16)
    width = CMP_STRIDE * LANES
    wb = win_state_t.shape[-1]
    grid_spec = pltpu.PrefetchScalarGridSpec(
        num_scalar_prefetch=1, grid=(n_seq,),
        in_specs=[_seq_spec(GROUP_W), _seq_spec(GROUP_W), _seq_spec(LANES), _seq_spec(GROUP_W), _seq_spec(LANES),
                  pl.BlockSpec((1, 1, 2 * HEAD_DIM, wb), lambda s, pt: (layer, s, 0, 0)),
                  _const_spec((1, width)), _const_spec((1, width)),
                  _const_spec((width, LANES)), _const_spec((width, LANES)),
                  _const_spec((1, LANES)), _const_spec((LANES, 2 * GROUP_W)),
                  _const_spec((n_chunks, ns_pad)), _const_spec((LANES, 3 * GROUP_W)), _const_spec((1, GROUP_W))]
                 + _page_specs(layer, (GROUP_W, page), n_pages),
        out_specs=_seq_spec(GROUP_W),
        scratch_shapes=[pltpu.VMEM((past, LANES), F32), pltpu.VMEM((n_chunks, width), F32)])
    return pl.pallas_call(
        functools.partial(_nsa_sample_kernel, n_pages=n_pages, page=page, ns=ns, n_sel=min(SEL_TOPK, ns)),
        grid_spec=grid_spec,
        out_shape=jax.ShapeDtypeStruct((n_seq * TS, GROUP_W), F32),
        compiler_params=_cparams(("arbitrary",)),
        name="nsa_sample",
    )(page_table, r["nqn"], r["nqr"], r["small"], r["nkv"], r["nwin"], win_state_t,
      nw["plo"], nw["phi"], nw["wlo"], nw["whi"], nw["gcmp"], nw["rep"], ovl, gexp, og,
      *([pool_kv_t] * n_pages))


def _sample_layer(x, mod_rows, lw, layer, pools, page_table, n_seq, t_real, tabs):
    rows = n_seq * TS
    tm = min(512, rows)
    sh1, sc1, gt1, sh2, sc2, gt2 = [m.reshape(rows // tm, tm, D_MODEL) for m in jnp.split(mod_rows, 6, axis=-1)]
    r = _in_proj(x, sc1, sh1, lw["g1"], lw["w_in"], tabs, lw["gains"], lw["bf_row"],
                 tiles_per_seq=1, prompt=False, tm=tm)
    o_lru, h_t, conv_t = _lru_sample(r["zl"], pools["lru_h"][layer], jnp.transpose(pools["lru_conv"][layer], (1, 0, 2)),
                                     lw["lru"], n_seq, t_real)
    lf_new = r["small"][:, :GROUP_HEADS].reshape(n_seq, TS, GROUP_HEADS)
    lf_new_t = jnp.pad(jnp.transpose(lf_new, (0, 2, 1)), ((0, 0), (0, 0), (0, LANES - TS)))
    o_fox = _fox_sample(page_table, layer, r["fq"], r["fkv"], lf_new_t, pools["fox_kv_t"], pools["fox_lf_t"],
                        lw["og"][1])
    o_moba = _moba_sample(page_table, layer, r["mq"], r["mkv"], pools["moba_kv_t"], lw["og"][2])
    o_nsa = _nsa_sample(page_table, layer, r, pools["nsa_win_t"], pools["nsa_kv_t"], lw["nsa"], lw["og"][3])
    x1, hf = _out_proj(x, (o_lru, o_fox, o_moba, o_nsa), lw["w_out"], gt1, sc2, sh2, lw["g2"], 1, tm)
    y = _mlp(x1, hf, lw["w_up"], lw["w_down"], gt2, 1, tm, 1024)
    real = lambda a, tail: a.reshape((n_seq, TS) + tail)[:, :t_real]
    win_new = real(r["nwin"], (2, HEAD_DIM))
    win_prev = pools["nsa_win"][layer]
    keep = min(WINDOW, win_prev.shape[1] + t_real)
    win_all = jnp.concatenate([win_prev, win_new], axis=1)
    state = (real(r["fkv"], (2, GROUP_HEADS, HEAD_DIM)), real(r["small"][:, :GROUP_HEADS], (GROUP_HEADS,)),
             real(r["mkv"], (2, GROUP_HEADS, HEAD_DIM)), real(r["nkv"], (4, HEAD_DIM)),
             win_all[:, win_all.shape[1] - keep:], h_t, jnp.transpose(conv_t, (1, 0, 2)))
    return y, state


def _sample_trunk(x_sample, mods, lws, caches, page_table):
    n_seq, t_real, _ = x_sample.shape
    assert t_real <= TS
    depth, n_phys, page = caches["fox_kv"].shape[:3]
    past = page_table.shape[1] * page
    rows = n_seq * TS
    x = jnp.pad(x_sample, ((0, 0), (0, TS - t_real), (0, 0))).reshape(rows, D_MODEL)
    tm = min(512, rows)
    tabs = _rope_tables(past + (jnp.arange(tm, dtype=jnp.int32) % TS))
    pools = dict(
        fox_kv_t=jnp.transpose(caches["fox_kv"], (0, 1, 3, 4, 5, 2)).reshape(depth, n_phys, 2 * GROUP_W, page),
        fox_lf_t=jnp.transpose(caches["fox_logf"], (0, 1, 3, 2)),
        moba_kv_t=jnp.transpose(caches["moba_kv"], (0, 1, 3, 4, 5, 2)).reshape(depth, n_phys, 2 * GROUP_W, page),
        nsa_kv_t=jnp.transpose(caches["nsa_kv"], (0, 1, 3, 4, 2)).reshape(depth, n_phys, GROUP_W, page),
        nsa_win_t=jnp.transpose(caches["nsa_win"], (0, 1, 3, 4, 2)).reshape(depth, n_seq, 2 * HEAD_DIM, -1),
        nsa_win=caches["nsa_win"], lru_h=caches["lru_h"], lru_conv=caches["lru_conv"])
    states = []
    for l, lw in enumerate(lws):
        mod_rows = jnp.repeat(mods[l], TS, axis=0)
        x, st = _sample_layer(x, mod_rows, lw, l, pools, page_table, n_seq, t_real, tabs)
        states.append(st)
    stacked = [jnp.stack([s[i] for s in states], axis=0) for i in range(len(states[0]))]
    return x.reshape(n_seq, TS, D_MODEL)[:, :t_real], stacked


def kernel(x_prompt, x_sample, cache_fox_kv, cache_fox_logf, cache_moba_kv, cache_nsa_kv, state_nsa_win,
           state_lru_h, state_lru_conv, page_table, c_prompt, c_sample, norm_g, w_ada, b_ada, w_in,
           lru_conv_w, lru_conv_b, lru_wa, lru_ba, lru_wx, lru_bx, lru_lambda, fox_bf, fox_qk_g, moba_qk_g,
           nsa_qk_g, nsa_cmp_pos, nsa_cmp_w, out_g, w_out, w_up, w_down):
    p = dict(norm_g=norm_g, w_in=w_in, lru_conv_w=lru_conv_w, lru_conv_b=lru_conv_b, lru_wa=lru_wa,
             lru_ba=lru_ba, lru_wx=lru_wx, lru_bx=lru_bx, lru_lambda=lru_lambda, fox_bf=fox_bf,
             fox_qk_g=fox_qk_g, moba_qk_g=moba_qk_g, nsa_qk_g=nsa_qk_g, nsa_cmp_pos=nsa_cmp_pos,
             nsa_cmp_w=nsa_cmp_w, out_g=out_g, w_out=w_out, w_up=w_up, w_down=w_down)
    depth = w_in.shape[0]
    batch = x_prompt.shape[0]
    dec_batch = x_sample.shape[0]
    n_c = batch + dec_batch
    n_c_pad = -(-n_c // 8) * 8
    c_all = jnp.concatenate([c_prompt, c_sample, jnp.zeros((n_c_pad - n_c, D_MODEL), F32)], axis=0)
    mods = _ada(c_all, w_ada, b_ada)
    lws = [_layer_weights(p, l) for l in range(depth)]
    y_prompt, st_p = _prompt_trunk(x_prompt, [mods[l, :batch] for l in range(depth)], lws)
    caches = dict(fox_kv=cache_fox_kv, fox_logf=cache_fox_logf, moba_kv=cache_moba_kv, nsa_kv=cache_nsa_kv,
                  nsa_win=state_nsa_win, lru_h=state_lru_h, lru_conv=state_lru_conv)
    y_sample, st_s = _sample_trunk(x_sample, [mods[l, batch:n_c] for l in range(depth)], lws, caches, page_table)
    return (y_prompt, y_sample) + tuple(st_p) + tuple(st_s)
```

```python
import functools

import numpy as np
import jax
import jax.numpy as jnp
from jax import lax
from jax.experimental import pallas as pl
from jax.experimental.pallas import tpu as pltpu

F32 = jnp.float32
BF16 = jnp.bfloat16

D_MODEL = 1024
N_MIXERS = 4
GROUP_W = 256
HEAD_DIM = 64
GROUP_HEADS = 4
D_FF = 4 * D_MODEL
CONV_W = 4
LRU_C = 8.0
ROPE_THETA = 500000.0
ROPE_DIM = HEAD_DIM // 4
ROPE_HALF = ROPE_DIM // 2
MOBA_BLOCK = 256
MOBA_TOPK = 3
CMP_LEN = 32
CMP_STRIDE = 16
SEL_BLOCK = 64
SEL_TOPK = 16
WINDOW = 512
EPS = 1e-6
LOG2E = float(np.log2(np.e))
QSCALE = HEAD_DIM ** -0.5 * LOG2E
LANES = 128
IN_W = 22 * LANES
NEG_INF = float("-inf")
VMEM_LIMIT = 56 * 1024 * 1024

TS = 8
TQ = LANES
TK = MOBA_BLOCK
VT_ROWS = GROUP_W + 16


def _cparams(sem):
    return pltpu.CompilerParams(dimension_semantics=sem, vmem_limit_bytes=VMEM_LIMIT)


def _dot(a, b):
    return jnp.dot(a, b, preferred_element_type=F32)


def _dot_nt(a, b):
    return lax.dot_general(a, b, (((1,), (1,)), ((), ())), preferred_element_type=F32)


def _split2(x):
    hi = x.astype(BF16)
    lo = (x - hi.astype(F32)).astype(BF16)
    return hi, lo


def _split3(x):
    hi = x.astype(BF16)
    r = x - hi.astype(F32)
    mid = r.astype(BF16)
    lo = (r - mid.astype(F32)).astype(BF16)
    return hi, mid, lo


def _dot_x3(x, m):
    hi, mid, lo = _split3(x)
    return _dot(hi, m) + _dot(mid, m) + _dot(lo, m)


def _dot_3x(m, x):
    hi, mid, lo = _split3(x)
    return _dot(m, hi) + _dot(m, mid) + _dot(m, lo)


def _log_sigmoid(x):
    return jnp.minimum(x, 0.0) - jnp.log1p(jnp.exp(-jnp.abs(x)))


def _softplus(x):
    return jnp.maximum(x, 0.0) + jnp.log1p(jnp.exp(-jnp.abs(x)))


def _gelu_tanh(x):
    return 0.5 * x * (1.0 + jnp.tanh(np.sqrt(2.0 / np.pi).astype(np.float32) * (x + 0.044715 * (x * x * x))))


def _rms_rows(x, g):
    return x * lax.rsqrt(jnp.mean(x * x, axis=-1, keepdims=True) + EPS) * g


def _head_lane(width=GROUP_W):
    return lax.broadcasted_iota(jnp.int32, (1, width), 1) // HEAD_DIM


def _stack_heads(q):
    hl = _head_lane()
    zero = jnp.zeros_like(q)
    return jnp.concatenate([jnp.where(hl == h, q, zero) for h in range(GROUP_HEADS)], axis=0)


def _unstack_heads(o, t):
    hl = _head_lane()
    out = jnp.zeros((t, GROUP_W), o.dtype)
    for h in range(GROUP_HEADS):
        out = out + jnp.where(hl == h, o[h * t:(h + 1) * t], 0.0)
    return out


def _rep4(a):
    return jnp.concatenate([a] * GROUP_HEADS, axis=0)


def _lanes4(a):
    return jnp.concatenate([a] * GROUP_HEADS, axis=1)


def _masked_softmax(l, mask, axis=-1):
    if mask is not None:
        l = jnp.where(mask, l, NEG_INF)
    m = jnp.max(l, axis=axis, keepdims=True)
    m = jnp.where(m > NEG_INF, m, 0.0)
    p = jnp.exp2(l - m)
    return p / jnp.maximum(jnp.sum(p, axis=axis, keepdims=True), 1e-30)


def _top_select(score, n_pick, axis):
    width = score.shape[axis]
    shape = [1, 1]
    shape[axis] = width
    j = lax.broadcasted_iota(jnp.int32, tuple(shape), axis)
    sel = jnp.zeros(score.shape, F32)
    work = score
    for _ in range(n_pick):
        mx = jnp.max(work, axis=axis, keepdims=True)
        cand = (work == mx) & (mx > NEG_INF)
        idx = jnp.min(jnp.where(cand, j, width), axis=axis, keepdims=True)
        pick = j == idx
        sel = jnp.where(pick, 1.0, sel)
        work = jnp.where(pick, NEG_INF, work)
    return sel


def _stacked_q_t(q):
    qt = q.astype(F32).T
    row_head = lax.broadcasted_iota(jnp.int32, (GROUP_W, 1), 0) // HEAD_DIM
    return jnp.concatenate([jnp.where(row_head == h, qt, 0.0) for h in range(GROUP_HEADS)], axis=1).astype(BF16)


def _flash_init(m_s, acc_s):
    m_s[...] = jnp.full_like(m_s, NEG_INF)
    acc_s[...] = jnp.zeros_like(acc_s)


def _flash_step(parts, m_s, acc_s, row_bias=None):
    m_old = m_s[...]
    m_new = m_old
    for s, _ in parts:
        tile_max = jnp.max(s, axis=0, keepdims=True)
        m_new = jnp.maximum(m_new, tile_max if row_bias is None else tile_max + row_bias)
    m_safe = jnp.where(m_new > NEG_INF, m_new, 0.0)
    acc = jnp.exp2(m_old - m_safe) * acc_s[...]
    shift = m_safe if row_bias is None else m_safe - row_bias
    for s, vt in parts:
        acc = acc + _dot(vt, jnp.exp2(s - shift).astype(BF16))
    acc_s[...] = acc
    m_s[...] = m_new


def _flash_pipelined(n_tiles, logits_fn, vt_fn, s_buf, m_s, acc_s, row_bias=None):
    n_pairs = (n_tiles + 1) // 2

    def produce(pair, slot):
        s_buf[slot, 0] = logits_fn(2 * pair)
        s_buf[slot, 1] = logits_fn(2 * pair + 1)

    def half_step(pair, src, dst):
        produce(pair + 1, dst)
        _flash_step([(s_buf[src, 0], vt_fn(2 * pair)), (s_buf[src, 1], vt_fn(2 * pair + 1))], m_s, acc_s, row_bias)

    produce(0, 0)

    def body(j, carry):
        half_step(2 * j, 0, 1)

        @pl.when(2 * j + 1 < n_pairs)
        def _():
            half_step(2 * j + 1, 1, 0)

        return carry

    lax.fori_loop(0, (n_pairs + 1) // 2, body, 0)


def _unstack_t(acc, scale):
    a = acc * scale
    parts = [a[h * HEAD_DIM:(h + 1) * HEAD_DIM, h * TQ:(h + 1) * TQ] for h in range(GROUP_HEADS)]
    return jnp.concatenate(parts, axis=0).T


def _flash_result(acc_s):
    return _unstack_t(acc_s[0:GROUP_W, :], 1.0 / jnp.maximum(acc_s[GROUP_W:GROUP_W + 1, :], 1e-30))


def _flash_scratch():
    return [pltpu.VMEM((1, GROUP_HEADS * TQ), F32), pltpu.VMEM((VT_ROWS, GROUP_HEADS * TQ), F32),
            pltpu.VMEM((2, 2, TK, GROUP_HEADS * TQ), F32)]


def _resident(shape, index_map):
    return pl.BlockSpec(shape, index_map, pipeline_mode=pl.Buffered(1))


def _key_pos(k0, n):
    return k0 + lax.broadcasted_iota(jnp.int32, (n, 1), 0)


def _query_pos(q0):
    return q0 + lax.broadcasted_iota(jnp.int32, (1, TQ), 1)


def _ada_kernel(c_ref, w_ref, b_ref, o_ref):
    c = c_ref[...]
    a = (c * jax.nn.sigmoid(c)).astype(BF16)
    o_ref[0] = _dot(a, w_ref[0].astype(BF16)) + b_ref[0]


def _ada(c_all, w_ada, b_ada):
    depth, _, n = w_ada.shape
    rows = c_all.shape[0]
    tn = 1024
    return pl.pallas_call(
        _ada_kernel,
        grid=(depth, n // tn),
        in_specs=[pl.BlockSpec((rows, D_MODEL), lambda l, j: (0, 0)),
                  pl.BlockSpec((1, D_MODEL, tn), lambda l, j: (l, 0, j)),
                  pl.BlockSpec((1, 1, tn), lambda l, j: (l, 0, j))],
        out_specs=pl.BlockSpec((1, rows, tn), lambda l, j: (l, 0, j)),
        out_shape=jax.ShapeDtypeStruct((depth, rows, n), F32),
        compiler_params=_cparams(("arbitrary", "arbitrary")),
        name="ada_mod",
    )(c_all, w_ada, b_ada.reshape(depth, 1, n))


def _in_perm():
    gw, hd = GROUP_W, HEAD_DIM
    nkv0 = 9 * gw + GROUP_HEADS
    cols = list(range(0, 5 * gw))
    cols += list(range(5 * gw + 4, 8 * gw + 4))
    cols += list(range(8 * gw + 4, 9 * gw + 4))
    part = lambda k: list(range(nkv0 + k * hd, nkv0 + (k + 1) * hd))
    cols += part(0) + part(1)
    cols += part(2) + part(3)
    cols += part(4) + part(5)
    cols += list(range(5 * gw, 5 * gw + 4))
    cols += list(range(nkv0 + 6 * hd, nkv0 + 6 * hd + 12))
    return np.asarray(cols, np.int32)


C_LRU, C_FQ, C_FK, C_FV, C_MQ, C_MK, C_MV, C_NQ, C_KCVC, C_SEL, C_WIN, C_SMALL = (
    0, 512, 768, 1024, 1280, 1536, 1792, 2048, 2304, 2432, 2560, 2688)


def _block_diag_mean():
    m = np.zeros((GROUP_W, GROUP_W), np.float32)
    for h in range(GROUP_HEADS):
        m[h * HEAD_DIM:(h + 1) * HEAD_DIM, h * HEAD_DIM:(h + 1) * HEAD_DIM] = 1.0 / HEAD_DIM
    return m


def _rep_matrix():
    m = np.zeros((LANES, 2 * GROUP_W), np.float32)
    for j in range(HEAD_DIM):
        for k in range(GROUP_HEADS):
            m[j, k * HEAD_DIM + j] = 1.0
            m[HEAD_DIM + j, GROUP_W + k * HEAD_DIM + j] = 1.0
    return m


def _head_expand_matrix():
    m = np.zeros((LANES, GROUP_HEADS * LANES), np.float32)
    for h in range(GROUP_HEADS):
        m[h, h * LANES:(h + 1) * LANES] = 1.0
    return m


def _rope_tables(pos):
    inv = jnp.power(ROPE_THETA, -jnp.arange(ROPE_HALF, dtype=F32) / ROPE_HALF)
    ang = pos.astype(F32)[:, None] * inv[None, :]
    cos, sin = jnp.cos(ang), jnp.sin(ang)
    n = pos.shape[0]
    ones = jnp.ones((n, HEAD_DIM - ROPE_DIM), F32)
    zeros = jnp.zeros((n, HEAD_DIM - ROPE_DIM), F32)
    z8 = jnp.zeros((n, ROPE_HALF), F32)
    c = jnp.concatenate([cos, cos, ones], axis=1)
    sa = jnp.concatenate([z8, sin, zeros], axis=1)
    sb = jnp.concatenate([-sin, z8, zeros], axis=1)
    two = lambda a: jnp.concatenate([a, a], axis=1)
    return two(c), two(sa), two(sb)


_IN_COMMON = ("zl", "fq", "fkv", "small", "mq", "mkv", "nqn", "nqr", "nkv", "nwin")
_IN_PROMPT = ("fk16", "fvt", "fcrep", "mk16", "mvt", "kmean", "ks16", "vst", "kw16", "vwt", "kcvc")


def _in_proj_kernel(*refs, tiles_per_seq, prompt, tm):
    n_in = 14 if prompt else 12
    (x_ref, sc_ref, sh_ref, g_ref, w_ref, tc_ref, ta_ref, tb_ref, gains_ref, bf_ref, bd_ref,
     rep_ref) = refs[:12]
    names = _IN_COMMON + (_IN_PROMPT if prompt else ())
    out = dict(zip(names, refs[n_in:n_in + len(names)]))

    x = x_ref[...]
    xn = x * lax.rsqrt(jnp.mean(x * x, axis=-1, keepdims=True) + EPS) * g_ref[...]
    hm = (xn * (1.0 + sc_ref[0]) + sh_ref[0]).astype(BF16)

    def proj(c0, width):
        return _dot(hm, w_ref[:, c0:c0 + width])

    bd = bd_ref[...]
    tc, ta, tb = tc_ref[...], ta_ref[...], tb_ref[...]

    def hnorm(z, row):
        hi, lo = _split2(z * z)
        ms = _dot(hi, bd) + _dot(lo, bd)
        return z * lax.rsqrt(ms + EPS) * gains_ref[row:row + 1, :]

    def rope128(y):
        return y * tc + pltpu.roll(y, 8, axis=1) * ta + pltpu.roll(y, LANES - 8, axis=1) * tb

    def rope(y):
        return jnp.concatenate([rope128(y[:, :LANES]), rope128(y[:, LANES:])], axis=1)

    def store_t(name, a, tile, ones_rows=0):
        at = a.T
        for j in range(tm // tile):
            out[name][j, 0:GROUP_W, :] = at[:, j * tile:(j + 1) * tile].astype(BF16)
            if ones_rows:
                out[name][j, GROUP_W:GROUP_W + ones_rows, :] = jnp.ones((ones_rows, tile), BF16)

    out["zl"][...] = proj(C_LRU, 512)

    out["fq"][...] = (hnorm(proj(C_FQ, 256), 0) * QSCALE).astype(out["fq"].dtype)
    fk = hnorm(proj(C_FK, 256), 1)
    fv = proj(C_FV, 256)
    out["fkv"][...] = jnp.concatenate([fk, fv], axis=1)

    out["mq"][...] = rope(hnorm(proj(C_MQ, 256), 2)) * QSCALE
    mk = rope(hnorm(proj(C_MK, 256), 3))
    mv = proj(C_MV, 256)
    out["mkv"][...] = jnp.concatenate([mk, mv], axis=1)

    nqn = hnorm(proj(C_NQ, 256), 4)
    out["nqn"][...] = (nqn * QSCALE).astype(out["nqn"].dtype)
    out["nqr"][...] = (rope(nqn) * QSCALE).astype(out["nqr"].dtype)

    kcvc = proj(C_KCVC, LANES)
    sw = proj(C_SEL, 2 * LANES)
    first = (lax.broadcasted_iota(jnp.int32, (1, 2 * LANES), 1) % LANES) < HEAD_DIM
    swn = jnp.where(first, rope(hnorm(sw, 5)), sw)
    sel_rows = swn[:, :LANES]
    win_rows = swn[:, LANES:]
    out["nkv"][...] = jnp.concatenate([kcvc, sel_rows], axis=1)
    out["nwin"][...] = win_rows

    zs = proj(C_SMALL, LANES)
    lane = lax.broadcasted_iota(jnp.int32, (1, LANES), 1)
    logf = _log_sigmoid(zs + bf_ref[...])
    gate = jax.nn.sigmoid(zs)
    out["small"][...] = jnp.where(lane < GROUP_HEADS, logf, jnp.where(lane < 16, gate, 0.0))

    if prompt:
        tri_ref, fexp_ref = refs[12:14]
        carry_ref = refs[n_in + len(names)]
        out["fk16"][...] = fk.astype(BF16)
        store_t("fvt", fv, TK, VT_ROWS - GROUP_W)
        out["mk16"][...] = mk.astype(BF16)
        store_t("mvt", mv, TK, VT_ROWS - GROUP_W)
        out["kcvc"][...] = kcvc
        rep = rep_ref[...]
        sel_rep = _dot(sel_rows.astype(BF16), rep)
        win_rep = _dot(win_rows.astype(BF16), rep)
        out["ks16"][...] = sel_rep[:, :GROUP_W].astype(BF16)
        store_t("vst", sel_rep[:, GROUP_W:], TK, VT_ROWS - GROUP_W)
        out["kw16"][...] = win_rep[:, :GROUP_W].astype(BF16)
        store_t("vwt", win_rep[:, GROUP_W:], TQ)

        i = pl.program_id(0)

        @pl.when(i % tiles_per_seq == 0)
        def _():
            carry_ref[...] = jnp.zeros_like(carry_ref)

        lf = jnp.where(lane < GROUP_HEADS, logf, 0.0)
        fc = _dot_3x(tri_ref[...], lf) + carry_ref[...]
        carry_ref[...] = fc[tm - 1:tm, :]
        out["fcrep"][...] = _dot_x3(fc, fexp_ref[...]) * LOG2E
        nblk = tm // MOBA_BLOCK
        out["kmean"][0] = jnp.sum(mk.reshape(nblk, MOBA_BLOCK, GROUP_W), axis=1) * (1.0 / MOBA_BLOCK)


def _in_proj(x, sc, sh, g, w, tabs, gains, bf_row, *, tiles_per_seq, prompt, tm):
    rows = x.shape[0]
    n_tiles = rows // tm
    tb = sc.shape[1]
    n_tab_tiles = tabs[0].shape[0] // tm
    mod_map = (lambda i: (i // tiles_per_seq, 0, 0)) if tb == 1 else (lambda i: (i, 0, 0))
    tab_map = lambda i: (i % n_tab_tiles, 0)
    const = lambda i: (0, 0)
    row_map = lambda i: (i, 0)
    bd = jnp.asarray(_block_diag_mean(), BF16)
    rep = jnp.asarray(_rep_matrix(), BF16)
    in_specs = [pl.BlockSpec((tm, D_MODEL), row_map),
                pl.BlockSpec((1, tb, D_MODEL), mod_map),
                pl.BlockSpec((1, tb, D_MODEL), mod_map),
                pl.BlockSpec((1, D_MODEL), const),
                pl.BlockSpec((D_MODEL, IN_W), const),
                pl.BlockSpec((tm, LANES), tab_map),
                pl.BlockSpec((tm, LANES), tab_map),
                pl.BlockSpec((tm, LANES), tab_map),
                pl.BlockSpec((8, GROUP_W), const),
                pl.BlockSpec((1, LANES), const),
                pl.BlockSpec((GROUP_W, GROUP_W), const),
                pl.BlockSpec((LANES, 2 * GROUP_W), const)]
    args = [x, sc, sh, g, w, tabs[0], tabs[1], tabs[2], gains, bf_row, bd, rep]
    if prompt:
        tri = jnp.asarray(np.tril(np.ones((tm, tm), np.float32)), BF16)
        fexp = jnp.asarray(_head_expand_matrix(), BF16)
        in_specs += [pl.BlockSpec((tm, tm), const), pl.BlockSpec((LANES, GROUP_HEADS * LANES), const)]
        args += [tri, fexp]

    def o(width, dtype):
        return jax.ShapeDtypeStruct((rows, width), dtype), pl.BlockSpec((tm, width), row_map)

    def ot(tile, n_rows=GROUP_W):
        return (jax.ShapeDtypeStruct((rows // tile, n_rows, tile), BF16),
                pl.BlockSpec((tm // tile, n_rows, tile), lambda i: (i, 0, 0)))

    q_dtype = BF16 if prompt else F32
    outs = dict(zl=o(512, F32), fq=o(256, q_dtype), fkv=o(512, F32), small=o(LANES, F32), mq=o(256, F32),
                mkv=o(512, F32), nqn=o(256, q_dtype), nqr=o(256, q_dtype), nkv=o(256, F32), nwin=o(LANES, F32))
    scratch = []
    if prompt:
        nblk = tm // MOBA_BLOCK
        outs.update(fk16=o(256, BF16), fvt=ot(TK, VT_ROWS), fcrep=o(GROUP_HEADS * LANES, F32), mk16=o(256, BF16),
                    mvt=ot(TK, VT_ROWS),
                    kmean=(jax.ShapeDtypeStruct((n_tiles, nblk, GROUP_W), F32),
                           pl.BlockSpec((1, nblk, GROUP_W), lambda i: (i, 0, 0))),
                    ks16=o(256, BF16), vst=ot(TK, VT_ROWS), kw16=o(256, BF16), vwt=ot(TQ), kcvc=o(LANES, F32))
        scratch.append(pltpu.VMEM((1, LANES), F32))
    names = _IN_COMMON + (_IN_PROMPT if prompt else ())
    res = pl.pallas_call(
        functools.partial(_in_proj_kernel, tiles_per_seq=tiles_per_seq, prompt=prompt, tm=tm),
        grid=(n_tiles,),
        in_specs=in_specs,
        out_specs=[outs[n][1] for n in names],
        out_shape=[outs[n][0] for n in names],
        scratch_shapes=scratch,
        compiler_params=_cparams(("arbitrary",)),
        name="in_proj_prompt" if prompt else "in_proj_sample",
    )(*args)
    return dict(zip(names, res))


def _lru_coeffs(u, wa_ref, wx_ref, ba_ref, bx_ref, lam_ref):
    u16 = u.astype(BF16)
    r = jax.nn.sigmoid(_dot(u16, wa_ref[...]) + ba_ref[...])
    i = jax.nn.sigmoid(_dot(u16, wx_ref[...]) + bx_ref[...])
    log_a = -LRU_C * r * _softplus(-lam_ref[...])
    a = jnp.exp(log_a)
    b = jnp.sqrt(-jnp.tanh(log_a) * (a * a + 1.0)) * i * u
    return a, b


def _lru_prompt_kernel(zl_ref, cw_ref, cb_ref, wa_ref, wx_ref, ba_ref, bx_ref, lam_ref, og_ref,
                       o_ref, ht_ref, cnew_ref, xp_s, a_s, b_s, hs_s, h_s, *, tm):
    t = pl.program_id(1)

    @pl.when(t == 0)
    def _():
        h_s[...] = jnp.zeros_like(h_s)
        xp_s[0:8, :] = jnp.zeros((8, GROUP_W), F32)

    xa = zl_ref[:, 0:GROUP_W]
    ga = zl_ref[:, GROUP_W:2 * GROUP_W]
    xp_s[8:8 + tm, :] = xa
    u = (cb_ref[...] + cw_ref[0:1, :] * xp_s[5:5 + tm, :] + cw_ref[1:2, :] * xp_s[6:6 + tm, :]
         + cw_ref[2:3, :] * xp_s[7:7 + tm, :] + cw_ref[3:4, :] * xa)
    tail = xp_s[tm:tm + 8, :]
    xp_s[0:8, :] = tail
    cnew_ref[0] = tail
    a, b = _lru_coeffs(u, wa_ref, wx_ref, ba_ref, bx_ref, lam_ref)
    a_s[...] = a
    b_s[...] = b

    def body(j, h):
        r0 = pl.multiple_of(j * 8, 8)
        a8 = a_s[pl.ds(r0, 8), :]
        b8 = b_s[pl.ds(r0, 8), :]
        rows = []
        for k in range(8):
            h = a8[k:k + 1, :] * h + b8[k:k + 1, :]
            rows.append(h)
        hs_s[pl.ds(r0, 8), :] = jnp.concatenate(rows, axis=0)
        return h

    h = lax.fori_loop(0, tm // 8, body, h_s[...])
    h_s[...] = h
    ht_ref[0] = h
    o = hs_s[...] * _gelu_tanh(ga)
    o_ref[...] = _rms_rows(o, og_ref[...]).astype(BF16)


def _lru_prompt(zl, lw, batch, seq, tm):
    nt = seq // tm
    const = lambda b, t: (0, 0)
    vec = pl.BlockSpec((1, GROUP_W), const)
    mat = pl.BlockSpec((GROUP_W, GROUP_W), const)
    return pl.pallas_call(
        functools.partial(_lru_prompt_kernel, tm=tm),
        grid=(batch, nt),
        in_specs=[pl.BlockSpec((tm, 2 * GROUP_W), lambda b, t: (b * nt + t, 0)),
                  pl.BlockSpec((CONV_W, GROUP_W), const), vec, mat, mat, vec, vec, vec, vec],
        out_specs=[pl.BlockSpec((tm, GROUP_W), lambda b, t: (b * nt + t, 0)),
                   pl.BlockSpec((1, 1, GROUP_W), lambda b, t: (b, 0, 0)),
                   pl.BlockSpec((1, 8, GROUP_W), lambda b, t: (b, 0, 0))],
        out_shape=[jax.ShapeDtypeStruct((batch * seq, GROUP_W), BF16),
                   jax.ShapeDtypeStruct((batch, 1, GROUP_W), F32),
                   jax.ShapeDtypeStruct((batch, 8, GROUP_W), F32)],
        scratch_shapes=[pltpu.VMEM((tm + 8, GROUP_W), F32), pltpu.VMEM((tm, GROUP_W), F32),
                        pltpu.VMEM((tm, GROUP_W), F32), pltpu.VMEM((tm, GROUP_W), F32),
                        pltpu.VMEM((1, GROUP_W), F32)],
        compiler_params=_cparams(("arbitrary", "arbitrary")),
        name="lru_prompt",
    )(zl, lw["cw"], lw["cb"], lw["wa"], lw["wx"], lw["ba"], lw["bx"], lw["lam"], lw["og"])


def _fox_prompt_kernel(q_ref, fcq_ref, k_ref, vt_ref, fck_ref, og_ref, o_ref, m_s, acc_s, s_buf):
    qi = pl.program_id(1)
    q0 = qi * TQ
    qst = _stacked_q_t(q_ref[...])
    fcq = fcq_ref[...]
    fq_row = jnp.concatenate([fcq[:, h * LANES:(h + 1) * LANES].T[0:1, :] for h in range(GROUP_HEADS)], axis=1)
    qpos = _lanes4(_query_pos(q0))
    _flash_init(m_s, acc_s)
    last = vt_ref.shape[0] - 1

    def logits(t):
        k0 = pl.multiple_of(jnp.minimum(t, last) * TK, TK)
        s = _dot(k_ref[pl.ds(k0, TK), :], qst) - fck_ref[pl.ds(k0, TK), :]
        return jnp.where(_key_pos(t * TK, TK) <= qpos, s, NEG_INF)

    _flash_pipelined(q0 // TK + 1, logits, lambda t: vt_ref[jnp.minimum(t, last)], s_buf, m_s, acc_s, fq_row)
    o_ref[...] = _rms_rows(_flash_result(acc_s), og_ref[...]).astype(BF16)


def _fox_prompt(r, og, batch, seq):
    nq = seq // TQ
    nk = seq // TK
    row = lambda b, i: (b * nq + i, 0)
    return pl.pallas_call(
        _fox_prompt_kernel,
        grid=(batch, nq),
        in_specs=[pl.BlockSpec((TQ, GROUP_W), row),
                  pl.BlockSpec((TQ, GROUP_HEADS * LANES), row),
                  _resident((seq, GROUP_W), lambda b, i: (b, 0)),
                  _resident((nk, VT_ROWS, TK), lambda b, i: (b, 0, 0)),
                  _resident((seq, GROUP_HEADS * LANES), lambda b, i: (b, 0)),
                  pl.BlockSpec((1, GROUP_W), lambda b, i: (0, 0))],
        out_specs=pl.BlockSpec((TQ, GROUP_W), row),
        out_shape=jax.ShapeDtypeStruct((batch * seq, GROUP_W), BF16),
        scratch_shapes=_flash_scratch(),
        compiler_params=_cparams(("arbitrary", "arbitrary")),
        name="fox_prompt",
    )(r["fq"], r["fcrep"], r["fk16"], r["fvt"], r["fcrep"], og)


def _moba_prompt_kernel(q_ref, kmean_ref, k_ref, vt_ref, og_ref, o_ref, m_s, acc_s, s_buf, sel_s, *, nb):
    qi = pl.program_id(1)
    q0 = qi * TQ
    cur = q0 // MOBA_BLOCK
    q32 = q_ref[...]
    qt = q32.T
    row_head = lax.broadcasted_iota(jnp.int32, (GROUP_W, 1), 0) // HEAD_DIM
    qst32 = jnp.concatenate([jnp.where(row_head == h, qt, 0.0) for h in range(GROUP_HEADS)], axis=1)
    qst = qst32.astype(BF16)
    qh, ql = _split2(qst32)
    kh, kl = _split2(kmean_ref[0])
    sc = _dot(kh, qh) + _dot(kh, ql) + _dot(kl, qh)
    n_col = lax.broadcasted_iota(jnp.int32, (nb, 1), 0)
    sel = _top_select(jnp.where(n_col < cur, sc, NEG_INF), min(MOBA_TOPK, nb), 0)
    sel_s[...] = jnp.where(n_col == cur, 1.0, sel)
    _flash_init(m_s, acc_s)
    qpos = _lanes4(_query_pos(q0))

    def logits(n):
        nc = jnp.minimum(n, nb - 1)
        k0 = pl.multiple_of(nc * MOBA_BLOCK, MOBA_BLOCK)
        s = _dot(k_ref[pl.ds(k0, MOBA_BLOCK), :], qst)
        keep = (sel_s[pl.ds(nc, 1), :] > 0.5) & (_key_pos(n * MOBA_BLOCK, MOBA_BLOCK) <= qpos)
        return jnp.where(keep, s, NEG_INF)

    _flash_pipelined(cur + 1, logits, lambda n: vt_ref[jnp.minimum(n, nb - 1)], s_buf, m_s, acc_s)
    o_ref[...] = _rms_rows(_flash_result(acc_s), og_ref[...]).astype(BF16)


def _moba_prompt(r, og, batch, seq):
    assert TK == MOBA_BLOCK
    nq = seq // TQ
    nb = seq // MOBA_BLOCK
    kmean = r["kmean"].reshape(batch, nb, GROUP_W)
    row = lambda b, i: (b * nq + i, 0)
    return pl.pallas_call(
        functools.partial(_moba_prompt_kernel, nb=nb),
        grid=(batch, nq),
        in_specs=[pl.BlockSpec((TQ, GROUP_W), row),
                  pl.BlockSpec((1, nb, GROUP_W), lambda b, i: (b, 0, 0)),
                  _resident((seq, GROUP_W), lambda b, i: (b, 0)),
                  _resident((nb, VT_ROWS, TK), lambda b, i: (b, 0, 0)),
                  pl.BlockSpec((1, GROUP_W), lambda b, i: (0, 0))],
        out_specs=pl.BlockSpec((TQ, GROUP_W), row),
        out_shape=jax.ShapeDtypeStruct((batch * seq, GROUP_W), BF16),
        scratch_shapes=_flash_scratch() + [pltpu.VMEM((nb, GROUP_HEADS * TQ), F32)],
        compiler_params=_cparams(("arbitrary", "arbitrary")),
        name="moba_prompt",
    )(r["mq"], kmean, r["mk16"], r["mvt"], og)


def _cmp_weights(cw):
    w = cw.reshape(2, CMP_LEN, HEAD_DIM, HEAD_DIM)
    zero = jnp.zeros((CMP_STRIDE, HEAD_DIM, HEAD_DIM), cw.dtype)

    def half(j0):
        wk = w[0, j0:j0 + CMP_STRIDE]
        wv = w[1, j0:j0 + CMP_STRIDE]
        top = jnp.concatenate([wk, zero], axis=2)
        bot = jnp.concatenate([zero, wv], axis=2)
        return jnp.concatenate([top, bot], axis=1).reshape(CMP_STRIDE * LANES, LANES)

    return half(0).astype(BF16), half(CMP_STRIDE).astype(BF16)


def _cmp_pos(cpos):
    p = jnp.concatenate([cpos[0], cpos[1]], axis=1)
    return p[:CMP_STRIDE].reshape(1, -1), p[CMP_STRIDE:].reshape(1, -1)


def _cmp_windows(x, plo_ref, phi_ref, wlo_ref, whi_ref, g_ref, rep_ref, n_valid):
    y_lo = _dot((x + plo_ref[...]).astype(BF16), wlo_ref[...])
    y_hi = _dot((x + phi_ref[...]).astype(BF16), whi_ref[...])
    rows = y_lo.shape[0]
    y = y_lo + pltpu.roll(y_hi, rows - 1, axis=0)
    lane = lax.broadcasted_iota(jnp.int32, (1, LANES), 1)
    is_k = lane < HEAD_DIM
    ms = jnp.sum(jnp.where(is_k, y * y, 0.0), axis=-1, keepdims=True) * (1.0 / HEAD_DIM)
    y = jnp.where(is_k, y * lax.rsqrt(ms + EPS) * g_ref[...], y)
    row = lax.broadcasted_iota(jnp.int32, (rows, 1), 0)
    y = jnp.where(row < n_valid, y, 0.0)
    return _dot(y.astype(BF16), rep_ref[...])


def _cmp_prompt_kernel(x_ref, plo_ref, phi_ref, wlo_ref, whi_ref, g_ref, rep_ref, k_ref, vt_ref, *, n_valid):
    c = _cmp_windows(x_ref[...], plo_ref, phi_ref, wlo_ref, whi_ref, g_ref, rep_ref, n_valid)
    k_ref[...] = c[:, :GROUP_W].astype(BF16)
    vt_ref[0] = c[:, GROUP_W:].T.astype(BF16)


def _cmp_prompt(kcvc, nw, batch, seq):
    rows = seq // CMP_STRIDE
    width = CMP_STRIDE * LANES
    x2 = kcvc.reshape(batch * rows, width)
    n_valid = (seq - CMP_LEN) // CMP_STRIDE + 1
    const = lambda b: (0, 0)
    return pl.pallas_call(
        functools.partial(_cmp_prompt_kernel, n_valid=n_valid),
        grid=(batch,),
        in_specs=[pl.BlockSpec((rows, width), lambda b: (b, 0)),
                  pl.BlockSpec((1, width), const), pl.BlockSpec((1, width), const),
                  pl.BlockSpec((width, LANES), const), pl.BlockSpec((width, LANES), const),
                  pl.BlockSpec((1, LANES), const), pl.BlockSpec((LANES, 2 * GROUP_W), const)],
        out_specs=[pl.BlockSpec((rows, GROUP_W), lambda b: (b, 0)),
                   pl.BlockSpec((1, GROUP_W, rows), lambda b: (b, 0, 0))],
        out_shape=[jax.ShapeDtypeStruct((batch * rows, GROUP_W), BF16),
                   jax.ShapeDtypeStruct((batch, GROUP_W, rows), BF16)],
        compiler_params=_cparams(("arbitrary",)),
        name="nsa_cmp_prompt",
    )(x2, nw["plo"], nw["phi"], nw["wlo"], nw["whi"], nw["gcmp"], nw["rep"])


def _overlap_matrix(n_rows, n_valid, ns, ns_pad):
    c = np.arange(n_rows)[:, None] * CMP_STRIDE
    b = np.arange(ns_pad)[None, :] * SEL_BLOCK
    m = (c < b + SEL_BLOCK) & (c + CMP_LEN > b)
    m &= (np.arange(n_rows)[:, None] < n_valid) & (np.arange(ns_pad)[None, :] < ns)
    return m.astype(np.float32)


def _gate_expand_matrix():
    m = np.zeros((LANES, 3 * GROUP_W), np.float32)
    for h in range(GROUP_HEADS):
        for g in range(3):
            m[4 + h * 3 + g, g * GROUP_W + h * HEAD_DIM:g * GROUP_W + (h + 1) * HEAD_DIM] = 1.0
    return m


def _nsa_gate_mix(o_c, o_s, o_w, small, gexp_ref, og_ref):
    gx = _dot_x3(small, gexp_ref[...])
    o = gx[:, 0:GROUP_W] * o_c + gx[:, GROUP_W:2 * GROUP_W] * o_s + gx[:, 2 * GROUP_W:3 * GROUP_W] * o_w
    return _rms_rows(o, og_ref[...])


def _nsa_prompt_kernel(qn_ref, qr_ref, small_ref, kc_ref, vct_ref, ovlt_ref, gexp_ref, ks_ref, vst_ref,
                       kw0_ref, kw1_ref, kw2_ref, kw3_ref, kw4_ref,
                       vw0_ref, vw1_ref, vw2_ref, vw3_ref, vw4_ref, og_ref, o_ref,
                       m_s, acc_s, s_buf, *, ns, n_sel):
    qi = pl.program_id(1)
    q0 = qi * TQ
    pos = _query_pos(q0)
    pos4 = _lanes4(pos)
    qnt = _stacked_q_t(qn_ref[...])
    qrt = _stacked_q_t(qr_ref[...])

    nc = kc_ref.shape[0]
    cend = lax.broadcasted_iota(jnp.int32, (nc, 1), 0) * CMP_STRIDE + (CMP_LEN - 1)
    pc = _masked_softmax(_dot(kc_ref[...], qnt), cend <= pos4, axis=0)
    o_c = _unstack_t(_dot(vct_ref[0], pc.astype(BF16)), 1.0)

    pcs = pc[:, 0:TQ] + pc[:, TQ:2 * TQ] + pc[:, 2 * TQ:3 * TQ] + pc[:, 3 * TQ:4 * TQ]
    imp = _dot_3x(ovlt_ref[...], pcs)
    ns_pad = imp.shape[0]
    j = lax.broadcasted_iota(jnp.int32, (ns_pad, 1), 0)
    cur = pos // SEL_BLOCK
    imp = jnp.where((j == 0) | (j == cur) | (j == cur - 1), jnp.inf, imp)
    imp = jnp.where((j * SEL_BLOCK > pos) | (j >= ns), NEG_INF, imp)
    sel16 = _top_select(imp, n_sel, 0).astype(BF16)

    _flash_init(m_s, acc_s)
    nk = (q0 + TQ + TK - 1) // TK
    blk_row = lax.broadcasted_iota(jnp.int32, (1, ns_pad), 1)
    key_blk = lax.broadcasted_iota(jnp.int32, (TK, 1), 0) // SEL_BLOCK

    last = vst_ref.shape[0] - 1

    def sel_logits(t):
        tc = jnp.minimum(t, last)
        k0 = pl.multiple_of(tc * TK, TK)
        s = _dot(ks_ref[pl.ds(k0, TK), :], qrt)
        expand = (blk_row == tc * (TK // SEL_BLOCK) + key_blk).astype(BF16)
        mask = (_dot(expand, sel16) > 0.5) & (_key_pos(t * TK, TK) <= pos)
        return s + _lanes4(jnp.where(mask, 0.0, NEG_INF))

    _flash_pipelined(nk, sel_logits, lambda t: vst_ref[jnp.minimum(t, last)], s_buf, m_s, acc_s)
    o_s = _flash_result(acc_s)

    kws = (kw0_ref, kw1_ref, kw2_ref, kw3_ref, kw4_ref)
    vws = (vw0_ref, vw1_ref, vw2_ref, vw3_ref, vw4_ref)
    logits = []
    for t, kw in enumerate(kws):
        kpos = _key_pos(q0 + (t - 4) * TQ, TQ)
        mask = (kpos >= 0) & (kpos <= pos4) & (kpos >= pos4 - WINDOW)
        logits.append(jnp.where(mask, _dot(kw[...], qrt), NEG_INF))
    pw = _masked_softmax(jnp.concatenate(logits, axis=0), None, axis=0).astype(BF16)
    o_wt = jnp.zeros((GROUP_W, GROUP_HEADS * TQ), F32)
    for t, vw in enumerate(vws):
        o_wt = o_wt + _dot(vw[0], pw[t * TQ:(t + 1) * TQ, :])
    o_w = _unstack_t(o_wt, 1.0)

    o_ref[...] = _nsa_gate_mix(o_c, o_s, o_w, small_ref[...], gexp_ref, og_ref).astype(BF16)


def _nsa_prompt(r, kc_rep, vct_rep, og, batch, seq):
    assert WINDOW == 4 * TQ
    nq = seq // TQ
    nk = seq // TK
    nc = seq // CMP_STRIDE
    ns = seq // SEL_BLOCK
    ns_pad = -(-ns // LANES) * LANES
    n_valid = (seq - CMP_LEN) // CMP_STRIDE + 1
    ovlt = jnp.asarray(_overlap_matrix(nc, n_valid, ns, ns_pad).T, BF16)
    gexp = jnp.asarray(_gate_expand_matrix(), BF16)
    const = lambda b, i: (0, 0)
    row = lambda b, i: (b * nq + i, 0)
    kw_spec = lambda t: pl.BlockSpec((TQ, GROUP_W), lambda b, i: (b * nq + jnp.maximum(i + t - 4, 0), 0))
    vw_spec = lambda t: pl.BlockSpec((1, GROUP_W, TQ), lambda b, i: (b * nq + jnp.maximum(i + t - 4, 0), 0, 0))
    return pl.pallas_call(
        functools.partial(_nsa_prompt_kernel, ns=ns, n_sel=min(SEL_TOPK, ns)),
        grid=(batch, nq),
        in_specs=[pl.BlockSpec((TQ, GROUP_W), row), pl.BlockSpec((TQ, GROUP_W), row),
                  pl.BlockSpec((TQ, LANES), row),
                  pl.BlockSpec((nc, GROUP_W), lambda b, i: (b, 0)),
                  pl.BlockSpec((1, GROUP_W, nc), lambda b, i: (b, 0, 0)),
                  pl.BlockSpec((ns_pad, nc), const),
                  pl.BlockSpec((LANES, 3 * GROUP_W), const),
                  _resident((seq, GROUP_W), lambda b, i: (b, 0)),
                  _resident((nk, VT_ROWS, TK), lambda b, i: (b, 0, 0))]
                 + [kw_spec(t) for t in range(5)] + [vw_spec(t) for t in range(5)]
                 + [pl.BlockSpec((1, GROUP_W), const)],
        out_specs=pl.BlockSpec((TQ, GROUP_W), row),
        out_shape=jax.ShapeDtypeStruct((batch * seq, GROUP_W), BF16),
        scratch_shapes=_flash_scratch(),
        compiler_params=_cparams(("arbitrary", "arbitrary")),
        name="nsa_prompt",
    )(r["nqn"], r["nqr"], r["small"], kc_rep, vct_rep, ovlt, gexp, r["ks16"], r["vst"],
      *([r["kw16"]] * 5), *([r["vwt"]] * 5), og)


def _out_proj_kernel(x_ref, o0_ref, o1_ref, o2_ref, o3_ref, w_ref, gt_ref, sc_ref, sh_ref, g_ref,
                     x1_ref, hf_ref):
    o_cat = jnp.concatenate([o0_ref[...], o1_ref[...], o2_ref[...], o3_ref[...]], axis=1).astype(BF16)
    x1 = x_ref[...] + gt_ref[0] * _dot(o_cat, w_ref[...])
    x1_ref[...] = x1
    hf_ref[...] = (_rms_rows(x1, g_ref[...]) * (1.0 + sc_ref[0]) + sh_ref[0]).astype(BF16)


def _out_proj(x, outs, w_out, gt1, sc2, sh2, g2, tiles_per_seq, tm):
    rows = x.shape[0]
    tb = gt1.shape[1]
    mod_map = (lambda i: (i // tiles_per_seq, 0, 0)) if tb == 1 else (lambda i: (i, 0, 0))
    row = lambda i: (i, 0)
    const = lambda i: (0, 0)
    mod = pl.BlockSpec((1, tb, D_MODEL), mod_map)
    return pl.pallas_call(
        _out_proj_kernel,
        grid=(rows // tm,),
        in_specs=[pl.BlockSpec((tm, D_MODEL), row)] + [pl.BlockSpec((tm, GROUP_W), row)] * 4
                 + [pl.BlockSpec((D_MODEL, D_MODEL), const), mod, mod, mod, pl.BlockSpec((1, D_MODEL), const)],
        out_specs=[pl.BlockSpec((tm, D_MODEL), row), pl.BlockSpec((tm, D_MODEL), row)],
        out_shape=[jax.ShapeDtypeStruct((rows, D_MODEL), F32), jax.ShapeDtypeStruct((rows, D_MODEL), BF16)],
        compiler_params=_cparams(("arbitrary",)),
        name="out_proj",
    )(x, *outs, w_out, gt1, sc2, sh2, g2)


def _mlp_kernel(x1_ref, hf_ref, wu_ref, wd_ref, gt_ref, y_ref, acc_s):
    f = pl.program_id(1)

    @pl.when(f == 0)
    def _():
        acc_s[...] = jnp.zeros_like(acc_s)

    up = jnp.maximum(_dot(hf_ref[...], wu_ref[...]), 0.0)
    acc_s[...] += _dot((up * up).astype(BF16), wd_ref[...])

    @pl.when(f == pl.num_programs(1) - 1)
    def _():
        y_ref[...] = x1_ref[...] + gt_ref[0] * acc_s[...]


def _mlp(x1, hf, w_up, w_down, gt2, tiles_per_seq, tm, tf):
    rows = x1.shape[0]
    tb = gt2.shape[1]
    mod_map = (lambda i, f: (i // tiles_per_seq, 0, 0)) if tb == 1 else (lambda i, f: (i, 0, 0))
    row = lambda i, f: (i, 0)
    return pl.pallas_call(
        _mlp_kernel,
        grid=(rows // tm, D_FF // tf),
        in_specs=[pl.BlockSpec((tm, D_MODEL), row), pl.BlockSpec((tm, D_MODEL), row),
                  pl.BlockSpec((D_MODEL, tf), lambda i, f: (0, f)),
                  pl.BlockSpec((tf, D_MODEL), lambda i, f: (f, 0)),
                  pl.BlockSpec((1, tb, D_MODEL), mod_map)],
        out_specs=pl.BlockSpec((tm, D_MODEL), row),
        out_shape=jax.ShapeDtypeStruct((rows, D_MODEL), F32),
        scratch_shapes=[pltpu.VMEM((tm, D_MODEL), F32)],
        compiler_params=_cparams(("arbitrary", "arbitrary")),
        name="mlp",
    )(x1, hf, w_up, w_down, gt2)


def _block_diag(w):
    out = jnp.zeros((GROUP_W, GROUP_W), w.dtype)
    for h in range(GROUP_HEADS):
        out = out.at[h * HEAD_DIM:(h + 1) * HEAD_DIM, h * HEAD_DIM:(h + 1) * HEAD_DIM].set(w[h])
    return out


def _layer_weights(p, l):
    tile4 = lambda g: jnp.tile(g, GROUP_HEADS)
    ones = jnp.ones((HEAD_DIM,), F32)
    w_in = jnp.concatenate([p["w_in"][l][:, _in_perm()],
                            jnp.zeros((D_MODEL, IN_W - _in_perm().shape[0]), F32)], axis=1).astype(BF16)
    ng = p["nsa_qk_g"][l]
    gains = jnp.stack([tile4(p["fox_qk_g"][l, 0]), tile4(p["fox_qk_g"][l, 1]),
                       tile4(p["moba_qk_g"][l, 0]), tile4(p["moba_qk_g"][l, 1]),
                       tile4(ng[0]), jnp.concatenate([ng[2], ones, ng[3], ones]),
                       jnp.ones((GROUP_W,), F32), jnp.ones((GROUP_W,), F32)])
    bf_row = jnp.zeros((1, LANES), F32).at[0, :GROUP_HEADS].set(p["fox_bf"][l])
    row = lambda a: a.reshape(1, -1)
    lru = dict(cw=p["lru_conv_w"][l], cb=row(p["lru_conv_b"][l]),
               wa=_block_diag(p["lru_wa"][l]).astype(BF16), wx=_block_diag(p["lru_wx"][l]).astype(BF16),
               ba=row(p["lru_ba"][l]), bx=row(p["lru_bx"][l]), lam=row(p["lru_lambda"][l]),
               og=row(p["out_g"][l, 0]))
    wlo, whi = _cmp_weights(p["nsa_cmp_w"][l])
    plo, phi = _cmp_pos(p["nsa_cmp_pos"][l])
    gcmp = jnp.concatenate([ng[1], ones]).reshape(1, LANES)
    nsa = dict(wlo=wlo, whi=whi, plo=plo, phi=phi, gcmp=gcmp, rep=jnp.asarray(_rep_matrix(), BF16))
    return dict(w_in=w_in, gains=gains, bf_row=bf_row, lru=lru, nsa=nsa,
                g1=row(p["norm_g"][l, 0]), g2=row(p["norm_g"][l, 1]),
                og=[row(p["out_g"][l, k]) for k in range(N_MIXERS)],
                w_out=p["w_out"][l].astype(BF16), w_up=p["w_up"][l].astype(BF16),
                w_down=p["w_down"][l].astype(BF16))


def _prompt_layer(x, mod, lw, batch, seq, tabs):
    tm = min(512, seq)
    tps = seq // tm
    sh1, sc1, gt1, sh2, sc2, gt2 = [m[:, None, :] for m in jnp.split(mod, 6, axis=-1)]
    r = _in_proj(x, sc1, sh1, lw["g1"], lw["w_in"], tabs, lw["gains"], lw["bf_row"],
                 tiles_per_seq=tps, prompt=True, tm=tm)
    o_lru, h_t, conv_t = _lru_prompt(r["zl"], lw["lru"], batch, seq, tm)
    o_fox = _fox_prompt(r, lw["og"][1], batch, seq)
    o_moba = _moba_prompt(r, lw["og"][2], batch, seq)
    kc_rep, vct_rep = _cmp_prompt(r["kcvc"], lw["nsa"], batch, seq)
    o_nsa = _nsa_prompt(r, kc_rep, vct_rep, lw["og"][3], batch, seq)
    x1, hf = _out_proj(x, (o_lru, o_fox, o_moba, o_nsa), lw["w_out"], gt1, sc2, sh2, lw["g2"], tps, tm)
    y = _mlp(x1, hf, lw["w_up"], lw["w_down"], gt2, tps, tm, 1024)
    keep = min(WINDOW, seq)
    state = (r["fkv"].reshape(batch, seq, 2, GROUP_HEADS, HEAD_DIM),
             r["small"][:, :GROUP_HEADS].reshape(batch, seq, GROUP_HEADS),
             r["mkv"].reshape(batch, seq, 2, GROUP_HEADS, HEAD_DIM),
             r["nkv"].reshape(batch, seq, 4, HEAD_DIM),
             r["nwin"].reshape(batch, seq, 2, HEAD_DIM)[:, seq - keep:],
             h_t.reshape(batch, GROUP_W),
             conv_t[:, 8 - (CONV_W - 1):, :])
    return y, state


def _prompt_trunk(x_prompt, mods, lws):
    batch, seq, _ = x_prompt.shape
    assert seq % max(512, TK) == 0 or seq in (256,)
    tabs = _rope_tables(jnp.arange(seq, dtype=jnp.int32))
    x = x_prompt.reshape(batch * seq, D_MODEL)
    states = []
    for l, lw in enumerate(lws):
        x, st = _prompt_layer(x, mods[l], lw, batch, seq, tabs)
        states.append(st)
    stacked = [jnp.stack([s[i] for s in states], axis=0) for i in range(len(states[0]))]
    return x.reshape(batch, seq, D_MODEL), stacked


def _lru_sample_kernel(zl_ref, h0_ref, cbuf_ref, cw_ref, cb_ref, wa_ref, wx_ref, ba_ref, bx_ref, lam_ref,
                       og_ref, o_ref, ht_ref, cnew_ref, *, n_seq, t_real):
    o_ref[...] = jnp.zeros_like(o_ref)
    xs = [cbuf_ref[k] for k in range(CONV_W - 1)] + [zl_ref[t, :, 0:GROUP_W] for t in range(t_real)]
    h = h0_ref[...]
    for t in range(t_real):
        u = cb_ref[...]
        for k in range(CONV_W):
            u = u + xs[t + k] * cw_ref[k:k + 1, :]
        a, b = _lru_coeffs(u, wa_ref, wx_ref, ba_ref, bx_ref, lam_ref)
        h = a * h + b
        o = h * _gelu_tanh(zl_ref[t, :, GROUP_W:2 * GROUP_W])
        o_ref[t] = _rms_rows(o, og_ref[...])
    ht_ref[...] = h
    for k in range(CONV_W - 1):
        cnew_ref[k] = xs[t_real + k]


def _lru_sample(zl, h0, cbuf, lw, n_seq, t_real):
    assert t_real >= CONV_W - 1
    zl_t = jnp.transpose(zl.reshape(n_seq, TS, 2 * GROUP_W), (1, 0, 2))
    o_t, h_t, conv_t = pl.pallas_call(
        functools.partial(_lru_sample_kernel, n_seq=n_seq, t_real=t_real),
        out_shape=[jax.ShapeDtypeStruct((TS, n_seq, GROUP_W), F32),
                   jax.ShapeDtypeStruct((n_seq, GROUP_W), F32),
                   jax.ShapeDtypeStruct((CONV_W - 1, n_seq, GROUP_W), F32)],
        compiler_params=pltpu.CompilerParams(vmem_limit_bytes=VMEM_LIMIT),
        name="lru_sample",
    )(zl_t, h0, cbuf, lw["cw"], lw["cb"], lw["wa"], lw["wx"], lw["ba"], lw["bx"], lw["lam"], lw["og"])
    return jnp.transpose(o_t, (1, 0, 2)).reshape(n_seq * TS, GROUP_W), h_t, conv_t


def _page_specs(layer, tail, n_pages):
    zeros = (0,) * len(tail)
    return [pl.BlockSpec((1, 1) + tail, functools.partial(lambda j, s, pt: (layer, pt[s, j]) + zeros, j))
            for j in range(n_pages)]


def _seq_spec(width, rows=TS):
    return pl.BlockSpec((rows, width), lambda s, pt: (s, 0))


def _const_spec(shape):
    zeros = (0,) * len(shape)
    return pl.BlockSpec(shape, lambda s, pt: zeros)


def _pad_rows(a, rows):
    return jnp.concatenate([a, jnp.zeros((rows - a.shape[0], a.shape[1]), a.dtype)], axis=0)


def _rep_rows(a, t):
    return jnp.concatenate([jnp.broadcast_to(a[h:h + 1, :], (t, a.shape[1])) for h in range(GROUP_HEADS)], axis=0)


def _sample_pos(past):
    return _rep4(past + lax.broadcasted_iota(jnp.int32, (TS, 1), 0))


def _weighted_values(p, vals):
    o = jnp.zeros((p.shape[0], GROUP_W), F32)
    for j, (kind, v) in enumerate(vals):
        pj = p[:, j * LANES:(j + 1) * LANES]
        o = o + (_dot_nt(pj, v) if kind == "t" else _dot(pj, v))
    return o


def _fox_sample_kernel(pt_ref, q_ref, kvn_ref, lfn_ref, u_ref, og_ref, *refs, n_pages, page):
    kv_refs = refs[:n_pages]
    lf_refs = refs[n_pages:2 * n_pages]
    o_ref = refs[2 * n_pages]
    past = n_pages * page
    qs = _stack_heads(q_ref[...]).astype(BF16)
    u = u_ref[...]
    off = jnp.zeros((GROUP_HEADS, 1), F32)
    logits, fcs, vals = [], [], []
    for j in range(n_pages):
        c = _dot_x3(lf_refs[j][0, 0], u) + off
        off = c[:, page - 1:page]
        fcs.append(c)
        logits.append(_dot(qs, kv_refs[j][0, 0, 0:GROUP_W, :].astype(BF16)))
        vals.append(("t", kv_refs[j][0, 0, GROUP_W:2 * GROUP_W, :].astype(BF16)))
    cn = _dot_x3(lfn_ref[0], u) + off
    fcs.append(cn)
    kvn = _pad_rows(kvn_ref[...], LANES).astype(BF16)
    logits.append(_dot_nt(qs, kvn[:, 0:GROUP_W]))
    vals.append(("n", kvn[:, GROUP_W:2 * GROUP_W]))

    lane = lax.broadcasted_iota(jnp.int32, (1, LANES), 1)
    t_row = lax.broadcasted_iota(jnp.int32, (GROUP_HEADS * TS, 1), 0) % TS
    fq = jnp.sum(jnp.where(lane == t_row, _rep_rows(cn, TS), 0.0), axis=-1, keepdims=True)
    decay = fq - jnp.concatenate([_rep_rows(c, TS) for c in fcs], axis=1)
    s = jnp.concatenate(logits, axis=1) + decay * LOG2E
    kpos = lax.broadcasted_iota(jnp.int32, (1, s.shape[1]), 1)
    p = _masked_softmax(s, kpos <= _sample_pos(past)).astype(BF16)
    o_ref[...] = _rms_rows(_unstack_heads(_weighted_values(p, vals), TS), og_ref[...])


def _fox_sample(page_table, layer, fq, fkv_new, lf_new_t, pool_kv_t, pool_lf_t, og):
    n_seq, n_pages = page_table.shape
    page = pool_kv_t.shape[-1]
    assert page == LANES
    u = jnp.asarray(np.triu(np.ones((page, page), np.float32)), BF16)
    grid_spec = pltpu.PrefetchScalarGridSpec(
        num_scalar_prefetch=1, grid=(n_seq,),
        in_specs=[_seq_spec(GROUP_W), _seq_spec(2 * GROUP_W),
                  pl.BlockSpec((1, GROUP_HEADS, LANES), lambda s, pt: (s, 0, 0)),
                  _const_spec((page, page)), _const_spec((1, GROUP_W))]
                 + _page_specs(layer, (2 * GROUP_W, page), n_pages)
                 + _page_specs(layer, (GROUP_HEADS, page), n_pages),
        out_specs=_seq_spec(GROUP_W))
    return pl.pallas_call(
        functools.partial(_fox_sample_kernel, n_pages=n_pages, page=page),
        grid_spec=grid_spec,
        out_shape=jax.ShapeDtypeStruct((n_seq * TS, GROUP_W), F32),
        compiler_params=_cparams(("arbitrary",)),
        name="fox_sample",
    )(page_table, fq, fkv_new, lf_new_t, u, og, *([pool_kv_t] * n_pages), *([pool_lf_t] * n_pages))


def _moba_sample_kernel(pt_ref, q_ref, kvn_ref, og_ref, *refs, n_pages, page):
    kv_refs = refs[:n_pages]
    o_ref = refs[n_pages]
    past = n_pages * page
    per_blk = MOBA_BLOCK // page
    nb = n_pages // per_blk
    qs32 = _stack_heads(q_ref[...])
    qs = qs32.astype(BF16)
    lane = lax.broadcasted_iota(jnp.int32, (1, LANES), 1)
    logits, vals = [], []
    kmean_t = jnp.zeros((GROUP_W, LANES), F32)
    for n in range(nb):
        ksum = jnp.zeros((GROUP_W, 1), F32)
        for j in range(n * per_blk, (n + 1) * per_blk):
            kt = kv_refs[j][0, 0, 0:GROUP_W, :]
            ksum = ksum + jnp.sum(kt, axis=1, keepdims=True)
            logits.append(_dot(qs, kt.astype(BF16)))
            vals.append(("t", kv_refs[j][0, 0, GROUP_W:2 * GROUP_W, :].astype(BF16)))
        kmean_t = jnp.where(lane == n, ksum * (1.0 / MOBA_BLOCK), kmean_t)
    qh, ql = _split2(qs32)
    kh, kl = _split2(kmean_t)
    sc = _dot(qh, kh) + _dot(qh, kl) + _dot(ql, kh)
    sc = jnp.where(lane < nb, sc, NEG_INF)
    sel = _top_select(sc, min(MOBA_TOPK, nb + 1), 1)
    masks = []
    for j in range(n_pages):
        n = j // per_blk
        masks.append(jnp.broadcast_to(sel[:, n:n + 1] > 0.5, (GROUP_HEADS * TS, page)))
    kvn = _pad_rows(kvn_ref[...], LANES).astype(BF16)
    logits.append(_dot_nt(qs, kvn[:, 0:GROUP_W]))
    vals.append(("n", kvn[:, GROUP_W:2 * GROUP_W]))
    kpos_new = past + lane
    masks.append(kpos_new <= _sample_pos(past))
    p = _masked_softmax(jnp.concatenate(logits, axis=1), jnp.concatenate(masks, axis=1)).astype(BF16)
    o_ref[...] = _rms_rows(_unstack_heads(_weighted_values(p, vals), TS), og_ref[...])


def _moba_sample(page_table, layer, mq, mkv_new, pool_kv_t, og):
    n_seq, n_pages = page_table.shape
    page = pool_kv_t.shape[-1]
    assert page == LANES and (n_pages * page) % MOBA_BLOCK == 0 and TS <= MOBA_BLOCK
    grid_spec = pltpu.PrefetchScalarGridSpec(
        num_scalar_prefetch=1, grid=(n_seq,),
        in_specs=[_seq_spec(GROUP_W), _seq_spec(2 * GROUP_W), _const_spec((1, GROUP_W))]
                 + _page_specs(layer, (2 * GROUP_W, page), n_pages),
        out_specs=_seq_spec(GROUP_W))
    return pl.pallas_call(
        functools.partial(_moba_sample_kernel, n_pages=n_pages, page=page),
        grid_spec=grid_spec,
        out_shape=jax.ShapeDtypeStruct((n_seq * TS, GROUP_W), F32),
        compiler_params=_cparams(("arbitrary",)),
        name="moba_sample",
    )(page_table, mq, mkv_new, og, *([pool_kv_t] * n_pages))


def _nsa_sample_kernel(pt_ref, qn_ref, qr_ref, small_ref, kvn_ref, winn_ref, win_ref, plo_ref, phi_ref, wlo_ref,
                       whi_ref, g_ref, rep_ref, ovl_ref, gexp_ref, og_ref, *refs, n_pages, page, ns, n_sel):
    pg_refs = refs[:n_pages]
    o_ref = refs[n_pages]
    tok_s, x2_s = refs[n_pages + 1:n_pages + 3]
    past = n_pages * page
    n_chunks = past // CMP_STRIDE
    pos = past + lax.broadcasted_iota(jnp.int32, (TS, 1), 0)
    pos_s = _rep4(pos)
    qn_s = _stack_heads(qn_ref[...]).astype(BF16)
    qr_s = _stack_heads(qr_ref[...]).astype(BF16)
    rep = rep_ref[...]
    rows4 = lambda a: jnp.concatenate([a] * GROUP_HEADS, axis=0)

    for p in range(n_pages):
        tok_s[p * page:(p + 1) * page, :] = pg_refs[p][0, 0, 0:LANES, :].T
    for j in range(CMP_STRIDE):
        x2_s[:, j * LANES:(j + 1) * LANES] = tok_s[pl.ds(j, n_chunks, stride=CMP_STRIDE), :]
    c = _cmp_windows(x2_s[...], plo_ref, phi_ref, wlo_ref, whi_ref, g_ref, rep_ref, n_chunks - 1)
    c16 = c.astype(BF16)

    lc = _dot_nt(qn_s, c16[:, 0:GROUP_W])
    cend = lax.broadcasted_iota(jnp.int32, (1, n_chunks), 1) * CMP_STRIDE + (CMP_LEN - 1)
    pc = _masked_softmax(lc, cend <= pos_s)
    o_c = _dot(pc.astype(BF16), c16[:, GROUP_W:2 * GROUP_W])

    pcs = pc[0:TS] + pc[TS:2 * TS] + pc[2 * TS:3 * TS] + pc[3 * TS:4 * TS]
    imp = _dot_x3(pcs, ovl_ref[...])
    jb = lax.broadcasted_iota(jnp.int32, (1, imp.shape[-1]), 1)
    cur = pos // SEL_BLOCK
    imp = jnp.where((jb == 0) | (jb == cur) | (jb == cur - 1), jnp.inf, imp)
    imp = jnp.where((jb * SEL_BLOCK > pos) | (jb >= ns), NEG_INF, imp)
    sel = _top_select(imp, n_sel, 1)

    lane = lax.broadcasted_iota(jnp.int32, (1, page), 1)
    per_blk = page // SEL_BLOCK
    logits, masks, vals = [], [], []
    for p in range(n_pages):
        logits.append(_dot(qr_s, rows4(pg_refs[p][0, 0, 2 * HEAD_DIM:3 * HEAD_DIM, :]).astype(BF16)))
        vals.append(("t", rows4(pg_refs[p][0, 0, 3 * HEAD_DIM:4 * HEAD_DIM, :]).astype(BF16)))
        chosen = jnp.zeros((TS, page), F32)
        for b in range(per_blk):
            n = p * per_blk + b
            chosen = jnp.where(lane // SEL_BLOCK == b, sel[:, n:n + 1], chosen)
        masks.append(_rep4(chosen > 0.5))
    kvn = _dot(_pad_rows(kvn_ref[:, LANES:2 * LANES], LANES).astype(BF16), rep).astype(BF16)
    logits.append(_dot_nt(qr_s, kvn[:, 0:GROUP_W]))
    vals.append(("n", kvn[:, GROUP_W:2 * GROUP_W]))
    kpos_new = past + lax.broadcasted_iota(jnp.int32, (1, LANES), 1)
    n_new = past // SEL_BLOCK
    chosen_new = (kpos_new // SEL_BLOCK == n_new) & (sel[:, n_new:n_new + 1] > 0.5)
    masks.append(_rep4(chosen_new) & (kpos_new <= pos_s))
    ps = _masked_softmax(jnp.concatenate(logits, axis=1), jnp.concatenate(masks, axis=1)).astype(BF16)
    o_s = _weighted_values(ps, vals)

    wb = win_ref.shape[-1]
    kwt = rows4(win_ref[0, 0, 0:HEAD_DIM, :]).astype(BF16)
    vwt = rows4(win_ref[0, 0, HEAD_DIM:2 * HEAD_DIM, :]).astype(BF16)
    kvwn = _dot(_pad_rows(winn_ref[...], LANES).astype(BF16), rep).astype(BF16)
    kpos = jnp.concatenate([past - wb + lax.broadcasted_iota(jnp.int32, (1, wb), 1), kpos_new], axis=1)
    lw = jnp.concatenate([_dot(qr_s, kwt), _dot_nt(qr_s, kvwn[:, 0:GROUP_W])], axis=1)
    wmask = (kpos >= past - wb) & (kpos <= pos_s) & (kpos >= pos_s - WINDOW)
    pw = _masked_softmax(lw, wmask).astype(BF16)
    o_w = _dot_nt(pw[:, 0:wb], vwt) + _dot(pw[:, wb:wb + LANES], kvwn[:, GROUP_W:2 * GROUP_W])

    o_ref[...] = _nsa_gate_mix(_unstack_heads(o_c, TS), _unstack_heads(o_s, TS), _unstack_heads(o_w, TS),
                               small_ref[...], gexp_ref, og_ref)


def _nsa_sample(page_table, layer, r, win_state_t, pool_kv_t, nw, og):
    n_seq, n_pages = page_table.shape
    page = pool_kv_t.shape[-1]
    past = n_pages * page
    assert page == LANES and TS < CMP_STRIDE and past % MOBA_BLOCK == 0 and TS <= SEL_BLOCK
    n_chunks = past // CMP_STRIDE
    ns = (-(-(past + TS) // MOBA_BLOCK) * MOBA_BLOCK) // SEL_BLOCK
    ns_pad = -(-ns // LANES) * LANES
    ovl = jnp.asarray(_overlap_matrix(n_chunks, n_chunks - 1, ns, ns_pad), BF16)
    gexp = jnp.asarray(_gate_expand_matrix(), BF16)
    width = CMP_STRIDE * LANES
    wb = win_state_t.shape[-1]
    grid_spec = pltpu.PrefetchScalarGridSpec(
        num_scalar_prefetch=1, grid=(n_seq,),
        in_specs=[_seq_spec(GROUP_W), _seq_spec(GROUP_W), _seq_spec(LANES), _seq_spec(GROUP_W), _seq_spec(LANES),
                  pl.BlockSpec((1, 1, 2 * HEAD_DIM, wb), lambda s, pt: (layer, s, 0, 0)),
                  _const_spec((1, width)), _const_spec((1, width)),
                  _const_spec((width, LANES)), _const_spec((width, LANES)),
                  _const_spec((1, LANES)), _const_spec((LANES, 2 * GROUP_W)),
                  _const_spec((n_chunks, ns_pad)), _const_spec((LANES, 3 * GROUP_W)), _const_spec((1, GROUP_W))]
                 + _page_specs(layer, (GROUP_W, page), n_pages),
        out_specs=_seq_spec(GROUP_W),
        scratch_shapes=[pltpu.VMEM((past, LANES), F32), pltpu.VMEM((n_chunks, width), F32)])
    return pl.pallas_call(
        functools.partial(_nsa_sample_kernel, n_pages=n_pages, page=page, ns=ns, n_sel=min(SEL_TOPK, ns)),
        grid_spec=grid_spec,
        out_shape=jax.ShapeDtypeStruct((n_seq * TS, GROUP_W), F32),
        compiler_params=_cparams(("arbitrary",)),
        name="nsa_sample",
    )(page_table, r["nqn"], r["nqr"], r["small"], r["nkv"], r["nwin"], win_state_t,
      nw["plo"], nw["phi"], nw["wlo"], nw["whi"], nw["gcmp"], nw["rep"], ovl, gexp, og,
      *([pool_kv_t] * n_pages))


def _sample_layer(x, mod_rows, lw, layer, pools, page_table, n_seq, t_real, tabs):
    rows = n_seq * TS
    tm = min(512, rows)
    sh1, sc1, gt1, sh2, sc2, gt2 = [m.reshape(rows // tm, tm, D_MODEL) for m in jnp.split(mod_rows, 6, axis=-1)]
    r = _in_proj(x, sc1, sh1, lw["g1"], lw["w_in"], tabs, lw["gains"], lw["bf_row"],
                 tiles_per_seq=1, prompt=False, tm=tm)
    o_lru, h_t, conv_t = _lru_sample(r["zl"], pools["lru_h"][layer], jnp.transpose(pools["lru_conv"][layer], (1, 0, 2)),
                                     lw["lru"], n_seq, t_real)
    lf_new = r["small"][:, :GROUP_HEADS].reshape(n_seq, TS, GROUP_HEADS)
    lf_new_t = jnp.pad(jnp.transpose(lf_new, (0, 2, 1)), ((0, 0), (0, 0), (0, LANES - TS)))
    o_fox = _fox_sample(page_table, layer, r["fq"], r["fkv"], lf_new_t, pools["fox_kv_t"], pools["fox_lf_t"],
                        lw["og"][1])
    o_moba = _moba_sample(page_table, layer, r["mq"], r["mkv"], pools["moba_kv_t"], lw["og"][2])
    o_nsa = _nsa_sample(page_table, layer, r, pools["nsa_win_t"], pools["nsa_kv_t"], lw["nsa"], lw["og"][3])
    x1, hf = _out_proj(x, (o_lru, o_fox, o_moba, o_nsa), lw["w_out"], gt1, sc2, sh2, lw["g2"], 1, tm)
    y = _mlp(x1, hf, lw["w_up"], lw["w_down"], gt2, 1, tm, 1024)
    real = lambda a, tail: a.reshape((n_seq, TS) + tail)[:, :t_real]
    win_new = real(r["nwin"], (2, HEAD_DIM))
    win_prev = pools["nsa_win"][layer]
    keep = min(WINDOW, win_prev.shape[1] + t_real)
    win_all = jnp.concatenate([win_prev, win_new], axis=1)
    state = (real(r["fkv"], (2, GROUP_HEADS, HEAD_DIM)), real(r["small"][:, :GROUP_HEADS], (GROUP_HEADS,)),
             real(r["mkv"], (2, GROUP_HEADS, HEAD_DIM)), real(r["nkv"], (4, HEAD_DIM)),
             win_all[:, win_all.shape[1] - keep:], h_t, jnp.transpose(conv_t, (1, 0, 2)))
    return y, state


def _sample_trunk(x_sample, mods, lws, caches, page_table):
    n_seq, t_real, _ = x_sample.shape
    assert t_real <= TS
    depth, n_phys, page = caches["fox_kv"].shape[:3]
    past = page_table.shape[1] * page
    rows = n_seq * TS
    x = jnp.pad(x_sample, ((0, 0), (0, TS - t_real), (0, 0))).reshape(rows, D_MODEL)
    tm = min(512, rows)
    tabs = _rope_tables(past + (jnp.arange(tm, dtype=jnp.int32) % TS))
    pools = dict(
        fox_kv_t=jnp.transpose(caches["fox_kv"], (0, 1, 3, 4, 5, 2)).reshape(depth, n_phys, 2 * GROUP_W, page),
        fox_lf_t=jnp.transpose(caches["fox_logf"], (0, 1, 3, 2)),
        moba_kv_t=jnp.transpose(caches["moba_kv"], (0, 1, 3, 4, 5, 2)).reshape(depth, n_phys, 2 * GROUP_W, page),
        nsa_kv_t=jnp.transpose(caches["nsa_kv"], (0, 1, 3, 4, 2)).reshape(depth, n_phys, GROUP_W, page),
        nsa_win_t=jnp.transpose(caches["nsa_win"], (0, 1, 3, 4, 2)).reshape(depth, n_seq, 2 * HEAD_DIM, -1),
        nsa_win=caches["nsa_win"], lru_h=caches["lru_h"], lru_conv=caches["lru_conv"])
    states = []
    for l, lw in enumerate(lws):
        mod_rows = jnp.repeat(mods[l], TS, axis=0)
        x, st = _sample_layer(x, mod_rows, lw, l, pools, page_table, n_seq, t_real, tabs)
        states.append(st)
    stacked = [jnp.stack([s[i] for s in states], axis=0) for i in range(len(states[0]))]
    return x.reshape(n_seq, TS, D_MODEL)[:, :t_real], stacked


def kernel(x_prompt, x_sample, cache_fox_kv, cache_fox_logf, cache_moba_kv, cache_nsa_kv, state_nsa_win,
           state_lru_h, state_lru_conv, page_table, c_prompt, c_sample, norm_g, w_ada, b_ada, w_in,
           lru_conv_w, lru_conv_b, lru_wa, lru_ba, lru_wx, lru_bx, lru_lambda, fox_bf, fox_qk_g, moba_qk_g,
           nsa_qk_g, nsa_cmp_pos, nsa_cmp_w, out_g, w_out, w_up, w_down):
    p = dict(norm_g=norm_g, w_in=w_in, lru_conv_w=lru_conv_w, lru_conv_b=lru_conv_b, lru_wa=lru_wa,
             lru_ba=lru_ba, lru_wx=lru_wx, lru_bx=lru_bx, lru_lambda=lru_lambda, fox_bf=fox_bf,
             fox_qk_g=fox_qk_g, moba_qk_g=moba_qk_g, nsa_qk_g=nsa_qk_g, nsa_cmp_pos=nsa_cmp_pos,
             nsa_cmp_w=nsa_cmp_w, out_g=out_g, w_out=w_out, w_up=w_up, w_down=w_down)
    depth = w_in.shape[0]
    batch = x_prompt.shape[0]
    dec_batch = x_sample.shape[0]
    n_c = batch + dec_batch
    n_c_pad = -(-n_c // 8) * 8
    c_all = jnp.concatenate([c_prompt, c_sample, jnp.zeros((n_c_pad - n_c, D_MODEL), F32)], axis=0)
    mods = _ada(c_all, w_ada, b_ada)
    lws = [_layer_weights(p, l) for l in range(depth)]
    y_prompt, st_p = _prompt_trunk(x_prompt, [mods[l, :batch] for l in range(depth)], lws)
    caches = dict(fox_kv=cache_fox_kv, fox_logf=cache_fox_logf, moba_kv=cache_moba_kv, nsa_kv=cache_nsa_kv,
                  nsa_win=state_nsa_win, lru_h=state_lru_h, lru_conv=state_lru_conv)
    y_sample, st_s = _sample_trunk(x_sample, [mods[l, batch:n_c] for l in range(depth)], lws, caches, page_table)
    return (y_prompt, y_sample) + tuple(st_p) + tuple(st_s)
```

```python
import functools

import numpy as np
import jax
import jax.numpy as jnp
from jax import lax
from jax.experimental import pallas as pl
from jax.experimental.pallas import tpu as pltpu

F32 = jnp.float32
BF16 = jnp.bfloat16

D_MODEL = 1024
N_MIXERS = 4
GROUP_W = 256
HEAD_DIM = 64
GROUP_HEADS = 4
D_FF = 4 * D_MODEL
CONV_W = 4
LRU_C = 8.0
ROPE_THETA = 500000.0
ROPE_DIM = HEAD_DIM // 4
ROPE_HALF = ROPE_DIM // 2
MOBA_BLOCK = 256
MOBA_TOPK = 3
CMP_LEN = 32
CMP_STRIDE = 16
SEL_BLOCK = 64
SEL_TOPK = 16
WINDOW = 512
EPS = 1e-6
LOG2E = float(np.log2(np.e))
QSCALE = HEAD_DIM ** -0.5 * LOG2E
LANES = 128
IN_W = 22 * LANES
NEG_INF = float("-inf")
VMEM_LIMIT = 56 * 1024 * 1024

TS = 8
TQ = LANES
TK = MOBA_BLOCK
VH = HEAD_DIM + 16


def _cparams(sem):
    return pltpu.CompilerParams(dimension_semantics=sem, vmem_limit_bytes=VMEM_LIMIT)


def _dot(a, b):
    return jnp.dot(a, b, preferred_element_type=F32)


def _dot_nt(a, b):
    return lax.dot_general(a, b, (((1,), (1,)), ((), ())), preferred_element_type=F32)


def _split2(x):
    hi = x.astype(BF16)
    lo = (x - hi.astype(F32)).astype(BF16)
    return hi, lo


def _split3(x):
    hi = x.astype(BF16)
    r = x - hi.astype(F32)
    mid = r.astype(BF16)
    lo = (r - mid.astype(F32)).astype(BF16)
    return hi, mid, lo


def _dot_x3(x, m):
    hi, mid, lo = _split3(x)
    return _dot(hi, m) + _dot(mid, m) + _dot(lo, m)


def _dot_3x(m, x):
    hi, mid, lo = _split3(x)
    return _dot(m, hi) + _dot(m, mid) + _dot(m, lo)


def _log_sigmoid(x):
    return jnp.minimum(x, 0.0) - jnp.log1p(jnp.exp(-jnp.abs(x)))


def _softplus(x):
    return jnp.maximum(x, 0.0) + jnp.log1p(jnp.exp(-jnp.abs(x)))


def _gelu_tanh(x):
    return 0.5 * x * (1.0 + jnp.tanh(np.sqrt(2.0 / np.pi).astype(np.float32) * (x + 0.044715 * (x * x * x))))


def _rms_rows(x, g):
    return x * lax.rsqrt(jnp.mean(x * x, axis=-1, keepdims=True) + EPS) * g


def _head_lane(width=GROUP_W):
    return lax.broadcasted_iota(jnp.int32, (1, width), 1) // HEAD_DIM


def _stack_heads(q):
    hl = _head_lane()
    zero = jnp.zeros_like(q)
    return jnp.concatenate([jnp.where(hl == h, q, zero) for h in range(GROUP_HEADS)], axis=0)


def _unstack_heads(o, t):
    hl = _head_lane()
    out = jnp.zeros((t, GROUP_W), o.dtype)
    for h in range(GROUP_HEADS):
        out = out + jnp.where(hl == h, o[h * t:(h + 1) * t], 0.0)
    return out


def _rep4(a):
    return jnp.concatenate([a] * GROUP_HEADS, axis=0)


def _lanes4(a):
    return jnp.concatenate([a] * GROUP_HEADS, axis=1)


def _masked_softmax(l, mask, axis=-1):
    if mask is not None:
        l = jnp.where(mask, l, NEG_INF)
    m = jnp.max(l, axis=axis, keepdims=True)
    m = jnp.where(m > NEG_INF, m, 0.0)
    p = jnp.exp2(l - m)
    return p * (1.0 / jnp.maximum(jnp.sum(p, axis=axis, keepdims=True), 1e-30))


def _top_select(score, n_pick, axis):
    width = score.shape[axis]
    shape = [1, 1]
    shape[axis] = width
    j = lax.broadcasted_iota(jnp.int32, tuple(shape), axis)
    sel = jnp.zeros(score.shape, F32)
    work = score
    for _ in range(n_pick):
        mx = jnp.max(work, axis=axis, keepdims=True)
        cand = (work == mx) & (mx > NEG_INF)
        idx = jnp.min(jnp.where(cand, j, width), axis=axis, keepdims=True)
        pick = j == idx
        sel = jnp.where(pick, 1.0, sel)
        work = jnp.where(pick, NEG_INF, work)
    return sel


def _stacked_q_t(q):
    qt = q.astype(F32).T
    row_head = lax.broadcasted_iota(jnp.int32, (GROUP_W, 1), 0) // HEAD_DIM
    return jnp.concatenate([jnp.where(row_head == h, qt, 0.0) for h in range(GROUP_HEADS)], axis=1).astype(BF16)


def _flash_init(m_s, acc_s):
    m_s[...] = jnp.full_like(m_s, NEG_INF)
    acc_s[...] = jnp.zeros_like(acc_s)


def _flash_step(parts, m_s, acc_s, row_bias=None):
    shared = acc_s.shape[0] == VH
    m_old = m_s[...]
    m_new = m_old
    for s, _ in parts:
        tile_max = jnp.max(s, axis=0, keepdims=True)
        m_new = jnp.maximum(m_new, tile_max if row_bias is None else tile_max + row_bias)
    m_safe = jnp.where(m_new > NEG_INF, m_new, 0.0)
    alpha = jnp.exp2(m_old - m_safe)
    shift = m_safe if row_bias is None else m_safe - row_bias
    ps = [jnp.exp2(s - shift).astype(BF16) for s, _ in parts]
    if shared:
        acc = alpha * acc_s[...]
        for p, (_, vt) in zip(ps, parts):
            acc = acc + _dot(vt, p)
        acc_s[...] = acc
    else:
        for h in range(GROUP_HEADS):
            rows = slice(h * VH, (h + 1) * VH)
            lanes = slice(h * TQ, (h + 1) * TQ)
            acc = alpha[:, lanes] * acc_s[rows, :]
            for p, (_, vt) in zip(ps, parts):
                acc = acc + _dot(vt[rows, :], p[:, lanes])
            acc_s[rows, :] = acc
    m_s[...] = m_new


def _flash_pipelined(n_tiles, n_plain, logits_fn, vt_fn, s_buf, m_s, acc_s, row_bias=None):
    n_pairs = (n_tiles + 1) // 2

    def produce(pair, slot, causal):
        s_buf[slot, 0] = logits_fn(2 * pair, causal)
        s_buf[slot, 1] = logits_fn(2 * pair + 1, causal)

    def half_step(pair, src, dst):
        def run(causal):
            produce(pair + 1, dst, causal)
            _flash_step([(s_buf[src, 0], vt_fn(2 * pair)), (s_buf[src, 1], vt_fn(2 * pair + 1))],
                        m_s, acc_s, row_bias)

        plain = 2 * pair + 3 < n_plain
        pl.when(plain)(lambda: run(False))
        pl.when(jnp.logical_not(plain))(lambda: run(True))

    produce(0, 0, True)

    def body(j, carry):
        half_step(2 * j, 0, 1)
        pl.when(2 * j + 1 < n_pairs)(lambda: half_step(2 * j + 1, 1, 0))
        return carry

    lax.fori_loop(0, (n_pairs + 1) // 2, body, 0)


def _unstack_shared(a):
    return jnp.concatenate([a[:, h * TQ:(h + 1) * TQ] for h in range(GROUP_HEADS)], axis=0).T


def _flash_result(acc_s):
    parts = []
    for h in range(GROUP_HEADS):
        if acc_s.shape[0] == VH:
            blk = acc_s[:, h * TQ:(h + 1) * TQ]
        else:
            blk = acc_s[h * VH:(h + 1) * VH, :]
        parts.append(blk[0:HEAD_DIM, :] * (1.0 / jnp.maximum(blk[HEAD_DIM:HEAD_DIM + 1, :], 1e-30)))
    return jnp.concatenate(parts, axis=0).T


def _flash_scratch(shared):
    acc = (VH, GROUP_HEADS * TQ) if shared else (GROUP_HEADS * VH, TQ)
    return [pltpu.VMEM((1, GROUP_HEADS * TQ), F32), pltpu.VMEM(acc, F32),
            pltpu.VMEM((2, 2, TK, GROUP_HEADS * TQ), F32)]


def _resident(shape, index_map):
    return pl.BlockSpec(shape, index_map, pipeline_mode=pl.Buffered(1))


def _key_pos(k0, n):
    return k0 + lax.broadcasted_iota(jnp.int32, (n, 1), 0)


def _query_pos(q0):
    return q0 + lax.broadcasted_iota(jnp.int32, (1, TQ), 1)


def _ada_kernel(c_ref, w_ref, b_ref, o_ref):
    c = c_ref[...]
    a = (c * jax.nn.sigmoid(c)).astype(BF16)
    o_ref[0] = _dot(a, w_ref[0].astype(BF16)) + b_ref[0]


def _ada(c_all, w_ada, b_ada):
    depth, _, n = w_ada.shape
    rows = c_all.shape[0]
    tn = 1024
    return pl.pallas_call(
        _ada_kernel,
        grid=(depth, n // tn),
        in_specs=[pl.BlockSpec((rows, D_MODEL), lambda l, j: (0, 0)),
                  pl.BlockSpec((1, D_MODEL, tn), lambda l, j: (l, 0, j)),
                  pl.BlockSpec((1, 1, tn), lambda l, j: (l, 0, j))],
        out_specs=pl.BlockSpec((1, rows, tn), lambda l, j: (l, 0, j)),
        out_shape=jax.ShapeDtypeStruct((depth, rows, n), F32),
        compiler_params=_cparams(("arbitrary", "arbitrary")),
        name="ada_mod",
    )(c_all, w_ada, b_ada.reshape(depth, 1, n))


def _in_perm():
    gw, hd = GROUP_W, HEAD_DIM
    nkv0 = 9 * gw + GROUP_HEADS
    cols = list(range(0, 5 * gw))
    cols += list(range(5 * gw + 4, 8 * gw + 4))
    cols += list(range(8 * gw + 4, 9 * gw + 4))
    part = lambda k: list(range(nkv0 + k * hd, nkv0 + (k + 1) * hd))
    cols += part(0) + part(1)
    cols += part(2) + part(3)
    cols += part(4) + part(5)
    cols += list(range(5 * gw, 5 * gw + 4))
    cols += list(range(nkv0 + 6 * hd, nkv0 + 6 * hd + 12))
    return np.asarray(cols, np.int32)


C_LRU, C_FQ, C_FK, C_FV, C_MQ, C_MK, C_MV, C_NQ, C_KCVC, C_SEL, C_WIN, C_SMALL = (
    0, 512, 768, 1024, 1280, 1536, 1792, 2048, 2304, 2432, 2560, 2688)


def _block_diag_mean():
    m = np.zeros((GROUP_W, GROUP_W), np.float32)
    for h in range(GROUP_HEADS):
        m[h * HEAD_DIM:(h + 1) * HEAD_DIM, h * HEAD_DIM:(h + 1) * HEAD_DIM] = 1.0 / HEAD_DIM
    return m


def _rep_matrix():
    m = np.zeros((LANES, 2 * GROUP_W), np.float32)
    for j in range(HEAD_DIM):
        for k in range(GROUP_HEADS):
            m[j, k * HEAD_DIM + j] = 1.0
            m[HEAD_DIM + j, GROUP_W + k * HEAD_DIM + j] = 1.0
    return m


def _head_expand_matrix():
    m = np.zeros((LANES, GROUP_HEADS * LANES), np.float32)
    for h in range(GROUP_HEADS):
        m[h, h * LANES:(h + 1) * LANES] = 1.0
    return m


def _rope_tables(pos):
    inv = jnp.power(ROPE_THETA, -jnp.arange(ROPE_HALF, dtype=F32) / ROPE_HALF)
    ang = pos.astype(F32)[:, None] * inv[None, :]
    cos, sin = jnp.cos(ang), jnp.sin(ang)
    n = pos.shape[0]
    ones = jnp.ones((n, HEAD_DIM - ROPE_DIM), F32)
    zeros = jnp.zeros((n, HEAD_DIM - ROPE_DIM), F32)
    z8 = jnp.zeros((n, ROPE_HALF), F32)
    c = jnp.concatenate([cos, cos, ones], axis=1)
    sa = jnp.concatenate([z8, sin, zeros], axis=1)
    sb = jnp.concatenate([-sin, z8, zeros], axis=1)
    two = lambda a: jnp.concatenate([a, a], axis=1)
    return two(c), two(sa), two(sb)


_IN_COMMON = ("zl", "fq", "fkv", "small", "mq", "mkv", "nqn", "nqr", "nkv", "nwin")
_IN_PROMPT = ("fk16", "fvt", "fcrep", "mk16", "mvt", "kmean", "ks16", "vst", "kw16", "vwt", "kcvc")


def _in_proj_kernel(*refs, tiles_per_seq, prompt, tm):
    n_in = 14 if prompt else 12
    (x_ref, sc_ref, sh_ref, g_ref, w_ref, tc_ref, ta_ref, tb_ref, gains_ref, bf_ref, bd_ref,
     rep_ref) = refs[:12]
    names = _IN_COMMON + (_IN_PROMPT if prompt else ())
    out = dict(zip(names, refs[n_in:n_in + len(names)]))

    x = x_ref[...]
    xn = x * lax.rsqrt(jnp.mean(x * x, axis=-1, keepdims=True) + EPS) * g_ref[...]
    hm = (xn * (1.0 + sc_ref[0]) + sh_ref[0]).astype(BF16)

    def proj(c0, width):
        return _dot(hm, w_ref[:, c0:c0 + width])

    bd = bd_ref[...]
    tc, ta, tb = tc_ref[...], ta_ref[...], tb_ref[...]

    def hnorm(z, row):
        hi, lo = _split2(z * z)
        ms = _dot(hi, bd) + _dot(lo, bd)
        return z * lax.rsqrt(ms + EPS) * gains_ref[row:row + 1, :]

    def rope128(y):
        return y * tc + pltpu.roll(y, 8, axis=1) * ta + pltpu.roll(y, LANES - 8, axis=1) * tb

    def rope(y):
        return jnp.concatenate([rope128(y[:, :LANES]), rope128(y[:, LANES:])], axis=1)

    def store_vt(name, at, n_heads, tile, ones):
        step = VH if ones else HEAD_DIM
        for j in range(tm // tile):
            for h in range(n_heads):
                blk = at[h * HEAD_DIM:(h + 1) * HEAD_DIM, j * tile:(j + 1) * tile].astype(BF16)
                out[name][j, h * step:h * step + HEAD_DIM, :] = blk
                if ones:
                    out[name][j, h * step + HEAD_DIM:(h + 1) * step, :] = jnp.ones((VH - HEAD_DIM, tile), BF16)

    out["zl"][...] = proj(C_LRU, 512)

    out["fq"][...] = (hnorm(proj(C_FQ, 256), 0) * QSCALE).astype(out["fq"].dtype)
    fk = hnorm(proj(C_FK, 256), 1)
    fv = proj(C_FV, 256)
    out["fkv"][...] = jnp.concatenate([fk, fv], axis=1)

    out["mq"][...] = rope(hnorm(proj(C_MQ, 256), 2)) * QSCALE
    mk = rope(hnorm(proj(C_MK, 256), 3))
    mv = proj(C_MV, 256)
    out["mkv"][...] = jnp.concatenate([mk, mv], axis=1)

    nqn = hnorm(proj(C_NQ, 256), 4)
    out["nqn"][...] = (nqn * QSCALE).astype(out["nqn"].dtype)
    out["nqr"][...] = (rope(nqn) * QSCALE).astype(out["nqr"].dtype)

    kcvc = proj(C_KCVC, LANES)
    sw = proj(C_SEL, 2 * LANES)
    first = (lax.broadcasted_iota(jnp.int32, (1, 2 * LANES), 1) % LANES) < HEAD_DIM
    swn = jnp.where(first, rope(hnorm(sw, 5)), sw)
    sel_rows = swn[:, :LANES]
    win_rows = swn[:, LANES:]
    out["nkv"][...] = jnp.concatenate([kcvc, sel_rows], axis=1)
    out["nwin"][...] = win_rows

    zs = proj(C_SMALL, LANES)
    lane = lax.broadcasted_iota(jnp.int32, (1, LANES), 1)
    logf = _log_sigmoid(zs + bf_ref[...])
    gate = jax.nn.sigmoid(zs)
    out["small"][...] = jnp.where(lane < GROUP_HEADS, logf, jnp.where(lane < 16, gate, 0.0))

    if prompt:
        tri_ref, fexp_ref = refs[12:14]
        carry_ref = refs[n_in + len(names)]
        out["fk16"][...] = fk.astype(BF16)
        store_vt("fvt", fv.T, GROUP_HEADS, TK, True)
        out["mk16"][...] = mk.astype(BF16)
        store_vt("mvt", mv.T, GROUP_HEADS, TK, True)
        out["kcvc"][...] = kcvc
        rep = rep_ref[...]
        out["ks16"][...] = _dot(sel_rows.astype(BF16), rep)[:, :GROUP_W].astype(BF16)
        out["kw16"][...] = _dot(win_rows.astype(BF16), rep)[:, :GROUP_W].astype(BF16)
        store_vt("vst", sel_rows.T[HEAD_DIM:, :], 1, TK, True)
        store_vt("vwt", win_rows.T[HEAD_DIM:, :], 1, TQ, False)

        i = pl.program_id(0)

        @pl.when(i % tiles_per_seq == 0)
        def _():
            carry_ref[...] = jnp.zeros_like(carry_ref)

        lf = jnp.where(lane < GROUP_HEADS, logf, 0.0)
        fc = _dot_3x(tri_ref[...], lf) + carry_ref[...]
        carry_ref[...] = fc[tm - 1:tm, :]
        out["fcrep"][...] = _dot_x3(fc, fexp_ref[...]) * LOG2E
        nblk = tm // MOBA_BLOCK
        out["kmean"][0] = jnp.sum(mk.reshape(nblk, MOBA_BLOCK, GROUP_W), axis=1) * (1.0 / MOBA_BLOCK)


def _in_proj(x, sc, sh, g, w, tabs, gains, bf_row, *, tiles_per_seq, prompt, tm):
    rows = x.shape[0]
    n_tiles = rows // tm
    tb = sc.shape[1]
    n_tab_tiles = tabs[0].shape[0] // tm
    mod_map = (lambda i: (i // tiles_per_seq, 0, 0)) if tb == 1 else (lambda i: (i, 0, 0))
    tab_map = lambda i: (i % n_tab_tiles, 0)
    const = lambda i: (0, 0)
    row_map = lambda i: (i, 0)
    bd = jnp.asarray(_block_diag_mean(), BF16)
    rep = jnp.asarray(_rep_matrix(), BF16)
    in_specs = [pl.BlockSpec((tm, D_MODEL), row_map),
                pl.BlockSpec((1, tb, D_MODEL), mod_map),
                pl.BlockSpec((1, tb, D_MODEL), mod_map),
                pl.BlockSpec((1, D_MODEL), const),
                pl.BlockSpec((D_MODEL, IN_W), const),
                pl.BlockSpec((tm, LANES), tab_map),
                pl.BlockSpec((tm, LANES), tab_map),
                pl.BlockSpec((tm, LANES), tab_map),
                pl.BlockSpec((8, GROUP_W), const),
                pl.BlockSpec((1, LANES), const),
                pl.BlockSpec((GROUP_W, GROUP_W), const),
                pl.BlockSpec((LANES, 2 * GROUP_W), const)]
    args = [x, sc, sh, g, w, tabs[0], tabs[1], tabs[2], gains, bf_row, bd, rep]
    if prompt:
        tri = jnp.asarray(np.tril(np.ones((tm, tm), np.float32)), BF16)
        fexp = jnp.asarray(_head_expand_matrix(), BF16)
        in_specs += [pl.BlockSpec((tm, tm), const), pl.BlockSpec((LANES, GROUP_HEADS * LANES), const)]
        args += [tri, fexp]

    def o(width, dtype):
        return jax.ShapeDtypeStruct((rows, width), dtype), pl.BlockSpec((tm, width), row_map)

    def ot(tile, n_rows=GROUP_W):
        return (jax.ShapeDtypeStruct((rows // tile, n_rows, tile), BF16),
                pl.BlockSpec((tm // tile, n_rows, tile), lambda i: (i, 0, 0)))

    q_dtype = BF16 if prompt else F32
    outs = dict(zl=o(512, F32), fq=o(256, q_dtype), fkv=o(512, F32), small=o(LANES, F32), mq=o(256, F32),
                mkv=o(512, F32), nqn=o(256, q_dtype), nqr=o(256, q_dtype), nkv=o(256, F32), nwin=o(LANES, F32))
    scratch = []
    if prompt:
        nblk = tm // MOBA_BLOCK
        outs.update(fk16=o(256, BF16), fvt=ot(TK, GROUP_HEADS * VH), fcrep=o(GROUP_HEADS * LANES, F32),
                    mk16=o(256, BF16), mvt=ot(TK, GROUP_HEADS * VH),
                    kmean=(jax.ShapeDtypeStruct((n_tiles, nblk, GROUP_W), F32),
                           pl.BlockSpec((1, nblk, GROUP_W), lambda i: (i, 0, 0))),
                    ks16=o(256, BF16), vst=ot(TK, VH), kw16=o(256, BF16), vwt=ot(TQ, HEAD_DIM),
                    kcvc=o(LANES, F32))
        scratch.append(pltpu.VMEM((1, LANES), F32))
    names = _IN_COMMON + (_IN_PROMPT if prompt else ())
    res = pl.pallas_call(
        functools.partial(_in_proj_kernel, tiles_per_seq=tiles_per_seq, prompt=prompt, tm=tm),
        grid=(n_tiles,),
        in_specs=in_specs,
        out_specs=[outs[n][1] for n in names],
        out_shape=[outs[n][0] for n in names],
        scratch_shapes=scratch,
        compiler_params=_cparams(("arbitrary",)),
        name="in_proj_prompt" if prompt else "in_proj_sample",
    )(*args)
    return dict(zip(names, res))


def _lru_coeffs(u, wa_ref, wx_ref, ba_ref, bx_ref, lam_ref):
    u16 = u.astype(BF16)
    r = jax.nn.sigmoid(_dot(u16, wa_ref[...]) + ba_ref[...])
    i = jax.nn.sigmoid(_dot(u16, wx_ref[...]) + bx_ref[...])
    log_a = -LRU_C * r * _softplus(-lam_ref[...])
    a = jnp.exp(log_a)
    b = jnp.sqrt(-jnp.tanh(log_a) * (a * a + 1.0)) * i * u
    return a, b


def _lru_prompt_kernel(zl_ref, cw_ref, cb_ref, wa_ref, wx_ref, ba_ref, bx_ref, lam_ref, og_ref,
                       o_ref, ht_ref, cnew_ref, xp_s, a_s, b_s, hs_s, h_s, *, tm):
    t = pl.program_id(1)

    @pl.when(t == 0)
    def _():
        h_s[...] = jnp.zeros_like(h_s)
        xp_s[0:8, :] = jnp.zeros((8, GROUP_W), F32)

    xa = zl_ref[:, 0:GROUP_W]
    ga = zl_ref[:, GROUP_W:2 * GROUP_W]
    xp_s[8:8 + tm, :] = xa
    u = (cb_ref[...] + cw_ref[0:1, :] * xp_s[5:5 + tm, :] + cw_ref[1:2, :] * xp_s[6:6 + tm, :]
         + cw_ref[2:3, :] * xp_s[7:7 + tm, :] + cw_ref[3:4, :] * xa)
    tail = xp_s[tm:tm + 8, :]
    xp_s[0:8, :] = tail
    cnew_ref[0] = tail
    a, b = _lru_coeffs(u, wa_ref, wx_ref, ba_ref, bx_ref, lam_ref)
    a_s[...] = a
    b_s[...] = b

    def body(j, h):
        r0 = pl.multiple_of(j * 8, 8)
        a8 = a_s[pl.ds(r0, 8), :]
        b8 = b_s[pl.ds(r0, 8), :]
        rows = []
        for k in range(8):
            h = a8[k:k + 1, :] * h + b8[k:k + 1, :]
            rows.append(h)
        hs_s[pl.ds(r0, 8), :] = jnp.concatenate(rows, axis=0)
        return h

    h = lax.fori_loop(0, tm // 8, body, h_s[...])
    h_s[...] = h
    ht_ref[0] = h
    o = hs_s[...] * _gelu_tanh(ga)
    o_ref[...] = _rms_rows(o, og_ref[...]).astype(BF16)


def _lru_prompt(zl, lw, batch, seq, tm):
    nt = seq // tm
    const = lambda b, t: (0, 0)
    vec = pl.BlockSpec((1, GROUP_W), const)
    mat = pl.BlockSpec((GROUP_W, GROUP_W), const)
    return pl.pallas_call(
        functools.partial(_lru_prompt_kernel, tm=tm),
        grid=(batch, nt),
        in_specs=[pl.BlockSpec((tm, 2 * GROUP_W), lambda b, t: (b * nt + t, 0)),
                  pl.BlockSpec((CONV_W, GROUP_W), const), vec, mat, mat, vec, vec, vec, vec],
        out_specs=[pl.BlockSpec((tm, GROUP_W), lambda b, t: (b * nt + t, 0)),
                   pl.BlockSpec((1, 1, GROUP_W), lambda b, t: (b, 0, 0)),
                   pl.BlockSpec((1, 8, GROUP_W), lambda b, t: (b, 0, 0))],
        out_shape=[jax.ShapeDtypeStruct((batch * seq, GROUP_W), BF16),
                   jax.ShapeDtypeStruct((batch, 1, GROUP_W), F32),
                   jax.ShapeDtypeStruct((batch, 8, GROUP_W), F32)],
        scratch_shapes=[pltpu.VMEM((tm + 8, GROUP_W), F32), pltpu.VMEM((tm, GROUP_W), F32),
                        pltpu.VMEM((tm, GROUP_W), F32), pltpu.VMEM((tm, GROUP_W), F32),
                        pltpu.VMEM((1, GROUP_W), F32)],
        compiler_params=_cparams(("arbitrary", "arbitrary")),
        name="lru_prompt",
    )(zl, lw["cw"], lw["cb"], lw["wa"], lw["wx"], lw["ba"], lw["bx"], lw["lam"], lw["og"])


def _fox_prompt_kernel(q_ref, fcq_ref, k_ref, vt_ref, fck_ref, og_ref, o_ref, m_s, acc_s, s_buf):
    qi = pl.program_id(1)
    q0 = qi * TQ
    qst = _stacked_q_t(q_ref[...])
    fcq = fcq_ref[...]
    fq_row = jnp.concatenate([fcq[:, h * LANES:(h + 1) * LANES].T[0:1, :] for h in range(GROUP_HEADS)], axis=1)
    qpos = _lanes4(_query_pos(q0))
    _flash_init(m_s, acc_s)
    last = vt_ref.shape[0] - 1

    def logits(t, causal):
        k0 = pl.multiple_of(jnp.minimum(t, last) * TK, TK)
        s = _dot(k_ref[pl.ds(k0, TK), :], qst) - fck_ref[pl.ds(k0, TK), :]
        return jnp.where(_key_pos(t * TK, TK) <= qpos, s, NEG_INF) if causal else s

    n_full = q0 // TK
    _flash_pipelined(n_full + 1, n_full, logits, lambda t: vt_ref[jnp.minimum(t, last)], s_buf, m_s, acc_s, fq_row)
    o_ref[...] = _rms_rows(_flash_result(acc_s), og_ref[...]).astype(BF16)


def _fox_prompt(r, og, batch, seq):
    nq = seq // TQ
    nk = seq // TK
    row = lambda b, i: (b * nq + i, 0)
    return pl.pallas_call(
        _fox_prompt_kernel,
        grid=(batch, nq),
        in_specs=[pl.BlockSpec((TQ, GROUP_W), row),
                  pl.BlockSpec((TQ, GROUP_HEADS * LANES), row),
                  _resident((seq, GROUP_W), lambda b, i: (b, 0)),
                  _resident((nk, GROUP_HEADS * VH, TK), lambda b, i: (b, 0, 0)),
                  _resident((seq, GROUP_HEADS * LANES), lambda b, i: (b, 0)),
                  pl.BlockSpec((1, GROUP_W), lambda b, i: (0, 0))],
        out_specs=pl.BlockSpec((TQ, GROUP_W), row),
        out_shape=jax.ShapeDtypeStruct((batch * seq, GROUP_W), BF16),
        scratch_shapes=_flash_scratch(False),
        compiler_params=_cparams(("arbitrary", "arbitrary")),
        name="fox_prompt",
    )(r["fq"], r["fcrep"], r["fk16"], r["fvt"], r["fcrep"], og)


def _moba_prompt_kernel(q_ref, kmean_ref, k_ref, vt_ref, og_ref, o_ref, m_s, acc_s, s_buf, sel_s, *, nb):
    qi = pl.program_id(1)
    q0 = qi * TQ
    cur = q0 // MOBA_BLOCK
    q32 = q_ref[...]
    qt = q32.T
    row_head = lax.broadcasted_iota(jnp.int32, (GROUP_W, 1), 0) // HEAD_DIM
    qst32 = jnp.concatenate([jnp.where(row_head == h, qt, 0.0) for h in range(GROUP_HEADS)], axis=1)
    qst = qst32.astype(BF16)
    qh, ql = _split2(qst32)
    kh, kl = _split2(kmean_ref[0])
    sc = _dot(kh, qh) + _dot(kh, ql) + _dot(kl, qh)
    n_col = lax.broadcasted_iota(jnp.int32, (nb, 1), 0)
    sel = _top_select(jnp.where(n_col < cur, sc, NEG_INF), min(MOBA_TOPK, nb), 0)
    sel_s[...] = jnp.where(n_col == cur, 1.0, sel)
    _flash_init(m_s, acc_s)
    qpos = _lanes4(_query_pos(q0))

    def logits(n, causal):
        nc = jnp.minimum(n, nb - 1)
        k0 = pl.multiple_of(nc * MOBA_BLOCK, MOBA_BLOCK)
        s = _dot(k_ref[pl.ds(k0, MOBA_BLOCK), :], qst)
        keep = sel_s[pl.ds(nc, 1), :] > 0.5
        if causal:
            keep = keep & (_key_pos(n * MOBA_BLOCK, MOBA_BLOCK) <= qpos)
        return jnp.where(keep, s, NEG_INF)

    _flash_pipelined(cur + 1, cur, logits, lambda n: vt_ref[jnp.minimum(n, nb - 1)], s_buf, m_s, acc_s)
    o_ref[...] = _rms_rows(_flash_result(acc_s), og_ref[...]).astype(BF16)


def _moba_prompt(r, og, batch, seq):
    assert TK == MOBA_BLOCK
    nq = seq // TQ
    nb = seq // MOBA_BLOCK
    kmean = r["kmean"].reshape(batch, nb, GROUP_W)
    row = lambda b, i: (b * nq + i, 0)
    return pl.pallas_call(
        functools.partial(_moba_prompt_kernel, nb=nb),
        grid=(batch, nq),
        in_specs=[pl.BlockSpec((TQ, GROUP_W), row),
                  pl.BlockSpec((1, nb, GROUP_W), lambda b, i: (b, 0, 0)),
                  _resident((seq, GROUP_W), lambda b, i: (b, 0)),
                  _resident((nb, GROUP_HEADS * VH, TK), lambda b, i: (b, 0, 0)),
                  pl.BlockSpec((1, GROUP_W), lambda b, i: (0, 0))],
        out_specs=pl.BlockSpec((TQ, GROUP_W), row),
        out_shape=jax.ShapeDtypeStruct((batch * seq, GROUP_W), BF16),
        scratch_shapes=_flash_scratch(False) + [pltpu.VMEM((nb, GROUP_HEADS * TQ), F32)],
        compiler_params=_cparams(("arbitrary", "arbitrary")),
        name="moba_prompt",
    )(r["mq"], kmean, r["mk16"], r["mvt"], og)


def _cmp_weights(cw):
    w = cw.reshape(2, CMP_LEN, HEAD_DIM, HEAD_DIM)
    zero = jnp.zeros((CMP_STRIDE, HEAD_DIM, HEAD_DIM), cw.dtype)

    def half(j0):
        wk = w[0, j0:j0 + CMP_STRIDE]
        wv = w[1, j0:j0 + CMP_STRIDE]
        top = jnp.concatenate([wk, zero], axis=2)
        bot = jnp.concatenate([zero, wv], axis=2)
        return jnp.concatenate([top, bot], axis=1).reshape(CMP_STRIDE * LANES, LANES)

    return jnp.concatenate([half(0), half(CMP_STRIDE)], axis=1).astype(BF16)


def _cmp_pos(cpos):
    p = jnp.concatenate([cpos[0], cpos[1]], axis=1)
    rows = jnp.stack([p[:CMP_STRIDE].reshape(-1), p[CMP_STRIDE:].reshape(-1)])
    return jnp.concatenate([rows, jnp.zeros((6, rows.shape[1]), rows.dtype)], axis=0)


def _cmp_windows(x, pos_ref, w_ref, g_ref, n_valid):
    w = w_ref[...]
    y2 = _dot(x.astype(BF16), w)
    pw = _dot(pos_ref[...].astype(BF16), w)
    rows = y2.shape[0]
    y = (y2[:, :LANES] + pw[0:1, :LANES]) + pltpu.roll(y2[:, LANES:] + pw[1:2, LANES:], rows - 1, axis=0)
    lane = lax.broadcasted_iota(jnp.int32, (1, LANES), 1)
    is_k = lane < HEAD_DIM
    ms = jnp.sum(jnp.where(is_k, y * y, 0.0), axis=-1, keepdims=True) * (1.0 / HEAD_DIM)
    y = jnp.where(is_k, y * lax.rsqrt(ms + EPS) * g_ref[...], y)
    row = lax.broadcasted_iota(jnp.int32, (rows, 1), 0)
    return jnp.where(row < n_valid, y, 0.0)


def _cmp_prompt_kernel(x_ref, pos_ref, w_ref, g_ref, rep_ref, k_ref, vt_ref, *, n_valid):
    y = _cmp_windows(x_ref[...], pos_ref, w_ref, g_ref, n_valid)
    k_ref[...] = _dot(y.astype(BF16), rep_ref[...])[:, :GROUP_W].astype(BF16)
    vt_ref[0] = y.T[HEAD_DIM:, :].astype(BF16)


def _cmp_prompt(kcvc, nw, batch, seq):
    rows = seq // CMP_STRIDE
    width = CMP_STRIDE * LANES
    x2 = kcvc.reshape(batch * rows, width)
    n_valid = (seq - CMP_LEN) // CMP_STRIDE + 1
    const = lambda b: (0, 0)
    return pl.pallas_call(
        functools.partial(_cmp_prompt_kernel, n_valid=n_valid),
        grid=(batch,),
        in_specs=[pl.BlockSpec((rows, width), lambda b: (b, 0)),
                  pl.BlockSpec((8, width), const), pl.BlockSpec((width, 2 * LANES), const),
                  pl.BlockSpec((1, LANES), const), pl.BlockSpec((LANES, 2 * GROUP_W), const)],
        out_specs=[pl.BlockSpec((rows, GROUP_W), lambda b: (b, 0)),
                   pl.BlockSpec((1, HEAD_DIM, rows), lambda b: (b, 0, 0))],
        out_shape=[jax.ShapeDtypeStruct((batch * rows, GROUP_W), BF16),
                   jax.ShapeDtypeStruct((batch, HEAD_DIM, rows), BF16)],
        compiler_params=_cparams(("arbitrary",)),
        name="nsa_cmp_prompt",
    )(x2, nw["cpos"], nw["cw"], nw["gcmp"], nw["rep"])


def _overlap_matrix(n_rows, n_valid, ns, ns_pad):
    c = np.arange(n_rows)[:, None] * CMP_STRIDE
    b = np.arange(ns_pad)[None, :] * SEL_BLOCK
    m = (c < b + SEL_BLOCK) & (c + CMP_LEN > b)
    m &= (np.arange(n_rows)[:, None] < n_valid) & (np.arange(ns_pad)[None, :] < ns)
    return m.astype(np.float32)


def _gate_expand_matrix():
    m = np.zeros((LANES, 3 * GROUP_W), np.float32)
    for h in range(GROUP_HEADS):
        for g in range(3):
            m[4 + h * 3 + g, g * GROUP_W + h * HEAD_DIM:g * GROUP_W + (h + 1) * HEAD_DIM] = 1.0
    return m


def _nsa_gate_mix(o_c, o_s, o_w, small, gexp_ref, og_ref):
    gx = _dot_x3(small, gexp_ref[...])
    o = gx[:, 0:GROUP_W] * o_c + gx[:, GROUP_W:2 * GROUP_W] * o_s + gx[:, 2 * GROUP_W:3 * GROUP_W] * o_w
    return _rms_rows(o, og_ref[...])


def _nsa_prompt_kernel(qn_ref, qr_ref, small_ref, kc_ref, vct_ref, ovlt_ref, gexp_ref, ks_ref, vst_ref,
                       kw0_ref, kw1_ref, kw2_ref, kw3_ref, kw4_ref,
                       vw0_ref, vw1_ref, vw2_ref, vw3_ref, vw4_ref, og_ref, o_ref,
                       m_s, acc_s, s_buf, *, ns, n_sel):
    qi = pl.program_id(1)
    q0 = qi * TQ
    pos = _query_pos(q0)
    pos4 = _lanes4(pos)
    qnt = _stacked_q_t(qn_ref[...])
    qrt = _stacked_q_t(qr_ref[...])

    nc = kc_ref.shape[0]
    cend = lax.broadcasted_iota(jnp.int32, (nc, 1), 0) * CMP_STRIDE + (CMP_LEN - 1)
    pc = _masked_softmax(_dot(kc_ref[...], qnt), cend <= pos4, axis=0)
    o_c = _unstack_shared(_dot(vct_ref[0], pc.astype(BF16)))

    pcs = pc[:, 0:TQ] + pc[:, TQ:2 * TQ] + pc[:, 2 * TQ:3 * TQ] + pc[:, 3 * TQ:4 * TQ]
    imp = _dot_3x(ovlt_ref[...], pcs)
    ns_pad = imp.shape[0]
    j = lax.broadcasted_iota(jnp.int32, (ns_pad, 1), 0)
    cur = pos // SEL_BLOCK
    imp = jnp.where((j == 0) | (j == cur) | (j == cur - 1), jnp.inf, imp)
    imp = jnp.where((j * SEL_BLOCK > pos) | (j >= ns), NEG_INF, imp)
    sel16 = _top_select(imp, n_sel, 0).astype(BF16)

    _flash_init(m_s, acc_s)
    nk = (q0 + TQ + TK - 1) // TK
    blk_row = lax.broadcasted_iota(jnp.int32, (1, ns_pad), 1)
    key_blk = lax.broadcasted_iota(jnp.int32, (TK, 1), 0) // SEL_BLOCK

    last = vst_ref.shape[0] - 1

    def sel_logits(t, causal):
        tc = jnp.minimum(t, last)
        k0 = pl.multiple_of(tc * TK, TK)
        s = _dot(ks_ref[pl.ds(k0, TK), :], qrt)
        expand = (blk_row == tc * (TK // SEL_BLOCK) + key_blk).astype(BF16)
        mask = _dot(expand, sel16) > 0.5
        if causal:
            mask = mask & (_key_pos(t * TK, TK) <= pos)
        return s + _lanes4(jnp.where(mask, 0.0, NEG_INF))

    _flash_pipelined(nk, q0 // TK, sel_logits, lambda t: vst_ref[jnp.minimum(t, last)], s_buf, m_s, acc_s)
    o_s = _flash_result(acc_s)

    kws = (kw0_ref, kw1_ref, kw2_ref, kw3_ref, kw4_ref)
    vws = (vw0_ref, vw1_ref, vw2_ref, vw3_ref, vw4_ref)
    logits = []
    for t, kw in enumerate(kws):
        k0 = q0 + (t - 4) * TQ
        s = _dot(kw[...], qrt)
        if t == 0:
            s = jnp.where((_key_pos(k0, TQ) >= pos4 - WINDOW) & (k0 >= 0), s, NEG_INF)
        elif t == 4:
            s = jnp.where(_key_pos(k0, TQ) <= pos4, s, NEG_INF)
        else:
            s = s + jnp.where(k0 >= 0, 0.0, NEG_INF)
        logits.append(s)
    pw = _masked_softmax(jnp.concatenate(logits, axis=0), None, axis=0).astype(BF16)
    o_wt = jnp.zeros((HEAD_DIM, GROUP_HEADS * TQ), F32)
    for t, vw in enumerate(vws):
        o_wt = o_wt + _dot(vw[0], pw[t * TQ:(t + 1) * TQ, :])
    o_w = _unstack_shared(o_wt)

    o_ref[...] = _nsa_gate_mix(o_c, o_s, o_w, small_ref[...], gexp_ref, og_ref).astype(BF16)


def _nsa_prompt(r, kc_rep, vct_rep, og, batch, seq):
    assert WINDOW == 4 * TQ
    nq = seq // TQ
    nk = seq // TK
    nc = seq // CMP_STRIDE
    ns = seq // SEL_BLOCK
    ns_pad = -(-ns // LANES) * LANES
    n_valid = (seq - CMP_LEN) // CMP_STRIDE + 1
    ovlt = jnp.asarray(_overlap_matrix(nc, n_valid, ns, ns_pad).T, BF16)
    gexp = jnp.asarray(_gate_expand_matrix(), BF16)
    const = lambda b, i: (0, 0)
    row = lambda b, i: (b * nq + i, 0)
    kw_spec = lambda t: pl.BlockSpec((TQ, GROUP_W), lambda b, i: (b * nq + jnp.maximum(i + t - 4, 0), 0))
    vw_spec = lambda t: pl.BlockSpec((1, HEAD_DIM, TQ), lambda b, i: (b * nq + jnp.maximum(i + t - 4, 0), 0, 0))
    return pl.pallas_call(
        functools.partial(_nsa_prompt_kernel, ns=ns, n_sel=min(SEL_TOPK, ns)),
        grid=(batch, nq),
        in_specs=[pl.BlockSpec((TQ, GROUP_W), row), pl.BlockSpec((TQ, GROUP_W), row),
                  pl.BlockSpec((TQ, LANES), row),
                  pl.BlockSpec((nc, GROUP_W), lambda b, i: (b, 0)),
                  pl.BlockSpec((1, HEAD_DIM, nc), lambda b, i: (b, 0, 0)),
                  pl.BlockSpec((ns_pad, nc), const),
                  pl.BlockSpec((LANES, 3 * GROUP_W), const),
                  _resident((seq, GROUP_W), lambda b, i: (b, 0)),
                  _resident((nk, VH, TK), lambda b, i: (b, 0, 0))]
                 + [kw_spec(t) for t in range(5)] + [vw_spec(t) for t in range(5)]
                 + [pl.BlockSpec((1, GROUP_W), const)],
        out_specs=pl.BlockSpec((TQ, GROUP_W), row),
        out_shape=jax.ShapeDtypeStruct((batch * seq, GROUP_W), BF16),
        scratch_shapes=_flash_scratch(True),
        compiler_params=_cparams(("arbitrary", "arbitrary")),
        name="nsa_prompt",
    )(r["nqn"], r["nqr"], r["small"], kc_rep, vct_rep, ovlt, gexp, r["ks16"], r["vst"],
      *([r["kw16"]] * 5), *([r["vwt"]] * 5), og)


def _out_proj_kernel(x_ref, o0_ref, o1_ref, o2_ref, o3_ref, w_ref, gt_ref, sc_ref, sh_ref, g_ref,
                     x1_ref, hf_ref):
    o_cat = jnp.concatenate([o0_ref[...], o1_ref[...], o2_ref[...], o3_ref[...]], axis=1).astype(BF16)
    x1 = x_ref[...] + gt_ref[0] * _dot(o_cat, w_ref[...])
    x1_ref[...] = x1
    hf_ref[...] = (_rms_rows(x1, g_ref[...]) * (1.0 + sc_ref[0]) + sh_ref[0]).astype(BF16)


def _out_proj(x, outs, w_out, gt1, sc2, sh2, g2, tiles_per_seq, tm):
    rows = x.shape[0]
    tb = gt1.shape[1]
    mod_map = (lambda i: (i // tiles_per_seq, 0, 0)) if tb == 1 else (lambda i: (i, 0, 0))
    row = lambda i: (i, 0)
    const = lambda i: (0, 0)
    mod = pl.BlockSpec((1, tb, D_MODEL), mod_map)
    return pl.pallas_call(
        _out_proj_kernel,
        grid=(rows // tm,),
        in_specs=[pl.BlockSpec((tm, D_MODEL), row)] + [pl.BlockSpec((tm, GROUP_W), row)] * 4
                 + [pl.BlockSpec((D_MODEL, D_MODEL), const), mod, mod, mod, pl.BlockSpec((1, D_MODEL), const)],
        out_specs=[pl.BlockSpec((tm, D_MODEL), row), pl.BlockSpec((tm, D_MODEL), row)],
        out_shape=[jax.ShapeDtypeStruct((rows, D_MODEL), F32), jax.ShapeDtypeStruct((rows, D_MODEL), BF16)],
        compiler_params=_cparams(("arbitrary",)),
        name="out_proj",
    )(x, *outs, w_out, gt1, sc2, sh2, g2)


def _mlp_kernel(x1_ref, hf_ref, wu_ref, wd_ref, gt_ref, y_ref, acc_s):
    f = pl.program_id(1)

    @pl.when(f == 0)
    def _():
        acc_s[...] = jnp.zeros_like(acc_s)

    up = jnp.maximum(_dot(hf_ref[...], wu_ref[...]), 0.0)
    acc_s[...] += _dot((up * up).astype(BF16), wd_ref[...])

    @pl.when(f == pl.num_programs(1) - 1)
    def _():
        y_ref[...] = x1_ref[...] + gt_ref[0] * acc_s[...]


def _mlp(x1, hf, w_up, w_down, gt2, tiles_per_seq, tm, tf):
    rows = x1.shape[0]
    tb = gt2.shape[1]
    mod_map = (lambda i, f: (i // tiles_per_seq, 0, 0)) if tb == 1 else (lambda i, f: (i, 0, 0))
    row = lambda i, f: (i, 0)
    return pl.pallas_call(
        _mlp_kernel,
        grid=(rows // tm, D_FF // tf),
        in_specs=[pl.BlockSpec((tm, D_MODEL), row), pl.BlockSpec((tm, D_MODEL), row),
                  pl.BlockSpec((D_MODEL, tf), lambda i, f: (0, f)),
                  pl.BlockSpec((tf, D_MODEL), lambda i, f: (f, 0)),
                  pl.BlockSpec((1, tb, D_MODEL), mod_map)],
        out_specs=pl.BlockSpec((tm, D_MODEL), row),
        out_shape=jax.ShapeDtypeStruct((rows, D_MODEL), F32),
        scratch_shapes=[pltpu.VMEM((tm, D_MODEL), F32)],
        compiler_params=_cparams(("arbitrary", "arbitrary")),
        name="mlp",
    )(x1, hf, w_up, w_down, gt2)


def _block_diag(w):
    out = jnp.zeros((GROUP_W, GROUP_W), w.dtype)
    for h in range(GROUP_HEADS):
        out = out.at[h * HEAD_DIM:(h + 1) * HEAD_DIM, h * HEAD_DIM:(h + 1) * HEAD_DIM].set(w[h])
    return out


def _layer_weights(p, l):
    tile4 = lambda g: jnp.tile(g, GROUP_HEADS)
    ones = jnp.ones((HEAD_DIM,), F32)
    w_in = jnp.concatenate([p["w_in"][l][:, _in_perm()],
                            jnp.zeros((D_MODEL, IN_W - _in_perm().shape[0]), F32)], axis=1).astype(BF16)
    ng = p["nsa_qk_g"][l]
    gains = jnp.stack([tile4(p["fox_qk_g"][l, 0]), tile4(p["fox_qk_g"][l, 1]),
                       tile4(p["moba_qk_g"][l, 0]), tile4(p["moba_qk_g"][l, 1]),
                       tile4(ng[0]), jnp.concatenate([ng[2], ones, ng[3], ones]),
                       jnp.ones((GROUP_W,), F32), jnp.ones((GROUP_W,), F32)])
    bf_row = jnp.zeros((1, LANES), F32).at[0, :GROUP_HEADS].set(p["fox_bf"][l])
    row = lambda a: a.reshape(1, -1)
    lru = dict(cw=p["lru_conv_w"][l], cb=row(p["lru_conv_b"][l]),
               wa=_block_diag(p["lru_wa"][l]).astype(BF16), wx=_block_diag(p["lru_wx"][l]).astype(BF16),
               ba=row(p["lru_ba"][l]), bx=row(p["lru_bx"][l]), lam=row(p["lru_lambda"][l]),
               og=row(p["out_g"][l, 0]))
    gcmp = jnp.concatenate([ng[1], ones]).reshape(1, LANES)
    nsa = dict(cw=_cmp_weights(p["nsa_cmp_w"][l]), cpos=_cmp_pos(p["nsa_cmp_pos"][l]), gcmp=gcmp,
               rep=jnp.asarray(_rep_matrix(), BF16))
    return dict(w_in=w_in, gains=gains, bf_row=bf_row, lru=lru, nsa=nsa,
                g1=row(p["norm_g"][l, 0]), g2=row(p["norm_g"][l, 1]),
                og=[row(p["out_g"][l, k]) for k in range(N_MIXERS)],
                w_out=p["w_out"][l].astype(BF16), w_up=p["w_up"][l].astype(BF16),
                w_down=p["w_down"][l].astype(BF16))


def _prompt_layer(x, mod, lw, batch, seq, tabs):
    tm = min(512, seq)
    tps = seq // tm
    sh1, sc1, gt1, sh2, sc2, gt2 = [m[:, None, :] for m in jnp.split(mod, 6, axis=-1)]
    r = _in_proj(x, sc1, sh1, lw["g1"], lw["w_in"], tabs, lw["gains"], lw["bf_row"],
                 tiles_per_seq=tps, prompt=True, tm=tm)
    o_lru, h_t, conv_t = _lru_prompt(r["zl"], lw["lru"], batch, seq, tm)
    o_fox = _fox_prompt(r, lw["og"][1], batch, seq)
    o_moba = _moba_prompt(r, lw["og"][2], batch, seq)
    kc_rep, vct_rep = _cmp_prompt(r["kcvc"], lw["nsa"], batch, seq)
    o_nsa = _nsa_prompt(r, kc_rep, vct_rep, lw["og"][3], batch, seq)
    x1, hf = _out_proj(x, (o_lru, o_fox, o_moba, o_nsa), lw["w_out"], gt1, sc2, sh2, lw["g2"], tps, tm)
    y = _mlp(x1, hf, lw["w_up"], lw["w_down"], gt2, tps, tm, 1024)
    keep = min(WINDOW, seq)
    state = (r["fkv"].reshape(batch, seq, 2, GROUP_HEADS, HEAD_DIM),
             r["small"][:, :GROUP_HEADS].reshape(batch, seq, GROUP_HEADS),
             r["mkv"].reshape(batch, seq, 2, GROUP_HEADS, HEAD_DIM),
             r["nkv"].reshape(batch, seq, 4, HEAD_DIM),
             r["nwin"].reshape(batch, seq, 2, HEAD_DIM)[:, seq - keep:],
             h_t.reshape(batch, GROUP_W),
             conv_t[:, 8 - (CONV_W - 1):, :])
    return y, state


def _prompt_trunk(x_prompt, mods, lws):
    batch, seq, _ = x_prompt.shape
    assert seq % max(512, TK) == 0 or seq in (256,)
    tabs = _rope_tables(jnp.arange(seq, dtype=jnp.int32))
    x = x_prompt.reshape(batch * seq, D_MODEL)
    states = []
    for l, lw in enumerate(lws):
        x, st = _prompt_layer(x, mods[l], lw, batch, seq, tabs)
        states.append(st)
    stacked = [jnp.stack([s[i] for s in states], axis=0) for i in range(len(states[0]))]
    return x.reshape(batch, seq, D_MODEL), stacked


def _lru_sample_kernel(zl_ref, h0_ref, cbuf_ref, cw_ref, cb_ref, wa_ref, wx_ref, ba_ref, bx_ref, lam_ref,
                       og_ref, o_ref, ht_ref, cnew_ref, *, n_seq, t_real):
    o_ref[...] = jnp.zeros_like(o_ref)
    xs = [cbuf_ref[k] for k in range(CONV_W - 1)] + [zl_ref[t, :, 0:GROUP_W] for t in range(t_real)]
    h = h0_ref[...]
    for t in range(t_real):
        u = cb_ref[...]
        for k in range(CONV_W):
            u = u + xs[t + k] * cw_ref[k:k + 1, :]
        a, b = _lru_coeffs(u, wa_ref, wx_ref, ba_ref, bx_ref, lam_ref)
        h = a * h + b
        o = h * _gelu_tanh(zl_ref[t, :, GROUP_W:2 * GROUP_W])
        o_ref[t] = _rms_rows(o, og_ref[...])
    ht_ref[...] = h
    for k in range(CONV_W - 1):
        cnew_ref[k] = xs[t_real + k]


def _lru_sample(zl, h0, cbuf, lw, n_seq, t_real):
    assert t_real >= CONV_W - 1
    zl_t = jnp.transpose(zl.reshape(n_seq, TS, 2 * GROUP_W), (1, 0, 2))
    o_t, h_t, conv_t = pl.pallas_call(
        functools.partial(_lru_sample_kernel, n_seq=n_seq, t_real=t_real),
        out_shape=[jax.ShapeDtypeStruct((TS, n_seq, GROUP_W), F32),
                   jax.ShapeDtypeStruct((n_seq, GROUP_W), F32),
                   jax.ShapeDtypeStruct((CONV_W - 1, n_seq, GROUP_W), F32)],
        compiler_params=pltpu.CompilerParams(vmem_limit_bytes=VMEM_LIMIT),
        name="lru_sample",
    )(zl_t, h0, cbuf, lw["cw"], lw["cb"], lw["wa"], lw["wx"], lw["ba"], lw["bx"], lw["lam"], lw["og"])
    return jnp.transpose(o_t, (1, 0, 2)).reshape(n_seq * TS, GROUP_W), h_t, conv_t


def _page_specs(layer, tail, n_pages):
    zeros = (0,) * len(tail)
    return [pl.BlockSpec((1, 1) + tail, functools.partial(lambda j, s, pt: (layer, pt[s, j]) + zeros, j))
            for j in range(n_pages)]


def _seq_spec(width, rows=TS):
    return pl.BlockSpec((rows, width), lambda s, pt: (s, 0))


def _const_spec(shape):
    zeros = (0,) * len(shape)
    return pl.BlockSpec(shape, lambda s, pt: zeros)


def _pad_rows(a, rows):
    return jnp.concatenate([a, jnp.zeros((rows - a.shape[0], a.shape[1]), a.dtype)], axis=0)


def _rep_rows(a, t):
    return jnp.concatenate([jnp.broadcast_to(a[h:h + 1, :], (t, a.shape[1])) for h in range(GROUP_HEADS)], axis=0)


def _sample_pos(past):
    return _rep4(past + lax.broadcasted_iota(jnp.int32, (TS, 1), 0))


def _weighted_values(p, vals):
    o = jnp.zeros((p.shape[0], GROUP_W), F32)
    for j, (kind, v) in enumerate(vals):
        pj = p[:, j * LANES:(j + 1) * LANES]
        o = o + (_dot_nt(pj, v) if kind == "t" else _dot(pj, v))
    return o


def _fox_sample_kernel(pt_ref, q_ref, kvn_ref, lfn_ref, u_ref, og_ref, *refs, n_pages, page):
    kv_refs = refs[:n_pages]
    lf_refs = refs[n_pages:2 * n_pages]
    o_ref = refs[2 * n_pages]
    past = n_pages * page
    qs = _stack_heads(q_ref[...]).astype(BF16)
    u = u_ref[...]
    off = jnp.zeros((GROUP_HEADS, 1), F32)
    logits, fcs, vals = [], [], []
    for j in range(n_pages):
        c = _dot_x3(lf_refs[j][0, 0], u) + off
        off = c[:, page - 1:page]
        fcs.append(c)
        logits.append(_dot(qs, kv_refs[j][0, 0, 0:GROUP_W, :].astype(BF16)))
        vals.append(("t", kv_refs[j][0, 0, GROUP_W:2 * GROUP_W, :].astype(BF16)))
    cn = _dot_x3(lfn_ref[0], u) + off
    fcs.append(cn)
    kvn = _pad_rows(kvn_ref[...], LANES).astype(BF16)
    logits.append(_dot_nt(qs, kvn[:, 0:GROUP_W]))
    vals.append(("n", kvn[:, GROUP_W:2 * GROUP_W]))

    lane = lax.broadcasted_iota(jnp.int32, (1, LANES), 1)
    t_row = lax.broadcasted_iota(jnp.int32, (GROUP_HEADS * TS, 1), 0) % TS
    fq = jnp.sum(jnp.where(lane == t_row, _rep_rows(cn, TS), 0.0), axis=-1, keepdims=True)
    decay = fq - jnp.concatenate([_rep_rows(c, TS) for c in fcs], axis=1)
    s = jnp.concatenate(logits, axis=1) + decay * LOG2E
    kpos = lax.broadcasted_iota(jnp.int32, (1, s.shape[1]), 1)
    p = _masked_softmax(s, kpos <= _sample_pos(past)).astype(BF16)
    o_ref[...] = _rms_rows(_unstack_heads(_weighted_values(p, vals), TS), og_ref[...])


def _fox_sample(page_table, layer, fq, fkv_new, lf_new_t, pool_kv_t, pool_lf_t, og):
    n_seq, n_pages = page_table.shape
    page = pool_kv_t.shape[-1]
    assert page == LANES
    u = jnp.asarray(np.triu(np.ones((page, page), np.float32)), BF16)
    grid_spec = pltpu.PrefetchScalarGridSpec(
        num_scalar_prefetch=1, grid=(n_seq,),
        in_specs=[_seq_spec(GROUP_W), _seq_spec(2 * GROUP_W),
                  pl.BlockSpec((1, GROUP_HEADS, LANES), lambda s, pt: (s, 0, 0)),
                  _const_spec((page, page)), _const_spec((1, GROUP_W))]
                 + _page_specs(layer, (2 * GROUP_W, page), n_pages)
                 + _page_specs(layer, (GROUP_HEADS, page), n_pages),
        out_specs=_seq_spec(GROUP_W))
    return pl.pallas_call(
        functools.partial(_fox_sample_kernel, n_pages=n_pages, page=page),
        grid_spec=grid_spec,
        out_shape=jax.ShapeDtypeStruct((n_seq * TS, GROUP_W), F32),
        compiler_params=_cparams(("arbitrary",)),
        name="fox_sample",
    )(page_table, fq, fkv_new, lf_new_t, u, og, *([pool_kv_t] * n_pages), *([pool_lf_t] * n_pages))


def _moba_sample_kernel(pt_ref, q_ref, kvn_ref, og_ref, *refs, n_pages, page):
    kv_refs = refs[:n_pages]
    o_ref = refs[n_pages]
    past = n_pages * page
    per_blk = MOBA_BLOCK // page
    nb = n_pages // per_blk
    qs32 = _stack_heads(q_ref[...])
    qs = qs32.astype(BF16)
    lane = lax.broadcasted_iota(jnp.int32, (1, LANES), 1)
    logits, vals = [], []
    kmean_t = jnp.zeros((GROUP_W, LANES), F32)
    for n in range(nb):
        ksum = jnp.zeros((GROUP_W, 1), F32)
        for j in range(n * per_blk, (n + 1) * per_blk):
            kt = kv_refs[j][0, 0, 0:GROUP_W, :]
            ksum = ksum + jnp.sum(kt, axis=1, keepdims=True)
            logits.append(_dot(qs, kt.astype(BF16)))
            vals.append(("t", kv_refs[j][0, 0, GROUP_W:2 * GROUP_W, :].astype(BF16)))
        kmean_t = jnp.where(lane == n, ksum * (1.0 / MOBA_BLOCK), kmean_t)
    qh, ql = _split2(qs32)
    kh, kl = _split2(kmean_t)
    sc = _dot(qh, kh) + _dot(qh, kl) + _dot(ql, kh)
    sc = jnp.where(lane < nb, sc, NEG_INF)
    sel = _top_select(sc, min(MOBA_TOPK, nb + 1), 1)
    masks = []
    for j in range(n_pages):
        n = j // per_blk
        masks.append(jnp.broadcast_to(sel[:, n:n + 1] > 0.5, (GROUP_HEADS * TS, page)))
    kvn = _pad_rows(kvn_ref[...], LANES).astype(BF16)
    logits.append(_dot_nt(qs, kvn[:, 0:GROUP_W]))
    vals.append(("n", kvn[:, GROUP_W:2 * GROUP_W]))
    kpos_new = past + lane
    masks.append(kpos_new <= _sample_pos(past))
    p = _masked_softmax(jnp.concatenate(logits, axis=1), jnp.concatenate(masks, axis=1)).astype(BF16)
    o_ref[...] = _rms_rows(_unstack_heads(_weighted_values(p, vals), TS), og_ref[...])


def _moba_sample(page_table, layer, mq, mkv_new, pool_kv_t, og):
    n_seq, n_pages = page_table.shape
    page = pool_kv_t.shape[-1]
    assert page == LANES and (n_pages * page) % MOBA_BLOCK == 0 and TS <= MOBA_BLOCK
    grid_spec = pltpu.PrefetchScalarGridSpec(
        num_scalar_prefetch=1, grid=(n_seq,),
        in_specs=[_seq_spec(GROUP_W), _seq_spec(2 * GROUP_W), _const_spec((1, GROUP_W))]
                 + _page_specs(layer, (2 * GROUP_W, page), n_pages),
        out_specs=_seq_spec(GROUP_W))
    return pl.pallas_call(
        functools.partial(_moba_sample_kernel, n_pages=n_pages, page=page),
        grid_spec=grid_spec,
        out_shape=jax.ShapeDtypeStruct((n_seq * TS, GROUP_W), F32),
        compiler_params=_cparams(("arbitrary",)),
        name="moba_sample",
    )(page_table, mq, mkv_new, og, *([pool_kv_t] * n_pages))


def _nsa_sample_kernel(pt_ref, qn_ref, qr_ref, small_ref, kvn_ref, winn_ref, win_ref, cpos_ref, cw_ref,
                       g_ref, rep_ref, ovl_ref, gexp_ref, og_ref, *refs, n_pages, page, ns, n_sel):
    pg_refs = refs[:n_pages]
    o_ref = refs[n_pages]
    tok_s, x2_s = refs[n_pages + 1:n_pages + 3]
    past = n_pages * page
    n_chunks = past // CMP_STRIDE
    pos = past + lax.broadcasted_iota(jnp.int32, (TS, 1), 0)
    pos_s = _rep4(pos)
    qn_s = _stack_heads(qn_ref[...]).astype(BF16)
    qr_s = _stack_heads(qr_ref[...]).astype(BF16)
    rep = rep_ref[...]
    rows4 = lambda a: jnp.concatenate([a] * GROUP_HEADS, axis=0)

    for p in range(n_pages):
        tok_s[p * page:(p + 1) * page, :] = pg_refs[p][0, 0, 0:LANES, :].T
    for j in range(CMP_STRIDE):
        x2_s[:, j * LANES:(j + 1) * LANES] = tok_s[pl.ds(j, n_chunks, stride=CMP_STRIDE), :]
    y = _cmp_windows(x2_s[...], cpos_ref, cw_ref, g_ref, n_chunks - 1)
    c16 = _dot(y.astype(BF16), rep).astype(BF16)

    lc = _dot_nt(qn_s, c16[:, 0:GROUP_W])
    cend = lax.broadcasted_iota(jnp.int32, (1, n_chunks), 1) * CMP_STRIDE + (CMP_LEN - 1)
    pc = _masked_softmax(lc, cend <= pos_s)
    o_c = _dot(pc.astype(BF16), c16[:, GROUP_W:2 * GROUP_W])

    pcs = pc[0:TS] + pc[TS:2 * TS] + pc[2 * TS:3 * TS] + pc[3 * TS:4 * TS]
    imp = _dot_x3(pcs, ovl_ref[...])
    jb = lax.broadcasted_iota(jnp.int32, (1, imp.shape[-1]), 1)
    cur = pos // SEL_BLOCK
    imp = jnp.where((jb == 0) | (jb == cur) | (jb == cur - 1), jnp.inf, imp)
    imp = jnp.where((jb * SEL_BLOCK > pos) | (jb >= ns), NEG_INF, imp)
    sel = _top_select(imp, n_sel, 1)

    lane = lax.broadcasted_iota(jnp.int32, (1, page), 1)
    per_blk = page // SEL_BLOCK
    logits, masks, vals = [], [], []
    for p in range(n_pages):
        logits.append(_dot(qr_s, rows4(pg_refs[p][0, 0, 2 * HEAD_DIM:3 * HEAD_DIM, :]).astype(BF16)))
        vals.append(("t", rows4(pg_refs[p][0, 0, 3 * HEAD_DIM:4 * HEAD_DIM, :]).astype(BF16)))
        chosen = jnp.zeros((TS, page), F32)
        for b in range(per_blk):
            n = p * per_blk + b
            chosen = jnp.where(lane // SEL_BLOCK == b, sel[:, n:n + 1], chosen)
        masks.append(_rep4(chosen > 0.5))
    kvn = _dot(_pad_rows(kvn_ref[:, LANES:2 * LANES], LANES).astype(BF16), rep).astype(BF16)
    logits.append(_dot_nt(qr_s, kvn[:, 0:GROUP_W]))
    vals.append(("n", kvn[:, GROUP_W:2 * GROUP_W]))
    kpos_new = past + lax.broadcasted_iota(jnp.int32, (1, LANES), 1)
    n_new = past // SEL_BLOCK
    chosen_new = (kpos_new // SEL_BLOCK == n_new) & (sel[:, n_new:n_new + 1] > 0.5)
    masks.append(_rep4(chosen_new) & (kpos_new <= pos_s))
    ps = _masked_softmax(jnp.concatenate(logits, axis=1), jnp.concatenate(masks, axis=1)).astype(BF16)
    o_s = _weighted_values(ps, vals)

    wb = win_ref.shape[-1]
    kwt = rows4(win_ref[0, 0, 0:HEAD_DIM, :]).astype(BF16)
    vwt = rows4(win_ref[0, 0, HEAD_DIM:2 * HEAD_DIM, :]).astype(BF16)
    kvwn = _dot(_pad_rows(winn_ref[...], LANES).astype(BF16), rep).astype(BF16)
    kpos = jnp.concatenate([past - wb + lax.broadcasted_iota(jnp.int32, (1, wb), 1), kpos_new], axis=1)
    lw = jnp.concatenate([_dot(qr_s, kwt), _dot_nt(qr_s, kvwn[:, 0:GROUP_W])], axis=1)
    wmask = (kpos >= past - wb) & (kpos <= pos_s) & (kpos >= pos_s - WINDOW)
    pw = _masked_softmax(lw, wmask).astype(BF16)
    o_w = _dot_nt(pw[:, 0:wb], vwt) + _dot(pw[:, wb:wb + LANES], kvwn[:, GROUP_W:2 * GROUP_W])

    o_ref[...] = _nsa_gate_mix(_unstack_heads(o_c, TS), _unstack_heads(o_s, TS), _unstack_heads(o_w, TS),
                               small_ref[...], gexp_ref, og_ref)


def _nsa_sample(page_table, layer, r, win_state_t, pool_kv_t, nw, og):
    n_seq, n_pages = page_table.shape
    page = pool_kv_t.shape[-1]
    past = n_pages * page
    assert page == LANES and TS < CMP_STRIDE and past % MOBA_BLOCK == 0 and TS <= SEL_BLOCK
    n_chunks = past // CMP_STRIDE
    ns = (-(-(past + TS) // MOBA_BLOCK) * MOBA_BLOCK) // SEL_BLOCK
    ns_pad = -(-ns // LANES) * LANES
    ovl = jnp.asarray(_overlap_matrix(n_chunks, n_chunks - 1, ns, ns_pad), BF16)
    gexp = jnp.asarray(_gate_expand_matrix(), BF16)
    width = CMP_STRIDE * LANES
    wb = win_state_t.shape[-1]
    grid_spec = pltpu.PrefetchScalarGridSpec(
        num_scalar_prefetch=1, grid=(n_seq,),
        in_specs=[_seq_spec(GROUP_W), _seq_spec(GROUP_W), _seq_spec(LANES), _seq_spec(GROUP_W), _seq_spec(LANES),
                  pl.BlockSpec((1, 1, 2 * HEAD_DIM, wb), lambda s, pt: (layer, s, 0, 0)),
                  _const_spec((8, width)), _const_spec((width, 2 * LANES)),
                  _const_spec((1, LANES)), _const_spec((LANES, 2 * GROUP_W)),
                  _const_spec((n_chunks, ns_pad)), _const_spec((LANES, 3 * GROUP_W)), _const_spec((1, GROUP_W))]
                 + _page_specs(layer, (GROUP_W, page), n_pages),
        out_specs=_seq_spec(GROUP_W),
        scratch_shapes=[pltpu.VMEM((past, LANES), F32), pltpu.VMEM((n_chunks, width), F32)])
    return pl.pallas_call(
        functools.partial(_nsa_sample_kernel, n_pages=n_pages, page=page, ns=ns, n_sel=min(SEL_TOPK, ns)),
        grid_spec=grid_spec,
        out_shape=jax.ShapeDtypeStruct((n_seq * TS, GROUP_W), F32),
        compiler_params=_cparams(("arbitrary",)),
        name="nsa_sample",
    )(page_table, r["nqn"], r["nqr"], r["small"], r["nkv"], r["nwin"], win_state_t,
      nw["cpos"], nw["cw"], nw["gcmp"], nw["rep"], ovl, gexp, og,
      *([pool_kv_t] * n_pages))


def _sample_layer(x, mod_rows, lw, layer, pools, page_table, n_seq, t_real, tabs):
    rows = n_seq * TS
    tm = min(512, rows)
    sh1, sc1, gt1, sh2, sc2, gt2 = [m.reshape(rows // tm, tm, D_MODEL) for m in jnp.split(mod_rows, 6, axis=-1)]
    r = _in_proj(x, sc1, sh1, lw["g1"], lw["w_in"], tabs, lw["gains"], lw["bf_row"],
                 tiles_per_seq=1, prompt=False, tm=tm)
    o_lru, h_t, conv_t = _lru_sample(r["zl"], pools["lru_h"][layer], jnp.transpose(pools["lru_conv"][layer], (1, 0, 2)),
                                     lw["lru"], n_seq, t_real)
    lf_new = r["small"][:, :GROUP_HEADS].reshape(n_seq, TS, GROUP_HEADS)
    lf_new_t = jnp.pad(jnp.transpose(lf_new, (0, 2, 1)), ((0, 0), (0, 0), (0, LANES - TS)))
    o_fox = _fox_sample(page_table, layer, r["fq"], r["fkv"], lf_new_t, pools["fox_kv_t"], pools["fox_lf_t"],
                        lw["og"][1])
    o_moba = _moba_sample(page_table, layer, r["mq"], r["mkv"], pools["moba_kv_t"], lw["og"][2])
    o_nsa = _nsa_sample(page_table, layer, r, pools["nsa_win_t"], pools["nsa_kv_t"], lw["nsa"], lw["og"][3])
    x1, hf = _out_proj(x, (o_lru, o_fox, o_moba, o_nsa), lw["w_out"], gt1, sc2, sh2, lw["g2"], 1, tm)
    y = _mlp(x1, hf, lw["w_up"], lw["w_down"], gt2, 1, tm, 1024)
    real = lambda a, tail: a.reshape((n_seq, TS) + tail)[:, :t_real]
    win_new = real(r["nwin"], (2, HEAD_DIM))
    win_prev = pools["nsa_win"][layer]
    keep = min(WINDOW, win_prev.shape[1] + t_real)
    win_all = jnp.concatenate([win_prev, win_new], axis=1)
    state = (real(r["fkv"], (2, GROUP_HEADS, HEAD_DIM)), real(r["small"][:, :GROUP_HEADS], (GROUP_HEADS,)),
             real(r["mkv"], (2, GROUP_HEADS, HEAD_DIM)), real(r["nkv"], (4, HEAD_DIM)),
             win_all[:, win_all.shape[1] - keep:], h_t, jnp.transpose(conv_t, (1, 0, 2)))
    return y, state


def _sample_trunk(x_sample, mods, lws, caches, page_table):
    n_seq, t_real, _ = x_sample.shape
    assert t_real <= TS
    depth, n_phys, page = caches["fox_kv"].shape[:3]
    past = page_table.shape[1] * page
    rows = n_seq * TS
    x = jnp.pad(x_sample, ((0, 0), (0, TS - t_real), (0, 0))).reshape(rows, D_MODEL)
    tm = min(512, rows)
    tabs = _rope_tables(past + (jnp.arange(tm, dtype=jnp.int32) % TS))
    pools = dict(
        fox_kv_t=jnp.transpose(caches["fox_kv"], (0, 1, 3, 4, 5, 2)).reshape(depth, n_phys, 2 * GROUP_W, page),
        fox_lf_t=jnp.transpose(caches["fox_logf"], (0, 1, 3, 2)),
        moba_kv_t=jnp.transpose(caches["moba_kv"], (0, 1, 3, 4, 5, 2)).reshape(depth, n_phys, 2 * GROUP_W, page),
        nsa_kv_t=jnp.transpose(caches["nsa_kv"], (0, 1, 3, 4, 2)).reshape(depth, n_phys, GROUP_W, page),
        nsa_win_t=jnp.transpose(caches["nsa_win"], (0, 1, 3, 4, 2)).reshape(depth, n_seq, 2 * HEAD_DIM, -1),
        nsa_win=caches["nsa_win"], lru_h=caches["lru_h"], lru_conv=caches["lru_conv"])
    states = []
    for l, lw in enumerate(lws):
        mod_rows = jnp.repeat(mods[l], TS, axis=0)
        x, st = _sample_layer(x, mod_rows, lw, l, pools, page_table, n_seq, t_real, tabs)
        states.append(st)
    stacked = [jnp.stack([s[i] for s in states], axis=0) for i in range(len(states[0]))]
    return x.reshape(n_seq, TS, D_MODEL)[:, :t_real], stacked


def kernel(x_prompt, x_sample, cache_fox_kv, cache_fox_logf, cache_moba_kv, cache_nsa_kv, state_nsa_win,
           state_lru_h, state_lru_conv, page_table, c_prompt, c_sample, norm_g, w_ada, b_ada, w_in,
           lru_conv_w, lru_conv_b, lru_wa, lru_ba, lru_wx, lru_bx, lru_lambda, fox_bf, fox_qk_g, moba_qk_g,
           nsa_qk_g, nsa_cmp_pos, nsa_cmp_w, out_g, w_out, w_up, w_down):
    p = dict(norm_g=norm_g, w_in=w_in, lru_conv_w=lru_conv_w, lru_conv_b=lru_conv_b, lru_wa=lru_wa,
             lru_ba=lru_ba, lru_wx=lru_wx, lru_bx=lru_bx, lru_lambda=lru_lambda, fox_bf=fox_bf,
             fox_qk_g=fox_qk_g, moba_qk_g=moba_qk_g, nsa_qk_g=nsa_qk_g, nsa_cmp_pos=nsa_cmp_pos,
             nsa_cmp_w=nsa_cmp_w, out_g=out_g, w_out=w_out, w_up=w_up, w_down=w_down)
    depth = w_in.shape[0]
    batch = x_prompt.shape[0]
    dec_batch = x_sample.shape[0]
    n_c = batch + dec_batch
    n_c_pad = -(-n_c // 8) * 8
    c_all = jnp.concatenate([c_prompt, c_sample, jnp.zeros((n_c_pad - n_c, D_MODEL), F32)], axis=0)
    mods = _ada(c_all, w_ada, b_ada)
    lws = [_layer_weights(p, l) for l in range(depth)]
    y_prompt, st_p = _prompt_trunk(x_prompt, [mods[l, :batch] for l in range(depth)], lws)
    caches = dict(fox_kv=cache_fox_kv, fox_logf=cache_fox_logf, moba_kv=cache_moba_kv, nsa_kv=cache_nsa_kv,
                  nsa_win=state_nsa_win, lru_h=state_lru_h, lru_conv=state_lru_conv)
    y_sample, st_s = _sample_trunk(x_sample, [mods[l, batch:n_c] for l in range(depth)], lws, caches, page_table)
    return (y_prompt, y_sample) + tuple(st_p) + tuple(st_s)
```

```python
import functools

import numpy as np
import jax
import jax.numpy as jnp
from jax import lax
from jax.experimental import pallas as pl
from jax.experimental.pallas import tpu as pltpu

F32 = jnp.float32
BF16 = jnp.bfloat16

D_MODEL = 1024
N_MIXERS = 4
GROUP_W = 256
HEAD_DIM = 64
GROUP_HEADS = 4
D_FF = 4 * D_MODEL
CONV_W = 4
LRU_C = 8.0
ROPE_THETA = 500000.0
ROPE_DIM = HEAD_DIM // 4
ROPE_HALF = ROPE_DIM // 2
MOBA_BLOCK = 256
MOBA_TOPK = 3
CMP_LEN = 32
CMP_STRIDE = 16
SEL_BLOCK = 64
SEL_TOPK = 16
WINDOW = 512
EPS = 1e-6
LOG2E = float(np.log2(np.e))
QSCALE = HEAD_DIM ** -0.5 * LOG2E
LANES = 128
IN_W = 22 * LANES
NEG_INF = float("-inf")
VMEM_LIMIT = 56 * 1024 * 1024

TS = 8
TQ = LANES
TK = MOBA_BLOCK
VH = HEAD_DIM + 16


def _cparams(sem):
    return pltpu.CompilerParams(dimension_semantics=sem, vmem_limit_bytes=VMEM_LIMIT)


def _dot(a, b):
    return jnp.dot(a, b, preferred_element_type=F32)


def _dot_nt(a, b):
    return lax.dot_general(a, b, (((1,), (1,)), ((), ())), preferred_element_type=F32)


def _split2(x):
    hi = x.astype(BF16)
    lo = (x - hi.astype(F32)).astype(BF16)
    return hi, lo


def _split3(x):
    hi = x.astype(BF16)
    r = x - hi.astype(F32)
    mid = r.astype(BF16)
    lo = (r - mid.astype(F32)).astype(BF16)
    return hi, mid, lo


def _dot_x3(x, m):
    hi, mid, lo = _split3(x)
    return _dot(hi, m) + _dot(mid, m) + _dot(lo, m)


def _dot_3x(m, x):
    hi, mid, lo = _split3(x)
    return _dot(m, hi) + _dot(m, mid) + _dot(m, lo)


def _log_sigmoid(x):
    return jnp.minimum(x, 0.0) - jnp.log1p(jnp.exp(-jnp.abs(x)))


def _softplus(x):
    return jnp.maximum(x, 0.0) + jnp.log1p(jnp.exp(-jnp.abs(x)))


def _gelu_tanh(x):
    return 0.5 * x * (1.0 + jnp.tanh(np.sqrt(2.0 / np.pi).astype(np.float32) * (x + 0.044715 * (x * x * x))))


def _rms_rows(x, g):
    return x * lax.rsqrt(jnp.mean(x * x, axis=-1, keepdims=True) + EPS) * g


def _head_lane(width=GROUP_W):
    return lax.broadcasted_iota(jnp.int32, (1, width), 1) // HEAD_DIM


def _stack_heads(q):
    hl = _head_lane()
    zero = jnp.zeros_like(q)
    return jnp.concatenate([jnp.where(hl == h, q, zero) for h in range(GROUP_HEADS)], axis=0)


def _unstack_heads(o, t):
    hl = _head_lane()
    out = jnp.zeros((t, GROUP_W), o.dtype)
    for h in range(GROUP_HEADS):
        out = out + jnp.where(hl == h, o[h * t:(h + 1) * t], 0.0)
    return out


def _rep4(a):
    return jnp.concatenate([a] * GROUP_HEADS, axis=0)


def _lanes4(a):
    return jnp.concatenate([a] * GROUP_HEADS, axis=1)


def _masked_softmax(l, mask, axis=-1):
    if mask is not None:
        l = jnp.where(mask, l, NEG_INF)
    m = jnp.max(l, axis=axis, keepdims=True)
    m = jnp.where(m > NEG_INF, m, 0.0)
    p = jnp.exp2(l - m)
    return p * (1.0 / jnp.maximum(jnp.sum(p, axis=axis, keepdims=True), 1e-30))


def _top_select(score, n_pick, axis):
    width = score.shape[axis]
    shape = [1, 1]
    shape[axis] = width
    j = lax.broadcasted_iota(jnp.int32, tuple(shape), axis)
    sel = jnp.zeros(score.shape, F32)
    work = score
    for _ in range(n_pick):
        mx = jnp.max(work, axis=axis, keepdims=True)
        cand = (work == mx) & (mx > NEG_INF)
        idx = jnp.min(jnp.where(cand, j, width), axis=axis, keepdims=True)
        pick = j == idx
        sel = jnp.where(pick, 1.0, sel)
        work = jnp.where(pick, NEG_INF, work)
    return sel


def _rank_select(score, n_pick, n_cand):
    j = lax.broadcasted_iota(jnp.int32, (1, score.shape[-1]), 1)
    rank = jnp.zeros(score.shape, F32)
    for k in range(n_cand):
        col = score[:, k:k + 1]
        rank = rank + jnp.where((col > score) | ((col == score) & (j > k)), 1.0, 0.0)
    return jnp.where((rank < n_pick) & (score > NEG_INF), 1.0, 0.0)


def _stacked_q_t(q):
    qt = q.astype(F32).T
    row_head = lax.broadcasted_iota(jnp.int32, (GROUP_W, 1), 0) // HEAD_DIM
    return jnp.concatenate([jnp.where(row_head == h, qt, 0.0) for h in range(GROUP_HEADS)], axis=1).astype(BF16)


def _flash_init(m_s, acc_s):
    m_s[...] = jnp.full_like(m_s, NEG_INF)
    acc_s[...] = jnp.zeros_like(acc_s)


def _flash_step(parts, m_s, acc_s, row_bias=None):
    shared = acc_s.shape[0] == VH
    m_old = m_s[...]
    m_new = m_old
    for s, _ in parts:
        tile_max = jnp.max(s, axis=0, keepdims=True)
        m_new = jnp.maximum(m_new, tile_max if row_bias is None else tile_max + row_bias)
    m_safe = jnp.where(m_new > NEG_INF, m_new, 0.0)
    alpha = jnp.exp2(m_old - m_safe)
    shift = m_safe if row_bias is None else m_safe - row_bias
    ps = [jnp.exp2(s - shift).astype(BF16) for s, _ in parts]
    if shared:
        acc = alpha * acc_s[...]
        for p, (_, vt) in zip(ps, parts):
            acc = acc + _dot(vt, p)
        acc_s[...] = acc
    else:
        for h in range(GROUP_HEADS):
            rows = slice(h * VH, (h + 1) * VH)
            lanes = slice(h * TQ, (h + 1) * TQ)
            acc = alpha[:, lanes] * acc_s[rows, :]
            for p, (_, vt) in zip(ps, parts):
                acc = acc + _dot(vt[rows, :], p[:, lanes])
            acc_s[rows, :] = acc
    m_s[...] = m_new


def _flash_pipelined(n_tiles, n_plain, logits_fn, vt_fn, s_buf, m_s, acc_s, row_bias=None):
    n_pairs = (n_tiles + 1) // 2

    def produce(pair, slot, causal):
        s_buf[slot, 0] = logits_fn(2 * pair, causal)
        s_buf[slot, 1] = logits_fn(2 * pair + 1, causal)

    def half_step(pair, src, dst):
        def run(causal):
            produce(pair + 1, dst, causal)
            _flash_step([(s_buf[src, 0], vt_fn(2 * pair)), (s_buf[src, 1], vt_fn(2 * pair + 1))],
                        m_s, acc_s, row_bias)

        plain = 2 * pair + 3 < n_plain
        pl.when(plain)(lambda: run(False))
        pl.when(jnp.logical_not(plain))(lambda: run(True))

    produce(0, 0, True)

    def body(j, carry):
        half_step(2 * j, 0, 1)
        pl.when(2 * j + 1 < n_pairs)(lambda: half_step(2 * j + 1, 1, 0))
        return carry

    lax.fori_loop(0, (n_pairs + 1) // 2, body, 0)


def _unstack_shared(a):
    return jnp.concatenate([a[:, h * TQ:(h + 1) * TQ] for h in range(GROUP_HEADS)], axis=0).T


def _flash_result(acc_s):
    parts = []
    for h in range(GROUP_HEADS):
        if acc_s.shape[0] == VH:
            blk = acc_s[:, h * TQ:(h + 1) * TQ]
        else:
            blk = acc_s[h * VH:(h + 1) * VH, :]
        parts.append(blk[0:HEAD_DIM, :] * (1.0 / jnp.maximum(blk[HEAD_DIM:HEAD_DIM + 1, :], 1e-30)))
    return jnp.concatenate(parts, axis=0).T


def _flash_scratch(shared):
    acc = (VH, GROUP_HEADS * TQ) if shared else (GROUP_HEADS * VH, TQ)
    return [pltpu.VMEM((1, GROUP_HEADS * TQ), F32), pltpu.VMEM(acc, F32),
            pltpu.VMEM((2, 2, TK, GROUP_HEADS * TQ), F32)]


def _resident(shape, index_map):
    return pl.BlockSpec(shape, index_map, pipeline_mode=pl.Buffered(1))


def _key_pos(k0, n):
    return k0 + lax.broadcasted_iota(jnp.int32, (n, 1), 0)


def _query_pos(q0):
    return q0 + lax.broadcasted_iota(jnp.int32, (1, TQ), 1)


def _ada_kernel(c_ref, w_ref, b_ref, o_ref):
    c = c_ref[...]
    a = (c * jax.nn.sigmoid(c)).astype(BF16)
    o_ref[0] = _dot(a, w_ref[0].astype(BF16)) + b_ref[0]


def _ada(c_all, w_ada, b_ada):
    depth, _, n = w_ada.shape
    rows = c_all.shape[0]
    tn = 1024
    return pl.pallas_call(
        _ada_kernel,
        grid=(depth, n // tn),
        in_specs=[pl.BlockSpec((rows, D_MODEL), lambda l, j: (0, 0)),
                  pl.BlockSpec((1, D_MODEL, tn), lambda l, j: (l, 0, j)),
                  pl.BlockSpec((1, 1, tn), lambda l, j: (l, 0, j))],
        out_specs=pl.BlockSpec((1, rows, tn), lambda l, j: (l, 0, j)),
        out_shape=jax.ShapeDtypeStruct((depth, rows, n), F32),
        compiler_params=_cparams(("arbitrary", "arbitrary")),
        name="ada_mod",
    )(c_all, w_ada, b_ada.reshape(depth, 1, n))


def _in_perm():
    gw, hd = GROUP_W, HEAD_DIM
    nkv0 = 9 * gw + GROUP_HEADS
    cols = list(range(0, 5 * gw))
    cols += list(range(5 * gw + 4, 8 * gw + 4))
    cols += list(range(8 * gw + 4, 9 * gw + 4))
    part = lambda k: list(range(nkv0 + k * hd, nkv0 + (k + 1) * hd))
    cols += part(0) + part(1)
    cols += part(2) + part(3)
    cols += part(4) + part(5)
    cols += list(range(5 * gw, 5 * gw + 4))
    cols += list(range(nkv0 + 6 * hd, nkv0 + 6 * hd + 12))
    return np.asarray(cols, np.int32)


C_LRU, C_FQ, C_FK, C_FV, C_MQ, C_MK, C_MV, C_NQ, C_KCVC, C_SEL, C_WIN, C_SMALL = (
    0, 512, 768, 1024, 1280, 1536, 1792, 2048, 2304, 2432, 2560, 2688)


def _block_diag_mean():
    m = np.zeros((GROUP_W, GROUP_W), np.float32)
    for h in range(GROUP_HEADS):
        m[h * HEAD_DIM:(h + 1) * HEAD_DIM, h * HEAD_DIM:(h + 1) * HEAD_DIM] = 1.0 / HEAD_DIM
    return m


def _rep_matrix():
    m = np.zeros((LANES, 2 * GROUP_W), np.float32)
    for j in range(HEAD_DIM):
        for k in range(GROUP_HEADS):
            m[j, k * HEAD_DIM + j] = 1.0
            m[HEAD_DIM + j, GROUP_W + k * HEAD_DIM + j] = 1.0
    return m


def _head_expand_matrix():
    m = np.zeros((LANES, GROUP_HEADS * LANES), np.float32)
    for h in range(GROUP_HEADS):
        m[h, h * LANES:(h + 1) * LANES] = 1.0
    return m


def _rope_tables(pos):
    inv = jnp.power(ROPE_THETA, -jnp.arange(ROPE_HALF, dtype=F32) / ROPE_HALF)
    ang = pos.astype(F32)[:, None] * inv[None, :]
    cos, sin = jnp.cos(ang), jnp.sin(ang)
    n = pos.shape[0]
    ones = jnp.ones((n, HEAD_DIM - ROPE_DIM), F32)
    zeros = jnp.zeros((n, HEAD_DIM - ROPE_DIM), F32)
    z8 = jnp.zeros((n, ROPE_HALF), F32)
    c = jnp.concatenate([cos, cos, ones], axis=1)
    sa = jnp.concatenate([z8, sin, zeros], axis=1)
    sb = jnp.concatenate([-sin, z8, zeros], axis=1)
    two = lambda a: jnp.concatenate([a, a], axis=1)
    return two(c), two(sa), two(sb)


_IN_COMMON = ("zl", "fq", "fkv", "small", "mq", "mkv", "nqn", "nqr", "nkv", "nwin")
_IN_PROMPT = ("fk16", "fvt", "fcrep", "mk16", "mvt", "kmean", "ks16", "vst", "kw16", "vwt", "kcvc")


def _in_proj_kernel(*refs, tiles_per_seq, prompt, tm):
    n_in = 14 if prompt else 12
    (x_ref, sc_ref, sh_ref, g_ref, w_ref, tc_ref, ta_ref, tb_ref, gains_ref, bf_ref, bd_ref,
     rep_ref) = refs[:12]
    names = _IN_COMMON + (_IN_PROMPT if prompt else ())
    out = dict(zip(names, refs[n_in:n_in + len(names)]))

    x = x_ref[...]
    xn = x * lax.rsqrt(jnp.mean(x * x, axis=-1, keepdims=True) + EPS) * g_ref[...]
    hm = (xn * (1.0 + sc_ref[0]) + sh_ref[0]).astype(BF16)

    def proj(c0, width):
        return _dot(hm, w_ref[:, c0:c0 + width])

    bd = bd_ref[...]
    tc, ta, tb = tc_ref[...], ta_ref[...], tb_ref[...]

    def hnorm(z, row):
        hi, lo = _split2(z * z)
        ms = _dot(hi, bd) + _dot(lo, bd)
        return z * lax.rsqrt(ms + EPS) * gains_ref[row:row + 1, :]

    def rope128(y):
        return y * tc + pltpu.roll(y, 8, axis=1) * ta + pltpu.roll(y, LANES - 8, axis=1) * tb

    def rope(y):
        return jnp.concatenate([rope128(y[:, :LANES]), rope128(y[:, LANES:])], axis=1)

    def store_vt(name, at, n_heads, tile, ones):
        step = VH if ones else HEAD_DIM
        for j in range(tm // tile):
            for h in range(n_heads):
                blk = at[h * HEAD_DIM:(h + 1) * HEAD_DIM, j * tile:(j + 1) * tile].astype(BF16)
                out[name][j, h * step:h * step + HEAD_DIM, :] = blk
                if ones:
                    out[name][j, h * step + HEAD_DIM:(h + 1) * step, :] = jnp.ones((VH - HEAD_DIM, tile), BF16)

    out["zl"][...] = proj(C_LRU, 512)

    out["fq"][...] = (hnorm(proj(C_FQ, 256), 0) * QSCALE).astype(out["fq"].dtype)
    fk = hnorm(proj(C_FK, 256), 1)
    fv = proj(C_FV, 256)
    out["fkv"][...] = jnp.concatenate([fk, fv], axis=1)

    out["mq"][...] = rope(hnorm(proj(C_MQ, 256), 2)) * QSCALE
    mk = rope(hnorm(proj(C_MK, 256), 3))
    mv = proj(C_MV, 256)
    out["mkv"][...] = jnp.concatenate([mk, mv], axis=1)

    nqn = hnorm(proj(C_NQ, 256), 4)
    out["nqn"][...] = (nqn * QSCALE).astype(out["nqn"].dtype)
    out["nqr"][...] = (rope(nqn) * QSCALE).astype(out["nqr"].dtype)

    kcvc = proj(C_KCVC, LANES)
    sw = proj(C_SEL, 2 * LANES)
    first = (lax.broadcasted_iota(jnp.int32, (1, 2 * LANES), 1) % LANES) < HEAD_DIM
    swn = jnp.where(first, rope(hnorm(sw, 5)), sw)
    sel_rows = swn[:, :LANES]
    win_rows = swn[:, LANES:]
    out["nkv"][...] = jnp.concatenate([kcvc, sel_rows], axis=1)
    out["nwin"][...] = win_rows

    zs = proj(C_SMALL, LANES)
    lane = lax.broadcasted_iota(jnp.int32, (1, LANES), 1)
    logf = _log_sigmoid(zs + bf_ref[...])
    gate = jax.nn.sigmoid(zs)
    out["small"][...] = jnp.where(lane < GROUP_HEADS, logf, jnp.where(lane < 16, gate, 0.0))

    if prompt:
        tri_ref, fexp_ref = refs[12:14]
        carry_ref = refs[n_in + len(names)]
        out["fk16"][...] = fk.astype(BF16)
        store_vt("fvt", fv.T, GROUP_HEADS, TK, True)
        out["mk16"][...] = mk.astype(BF16)
        store_vt("mvt", mv.T, GROUP_HEADS, TK, True)
        out["kcvc"][...] = kcvc
        rep = rep_ref[...]
        out["ks16"][...] = _dot(sel_rows.astype(BF16), rep)[:, :GROUP_W].astype(BF16)
        out["kw16"][...] = _dot(win_rows.astype(BF16), rep)[:, :GROUP_W].astype(BF16)
        store_vt("vst", sel_rows.T[HEAD_DIM:, :], 1, TK, True)
        store_vt("vwt", win_rows.T[HEAD_DIM:, :], 1, TQ, False)

        i = pl.program_id(0)

        @pl.when(i % tiles_per_seq == 0)
        def _():
            carry_ref[...] = jnp.zeros_like(carry_ref)

        lf = jnp.where(lane < GROUP_HEADS, logf, 0.0)
        fc = _dot_3x(tri_ref[...], lf) + carry_ref[...]
        carry_ref[...] = fc[tm - 1:tm, :]
        out["fcrep"][...] = _dot_x3(fc, fexp_ref[...]) * LOG2E
        nblk = tm // MOBA_BLOCK
        out["kmean"][0] = jnp.sum(mk.reshape(nblk, MOBA_BLOCK, GROUP_W), axis=1) * (1.0 / MOBA_BLOCK)


def _in_proj(x, sc, sh, g, w, tabs, gains, bf_row, *, tiles_per_seq, prompt, tm):
    rows = x.shape[0]
    n_tiles = rows // tm
    tb = sc.shape[1]
    n_tab_tiles = tabs[0].shape[0] // tm
    mod_map = (lambda i: (i // tiles_per_seq, 0, 0)) if tb == 1 else (lambda i: (i, 0, 0))
    tab_map = lambda i: (i % n_tab_tiles, 0)
    const = lambda i: (0, 0)
    row_map = lambda i: (i, 0)
    bd = jnp.asarray(_block_diag_mean(), BF16)
    rep = jnp.asarray(_rep_matrix(), BF16)
    in_specs = [pl.BlockSpec((tm, D_MODEL), row_map),
                pl.BlockSpec((1, tb, D_MODEL), mod_map),
                pl.BlockSpec((1, tb, D_MODEL), mod_map),
                pl.BlockSpec((1, D_MODEL), const),
                pl.BlockSpec((D_MODEL, IN_W), const),
                pl.BlockSpec((tm, LANES), tab_map),
                pl.BlockSpec((tm, LANES), tab_map),
                pl.BlockSpec((tm, LANES), tab_map),
                pl.BlockSpec((8, GROUP_W), const),
                pl.BlockSpec((1, LANES), const),
                pl.BlockSpec((GROUP_W, GROUP_W), const),
                pl.BlockSpec((LANES, 2 * GROUP_W), const)]
    args = [x, sc, sh, g, w, tabs[0], tabs[1], tabs[2], gains, bf_row, bd, rep]
    if prompt:
        tri = jnp.asarray(np.tril(np.ones((tm, tm), np.float32)), BF16)
        fexp = jnp.asarray(_head_expand_matrix(), BF16)
        in_specs += [pl.BlockSpec((tm, tm), const), pl.BlockSpec((LANES, GROUP_HEADS * LANES), const)]
        args += [tri, fexp]

    def o(width, dtype):
        return jax.ShapeDtypeStruct((rows, width), dtype), pl.BlockSpec((tm, width), row_map)

    def ot(tile, n_rows=GROUP_W):
        return (jax.ShapeDtypeStruct((rows // tile, n_rows, tile), BF16),
                pl.BlockSpec((tm // tile, n_rows, tile), lambda i: (i, 0, 0)))

    q_dtype = BF16 if prompt else F32
    outs = dict(zl=o(512, F32), fq=o(256, q_dtype), fkv=o(512, F32), small=o(LANES, F32), mq=o(256, F32),
                mkv=o(512, F32), nqn=o(256, q_dtype), nqr=o(256, q_dtype), nkv=o(256, F32), nwin=o(LANES, F32))
    scratch = []
    if prompt:
        nblk = tm // MOBA_BLOCK
        outs.update(fk16=o(256, BF16), fvt=ot(TK, GROUP_HEADS * VH), fcrep=o(GROUP_HEADS * LANES, F32),
                    mk16=o(256, BF16), mvt=ot(TK, GROUP_HEADS * VH),
                    kmean=(jax.ShapeDtypeStruct((n_tiles, nblk, GROUP_W), F32),
                           pl.BlockSpec((1, nblk, GROUP_W), lambda i: (i, 0, 0))),
                    ks16=o(256, BF16), vst=ot(TK, VH), kw16=o(256, BF16), vwt=ot(TQ, HEAD_DIM),
                    kcvc=o(LANES, F32))
        scratch.append(pltpu.VMEM((1, LANES), F32))
    names = _IN_COMMON + (_IN_PROMPT if prompt else ())
    res = pl.pallas_call(
        functools.partial(_in_proj_kernel, tiles_per_seq=tiles_per_seq, prompt=prompt, tm=tm),
        grid=(n_tiles,),
        in_specs=in_specs,
        out_specs=[outs[n][1] for n in names],
        out_shape=[outs[n][0] for n in names],
        scratch_shapes=scratch,
        compiler_params=_cparams(("arbitrary",)),
        name="in_proj_prompt" if prompt else "in_proj_sample",
    )(*args)
    return dict(zip(names, res))


def _lru_coeffs(u, wa_ref, wx_ref, ba_ref, bx_ref, lam_ref):
    u16 = u.astype(BF16)
    r = jax.nn.sigmoid(_dot(u16, wa_ref[...]) + ba_ref[...])
    i = jax.nn.sigmoid(_dot(u16, wx_ref[...]) + bx_ref[...])
    log_a = -LRU_C * r * _softplus(-lam_ref[...])
    a = jnp.exp(log_a)
    b = jnp.sqrt(-jnp.tanh(log_a) * (a * a + 1.0)) * i * u
    return a, b


def _lru_prompt_kernel(zl_ref, cw_ref, cb_ref, wa_ref, wx_ref, ba_ref, bx_ref, lam_ref, og_ref,
                       o_ref, ht_ref, cnew_ref, xp_s, a_s, b_s, hs_s, h_s, *, tm):
    t = pl.program_id(1)

    @pl.when(t == 0)
    def _():
        h_s[...] = jnp.zeros_like(h_s)
        xp_s[0:8, :] = jnp.zeros((8, GROUP_W), F32)

    xa = zl_ref[:, 0:GROUP_W]
    ga = zl_ref[:, GROUP_W:2 * GROUP_W]
    xp_s[8:8 + tm, :] = xa
    u = (cb_ref[...] + cw_ref[0:1, :] * xp_s[5:5 + tm, :] + cw_ref[1:2, :] * xp_s[6:6 + tm, :]
         + cw_ref[2:3, :] * xp_s[7:7 + tm, :] + cw_ref[3:4, :] * xa)
    tail = xp_s[tm:tm + 8, :]
    xp_s[0:8, :] = tail
    cnew_ref[0] = tail
    a, b = _lru_coeffs(u, wa_ref, wx_ref, ba_ref, bx_ref, lam_ref)
    row8 = lax.broadcasted_iota(jnp.int32, (tm, 1), 0) % 8
    for s in (1, 2, 4):
        keep = row8 >= s
        b = jnp.where(keep, a * pltpu.roll(b, s, axis=0) + b, b)
        a = jnp.where(keep, a * pltpu.roll(a, s, axis=0), a)
    a_s[...] = a
    b_s[...] = b

    def body(j, h):
        r0 = pl.multiple_of(j * 8, 8)
        h8 = a_s[pl.ds(r0, 8), :] * h + b_s[pl.ds(r0, 8), :]
        hs_s[pl.ds(r0, 8), :] = h8
        return h8[7:8, :]

    h = lax.fori_loop(0, tm // 8, body, h_s[...], unroll=8)
    h_s[...] = h
    ht_ref[0] = h
    o = hs_s[...] * _gelu_tanh(ga)
    o_ref[...] = _rms_rows(o, og_ref[...]).astype(BF16)


def _lru_prompt(zl, lw, batch, seq, tm):
    nt = seq // tm
    const = lambda b, t: (0, 0)
    vec = pl.BlockSpec((1, GROUP_W), const)
    mat = pl.BlockSpec((GROUP_W, GROUP_W), const)
    return pl.pallas_call(
        functools.partial(_lru_prompt_kernel, tm=tm),
        grid=(batch, nt),
        in_specs=[pl.BlockSpec((tm, 2 * GROUP_W), lambda b, t: (b * nt + t, 0)),
                  pl.BlockSpec((CONV_W, GROUP_W), const), vec, mat, mat, vec, vec, vec, vec],
        out_specs=[pl.BlockSpec((tm, GROUP_W), lambda b, t: (b * nt + t, 0)),
                   pl.BlockSpec((1, 1, GROUP_W), lambda b, t: (b, 0, 0)),
                   pl.BlockSpec((1, 8, GROUP_W), lambda b, t: (b, 0, 0))],
        out_shape=[jax.ShapeDtypeStruct((batch * seq, GROUP_W), BF16),
                   jax.ShapeDtypeStruct((batch, 1, GROUP_W), F32),
                   jax.ShapeDtypeStruct((batch, 8, GROUP_W), F32)],
        scratch_shapes=[pltpu.VMEM((tm + 8, GROUP_W), F32), pltpu.VMEM((tm, GROUP_W), F32),
                        pltpu.VMEM((tm, GROUP_W), F32), pltpu.VMEM((tm, GROUP_W), F32),
                        pltpu.VMEM((1, GROUP_W), F32)],
        compiler_params=_cparams(("arbitrary", "arbitrary")),
        name="lru_prompt",
    )(zl, lw["cw"], lw["cb"], lw["wa"], lw["wx"], lw["ba"], lw["bx"], lw["lam"], lw["og"])


def _fox_prompt_kernel(q_ref, fcq_ref, k_ref, vt_ref, fck_ref, og_ref, o_ref, m_s, acc_s, s_buf):
    qi = pl.program_id(1)
    q0 = qi * TQ
    qst = _stacked_q_t(q_ref[...])
    fcq = fcq_ref[...]
    fq_row = jnp.concatenate([fcq[:, h * LANES:(h + 1) * LANES].T[0:1, :] for h in range(GROUP_HEADS)], axis=1)
    qpos = _lanes4(_query_pos(q0))
    _flash_init(m_s, acc_s)
    last = vt_ref.shape[0] - 1

    def logits(t, causal):
        k0 = pl.multiple_of(jnp.minimum(t, last) * TK, TK)
        s = _dot(k_ref[pl.ds(k0, TK), :], qst) - fck_ref[pl.ds(k0, TK), :]
        return jnp.where(_key_pos(t * TK, TK) <= qpos, s, NEG_INF) if causal else s

    n_full = q0 // TK
    _flash_pipelined(n_full + 1, n_full, logits, lambda t: vt_ref[jnp.minimum(t, last)], s_buf, m_s, acc_s, fq_row)
    o_ref[...] = _rms_rows(_flash_result(acc_s), og_ref[...]).astype(BF16)


def _fox_prompt(r, og, batch, seq):
    nq = seq // TQ
    nk = seq // TK
    row = lambda b, i: (b * nq + i, 0)
    return pl.pallas_call(
        _fox_prompt_kernel,
        grid=(batch, nq),
        in_specs=[pl.BlockSpec((TQ, GROUP_W), row),
                  pl.BlockSpec((TQ, GROUP_HEADS * LANES), row),
                  _resident((seq, GROUP_W), lambda b, i: (b, 0)),
                  _resident((nk, GROUP_HEADS * VH, TK), lambda b, i: (b, 0, 0)),
                  _resident((seq, GROUP_HEADS * LANES), lambda b, i: (b, 0)),
                  pl.BlockSpec((1, GROUP_W), lambda b, i: (0, 0))],
        out_specs=pl.BlockSpec((TQ, GROUP_W), row),
        out_shape=jax.ShapeDtypeStruct((batch * seq, GROUP_W), BF16),
        scratch_shapes=_flash_scratch(False),
        compiler_params=_cparams(("arbitrary", "arbitrary")),
        name="fox_prompt",
    )(r["fq"], r["fcrep"], r["fk16"], r["fvt"], r["fcrep"], og)


def _moba_prompt_kernel(q_ref, kmean_ref, k_ref, vt_ref, og_ref, o_ref, m_s, acc_s, s_buf, sel_s, *, nb):
    qi = pl.program_id(1)
    q0 = qi * TQ
    cur = q0 // MOBA_BLOCK
    q32 = q_ref[...]
    qt = q32.T
    row_head = lax.broadcasted_iota(jnp.int32, (GROUP_W, 1), 0) // HEAD_DIM
    qst32 = jnp.concatenate([jnp.where(row_head == h, qt, 0.0) for h in range(GROUP_HEADS)], axis=1)
    qst = qst32.astype(BF16)
    qh, ql = _split2(qst32)
    kh, kl = _split2(kmean_ref[0])
    sc = _dot(kh, qh) + _dot(kh, ql) + _dot(kl, qh)
    n_col = lax.broadcasted_iota(jnp.int32, (nb, 1), 0)
    sel = _top_select(jnp.where(n_col < cur, sc, NEG_INF), min(MOBA_TOPK, nb), 0)
    sel_s[...] = jnp.where(n_col == cur, 1.0, sel)
    _flash_init(m_s, acc_s)
    qpos = _lanes4(_query_pos(q0))

    def logits(n, causal):
        nc = jnp.minimum(n, nb - 1)
        k0 = pl.multiple_of(nc * MOBA_BLOCK, MOBA_BLOCK)
        s = _dot(k_ref[pl.ds(k0, MOBA_BLOCK), :], qst)
        keep = sel_s[pl.ds(nc, 1), :] > 0.5
        if causal:
            keep = keep & (_key_pos(n * MOBA_BLOCK, MOBA_BLOCK) <= qpos)
        return jnp.where(keep, s, NEG_INF)

    _flash_pipelined(cur + 1, cur, logits, lambda n: vt_ref[jnp.minimum(n, nb - 1)], s_buf, m_s, acc_s)
    o_ref[...] = _rms_rows(_flash_result(acc_s), og_ref[...]).astype(BF16)


def _moba_prompt(r, og, batch, seq):
    assert TK == MOBA_BLOCK
    nq = seq // TQ
    nb = seq // MOBA_BLOCK
    kmean = r["kmean"].reshape(batch, nb, GROUP_W)
    row = lambda b, i: (b * nq + i, 0)
    return pl.pallas_call(
        functools.partial(_moba_prompt_kernel, nb=nb),
        grid=(batch, nq),
        in_specs=[pl.BlockSpec((TQ, GROUP_W), row),
                  pl.BlockSpec((1, nb, GROUP_W), lambda b, i: (b, 0, 0)),
                  _resident((seq, GROUP_W), lambda b, i: (b, 0)),
                  _resident((nb, GROUP_HEADS * VH, TK), lambda b, i: (b, 0, 0)),
                  pl.BlockSpec((1, GROUP_W), lambda b, i: (0, 0))],
        out_specs=pl.BlockSpec((TQ, GROUP_W), row),
        out_shape=jax.ShapeDtypeStruct((batch * seq, GROUP_W), BF16),
        scratch_shapes=_flash_scratch(False) + [pltpu.VMEM((nb, GROUP_HEADS * TQ), F32)],
        compiler_params=_cparams(("arbitrary", "arbitrary")),
        name="moba_prompt",
    )(r["mq"], kmean, r["mk16"], r["mvt"], og)


def _cmp_weights(cw):
    w = cw.reshape(2, CMP_LEN, HEAD_DIM, HEAD_DIM)
    zero = jnp.zeros((CMP_STRIDE, HEAD_DIM, HEAD_DIM), cw.dtype)

    def half(j0):
        wk = w[0, j0:j0 + CMP_STRIDE]
        wv = w[1, j0:j0 + CMP_STRIDE]
        top = jnp.concatenate([wk, zero], axis=2)
        bot = jnp.concatenate([zero, wv], axis=2)
        return jnp.concatenate([top, bot], axis=1).reshape(CMP_STRIDE * LANES, LANES)

    return jnp.concatenate([half(0), half(CMP_STRIDE)], axis=1).astype(BF16)


def _cmp_pos(cpos):
    p = jnp.concatenate([cpos[0], cpos[1]], axis=1)
    rows = jnp.stack([p[:CMP_STRIDE].reshape(-1), p[CMP_STRIDE:].reshape(-1)])
    return jnp.concatenate([rows, jnp.zeros((6, rows.shape[1]), rows.dtype)], axis=0)


def _cmp_windows(x, pos_ref, w_ref, g_ref, n_valid):
    w = w_ref[...]
    y2 = _dot(x.astype(BF16), w)
    pw = _dot(pos_ref[...].astype(BF16), w)
    rows = y2.shape[0]
    y = (y2[:, :LANES] + pw[0:1, :LANES]) + pltpu.roll(y2[:, LANES:] + pw[1:2, LANES:], rows - 1, axis=0)
    lane = lax.broadcasted_iota(jnp.int32, (1, LANES), 1)
    is_k = lane < HEAD_DIM
    ms = jnp.sum(jnp.where(is_k, y * y, 0.0), axis=-1, keepdims=True) * (1.0 / HEAD_DIM)
    y = jnp.where(is_k, y * lax.rsqrt(ms + EPS) * g_ref[...], y)
    row = lax.broadcasted_iota(jnp.int32, (rows, 1), 0)
    return jnp.where(row < n_valid, y, 0.0)


def _cmp_prompt_kernel(x_ref, pos_ref, w_ref, g_ref, rep_ref, k_ref, vt_ref, *, n_valid):
    y = _cmp_windows(x_ref[...], pos_ref, w_ref, g_ref, n_valid)
    k_ref[...] = _dot(y.astype(BF16), rep_ref[...])[:, :GROUP_W].astype(BF16)
    vt_ref[0] = y.T[HEAD_DIM:, :].astype(BF16)


def _cmp_prompt(kcvc, nw, batch, seq):
    rows = seq // CMP_STRIDE
    width = CMP_STRIDE * LANES
    x2 = kcvc.reshape(batch * rows, width)
    n_valid = (seq - CMP_LEN) // CMP_STRIDE + 1
    const = lambda b: (0, 0)
    return pl.pallas_call(
        functools.partial(_cmp_prompt_kernel, n_valid=n_valid),
        grid=(batch,),
        in_specs=[pl.BlockSpec((rows, width), lambda b: (b, 0)),
                  pl.BlockSpec((8, width), const), pl.BlockSpec((width, 2 * LANES), const),
                  pl.BlockSpec((1, LANES), const), pl.BlockSpec((LANES, 2 * GROUP_W), const)],
        out_specs=[pl.BlockSpec((rows, GROUP_W), lambda b: (b, 0)),
                   pl.BlockSpec((1, HEAD_DIM, rows), lambda b: (b, 0, 0))],
        out_shape=[jax.ShapeDtypeStruct((batch * rows, GROUP_W), BF16),
                   jax.ShapeDtypeStruct((batch, HEAD_DIM, rows), BF16)],
        compiler_params=_cparams(("arbitrary",)),
        name="nsa_cmp_prompt",
    )(x2, nw["cpos"], nw["cw"], nw["gcmp"], nw["rep"])


def _overlap_matrix(n_rows, n_valid, ns, ns_pad):
    c = np.arange(n_rows)[:, None] * CMP_STRIDE
    b = np.arange(ns_pad)[None, :] * SEL_BLOCK
    m = (c < b + SEL_BLOCK) & (c + CMP_LEN > b)
    m &= (np.arange(n_rows)[:, None] < n_valid) & (np.arange(ns_pad)[None, :] < ns)
    return m.astype(np.float32)


def _gate_expand_matrix():
    m = np.zeros((LANES, 3 * GROUP_W), np.float32)
    for h in range(GROUP_HEADS):
        for g in range(3):
            m[4 + h * 3 + g, g * GROUP_W + h * HEAD_DIM:g * GROUP_W + (h + 1) * HEAD_DIM] = 1.0
    return m


def _nsa_gate_mix(o_c, o_s, o_w, small, gexp_ref, og_ref):
    gx = _dot_x3(small, gexp_ref[...])
    o = gx[:, 0:GROUP_W] * o_c + gx[:, GROUP_W:2 * GROUP_W] * o_s + gx[:, 2 * GROUP_W:3 * GROUP_W] * o_w
    return _rms_rows(o, og_ref[...])


def _nsa_prompt_kernel(qn_ref, qr_ref, small_ref, kc_ref, vct_ref, ovlt_ref, gexp_ref, ks_ref, vst_ref,
                       kw0_ref, kw1_ref, kw2_ref, kw3_ref, kw4_ref,
                       vw0_ref, vw1_ref, vw2_ref, vw3_ref, vw4_ref, og_ref, o_ref,
                       m_s, acc_s, s_buf, *, ns, n_sel):
    qi = pl.program_id(1)
    q0 = qi * TQ
    pos = _query_pos(q0)
    pos4 = _lanes4(pos)
    qnt = _stacked_q_t(qn_ref[...])
    qrt = _stacked_q_t(qr_ref[...])

    nc = kc_ref.shape[0]
    cend = lax.broadcasted_iota(jnp.int32, (nc, 1), 0) * CMP_STRIDE + (CMP_LEN - 1)
    pc = _masked_softmax(_dot(kc_ref[...], qnt), cend <= pos4, axis=0)
    o_c = _unstack_shared(_dot(vct_ref[0], pc.astype(BF16)))

    pcs = pc[:, 0:TQ] + pc[:, TQ:2 * TQ] + pc[:, 2 * TQ:3 * TQ] + pc[:, 3 * TQ:4 * TQ]
    imp = _dot_3x(ovlt_ref[...], pcs)
    ns_pad = imp.shape[0]
    j = lax.broadcasted_iota(jnp.int32, (ns_pad, 1), 0)
    cur = pos // SEL_BLOCK
    imp = jnp.where((j == 0) | (j == cur) | (j == cur - 1), jnp.inf, imp)
    imp = jnp.where((j * SEL_BLOCK > pos) | (j >= ns), NEG_INF, imp)
    sel16 = _top_select(imp, n_sel, 0).astype(BF16)

    _flash_init(m_s, acc_s)
    nk = (q0 + TQ + TK - 1) // TK
    blk_row = lax.broadcasted_iota(jnp.int32, (1, ns_pad), 1)
    key_blk = lax.broadcasted_iota(jnp.int32, (TK, 1), 0) // SEL_BLOCK

    last = vst_ref.shape[0] - 1

    def sel_logits(t, causal):
        tc = jnp.minimum(t, last)
        k0 = pl.multiple_of(tc * TK, TK)
        s = _dot(ks_ref[pl.ds(k0, TK), :], qrt)
        expand = (blk_row == tc * (TK // SEL_BLOCK) + key_blk).astype(BF16)
        mask = _dot(expand, sel16) > 0.5
        if causal:
            mask = mask & (_key_pos(t * TK, TK) <= pos)
        return s + _lanes4(jnp.where(mask, 0.0, NEG_INF))

    _flash_pipelined(nk, q0 // TK, sel_logits, lambda t: vst_ref[jnp.minimum(t, last)], s_buf, m_s, acc_s)
    o_s = _flash_result(acc_s)

    kws = (kw0_ref, kw1_ref, kw2_ref, kw3_ref, kw4_ref)
    vws = (vw0_ref, vw1_ref, vw2_ref, vw3_ref, vw4_ref)
    logits = []
    for t, kw in enumerate(kws):
        k0 = q0 + (t - 4) * TQ
        s = _dot(kw[...], qrt)
        if t == 0:
            s = jnp.where((_key_pos(k0, TQ) >= pos4 - WINDOW) & (k0 >= 0), s, NEG_INF)
        elif t == 4:
            s = jnp.where(_key_pos(k0, TQ) <= pos4, s, NEG_INF)
        else:
            s = s + jnp.where(k0 >= 0, 0.0, NEG_INF)
        logits.append(s)
    pw = _masked_softmax(jnp.concatenate(logits, axis=0), None, axis=0).astype(BF16)
    o_wt = jnp.zeros((HEAD_DIM, GROUP_HEADS * TQ), F32)
    for t, vw in enumerate(vws):
        o_wt = o_wt + _dot(vw[0], pw[t * TQ:(t + 1) * TQ, :])
    o_w = _unstack_shared(o_wt)

    o_ref[...] = _nsa_gate_mix(o_c, o_s, o_w, small_ref[...], gexp_ref, og_ref).astype(BF16)


def _nsa_prompt(r, kc_rep, vct_rep, og, batch, seq):
    assert WINDOW == 4 * TQ
    nq = seq // TQ
    nk = seq // TK
    nc = seq // CMP_STRIDE
    ns = seq // SEL_BLOCK
    ns_pad = -(-ns // LANES) * LANES
    n_valid = (seq - CMP_LEN) // CMP_STRIDE + 1
    ovlt = jnp.asarray(_overlap_matrix(nc, n_valid, ns, ns_pad).T, BF16)
    gexp = jnp.asarray(_gate_expand_matrix(), BF16)
    const = lambda b, i: (0, 0)
    row = lambda b, i: (b * nq + i, 0)
    kw_spec = lambda t: pl.BlockSpec((TQ, GROUP_W), lambda b, i: (b * nq + jnp.maximum(i + t - 4, 0), 0))
    vw_spec = lambda t: pl.BlockSpec((1, HEAD_DIM, TQ), lambda b, i: (b * nq + jnp.maximum(i + t - 4, 0), 0, 0))
    return pl.pallas_call(
        functools.partial(_nsa_prompt_kernel, ns=ns, n_sel=min(SEL_TOPK, ns)),
        grid=(batch, nq),
        in_specs=[pl.BlockSpec((TQ, GROUP_W), row), pl.BlockSpec((TQ, GROUP_W), row),
                  pl.BlockSpec((TQ, LANES), row),
                  pl.BlockSpec((nc, GROUP_W), lambda b, i: (b, 0)),
                  pl.BlockSpec((1, HEAD_DIM, nc), lambda b, i: (b, 0, 0)),
                  pl.BlockSpec((ns_pad, nc), const),
                  pl.BlockSpec((LANES, 3 * GROUP_W), const),
                  _resident((seq, GROUP_W), lambda b, i: (b, 0)),
                  _resident((nk, VH, TK), lambda b, i: (b, 0, 0))]
                 + [kw_spec(t) for t in range(5)] + [vw_spec(t) for t in range(5)]
                 + [pl.BlockSpec((1, GROUP_W), const)],
        out_specs=pl.BlockSpec((TQ, GROUP_W), row),
        out_shape=jax.ShapeDtypeStruct((batch * seq, GROUP_W), BF16),
        scratch_shapes=_flash_scratch(True),
        compiler_params=_cparams(("arbitrary", "arbitrary")),
        name="nsa_prompt",
    )(r["nqn"], r["nqr"], r["small"], kc_rep, vct_rep, ovlt, gexp, r["ks16"], r["vst"],
      *([r["kw16"]] * 5), *([r["vwt"]] * 5), og)


def _out_proj_kernel(x_ref, o0_ref, o1_ref, o2_ref, o3_ref, w_ref, gt_ref, sc_ref, sh_ref, g_ref,
                     x1_ref, hf_ref):
    o_cat = jnp.concatenate([o0_ref[...], o1_ref[...], o2_ref[...], o3_ref[...]], axis=1).astype(BF16)
    x1 = x_ref[...] + gt_ref[0] * _dot(o_cat, w_ref[...])
    x1_ref[...] = x1
    hf_ref[...] = (_rms_rows(x1, g_ref[...]) * (1.0 + sc_ref[0]) + sh_ref[0]).astype(BF16)


def _out_proj(x, outs, w_out, gt1, sc2, sh2, g2, tiles_per_seq, tm):
    rows = x.shape[0]
    tb = gt1.shape[1]
    mod_map = (lambda i: (i // tiles_per_seq, 0, 0)) if tb == 1 else (lambda i: (i, 0, 0))
    row = lambda i: (i, 0)
    const = lambda i: (0, 0)
    mod = pl.BlockSpec((1, tb, D_MODEL), mod_map)
    return pl.pallas_call(
        _out_proj_kernel,
        grid=(rows // tm,),
        in_specs=[pl.BlockSpec((tm, D_MODEL), row)] + [pl.BlockSpec((tm, GROUP_W), row)] * 4
                 + [pl.BlockSpec((D_MODEL, D_MODEL), const), mod, mod, mod, pl.BlockSpec((1, D_MODEL), const)],
        out_specs=[pl.BlockSpec((tm, D_MODEL), row), pl.BlockSpec((tm, D_MODEL), row)],
        out_shape=[jax.ShapeDtypeStruct((rows, D_MODEL), F32), jax.ShapeDtypeStruct((rows, D_MODEL), BF16)],
        compiler_params=_cparams(("arbitrary",)),
        name="out_proj",
    )(x, *outs, w_out, gt1, sc2, sh2, g2)


def _mlp_kernel(x1_ref, hf_ref, wu_ref, wd_ref, gt_ref, y_ref, acc_s):
    f = pl.program_id(1)

    @pl.when(f == 0)
    def _():
        acc_s[...] = jnp.zeros_like(acc_s)

    up = jnp.maximum(_dot(hf_ref[...], wu_ref[...]), 0.0)
    acc_s[...] += _dot((up * up).astype(BF16), wd_ref[...])

    @pl.when(f == pl.num_programs(1) - 1)
    def _():
        y_ref[...] = x1_ref[...] + gt_ref[0] * acc_s[...]


def _mlp(x1, hf, w_up, w_down, gt2, tiles_per_seq, tm, tf):
    rows = x1.shape[0]
    tb = gt2.shape[1]
    mod_map = (lambda i, f: (i // tiles_per_seq, 0, 0)) if tb == 1 else (lambda i, f: (i, 0, 0))
    row = lambda i, f: (i, 0)
    return pl.pallas_call(
        _mlp_kernel,
        grid=(rows // tm, D_FF // tf),
        in_specs=[pl.BlockSpec((tm, D_MODEL), row), pl.BlockSpec((tm, D_MODEL), row),
                  pl.BlockSpec((D_MODEL, tf), lambda i, f: (0, f)),
                  pl.BlockSpec((tf, D_MODEL), lambda i, f: (f, 0)),
                  pl.BlockSpec((1, tb, D_MODEL), mod_map)],
        out_specs=pl.BlockSpec((tm, D_MODEL), row),
        out_shape=jax.ShapeDtypeStruct((rows, D_MODEL), F32),
        scratch_shapes=[pltpu.VMEM((tm, D_MODEL), F32)],
        compiler_params=_cparams(("arbitrary", "arbitrary")),
        name="mlp",
    )(x1, hf, w_up, w_down, gt2)


def _block_diag(w):
    out = jnp.zeros((GROUP_W, GROUP_W), w.dtype)
    for h in range(GROUP_HEADS):
        out = out.at[h * HEAD_DIM:(h + 1) * HEAD_DIM, h * HEAD_DIM:(h + 1) * HEAD_DIM].set(w[h])
    return out


def _layer_weights(p, l):
    tile4 = lambda g: jnp.tile(g, GROUP_HEADS)
    ones = jnp.ones((HEAD_DIM,), F32)
    w_in = jnp.concatenate([p["w_in"][l][:, _in_perm()],
                            jnp.zeros((D_MODEL, IN_W - _in_perm().shape[0]), F32)], axis=1).astype(BF16)
    ng = p["nsa_qk_g"][l]
    gains = jnp.stack([tile4(p["fox_qk_g"][l, 0]), tile4(p["fox_qk_g"][l, 1]),
                       tile4(p["moba_qk_g"][l, 0]), tile4(p["moba_qk_g"][l, 1]),
                       tile4(ng[0]), jnp.concatenate([ng[2], ones, ng[3], ones]),
                       jnp.ones((GROUP_W,), F32), jnp.ones((GROUP_W,), F32)])
    bf_row = jnp.zeros((1, LANES), F32).at[0, :GROUP_HEADS].set(p["fox_bf"][l])
    row = lambda a: a.reshape(1, -1)
    lru = dict(cw=p["lru_conv_w"][l], cb=row(p["lru_conv_b"][l]),
               wa=_block_diag(p["lru_wa"][l]).astype(BF16), wx=_block_diag(p["lru_wx"][l]).astype(BF16),
               ba=row(p["lru_ba"][l]), bx=row(p["lru_bx"][l]), lam=row(p["lru_lambda"][l]),
               og=row(p["out_g"][l, 0]))
    gcmp = jnp.concatenate([ng[1], ones]).reshape(1, LANES)
    nsa = dict(cw=_cmp_weights(p["nsa_cmp_w"][l]), cpos=_cmp_pos(p["nsa_cmp_pos"][l]), gcmp=gcmp,
               rep=jnp.asarray(_rep_matrix(), BF16))
    return dict(w_in=w_in, gains=gains, bf_row=bf_row, lru=lru, nsa=nsa,
                g1=row(p["norm_g"][l, 0]), g2=row(p["norm_g"][l, 1]),
                og=[row(p["out_g"][l, k]) for k in range(N_MIXERS)],
                w_out=p["w_out"][l].astype(BF16), w_up=p["w_up"][l].astype(BF16),
                w_down=p["w_down"][l].astype(BF16))


def _prompt_layer(x, mod, lw, batch, seq, tabs):
    tm = min(512, seq)
    tps = seq // tm
    sh1, sc1, gt1, sh2, sc2, gt2 = [m[:, None, :] for m in jnp.split(mod, 6, axis=-1)]
    r = _in_proj(x, sc1, sh1, lw["g1"], lw["w_in"], tabs, lw["gains"], lw["bf_row"],
                 tiles_per_seq=tps, prompt=True, tm=tm)
    o_lru, h_t, conv_t = _lru_prompt(r["zl"], lw["lru"], batch, seq, tm)
    o_fox = _fox_prompt(r, lw["og"][1], batch, seq)
    o_moba = _moba_prompt(r, lw["og"][2], batch, seq)
    kc_rep, vct_rep = _cmp_prompt(r["kcvc"], lw["nsa"], batch, seq)
    o_nsa = _nsa_prompt(r, kc_rep, vct_rep, lw["og"][3], batch, seq)
    x1, hf = _out_proj(x, (o_lru, o_fox, o_moba, o_nsa), lw["w_out"], gt1, sc2, sh2, lw["g2"], tps, tm)
    y = _mlp(x1, hf, lw["w_up"], lw["w_down"], gt2, tps, tm, 1024)
    keep = min(WINDOW, seq)
    state = (r["fkv"].reshape(batch, seq, 2, GROUP_HEADS, HEAD_DIM),
             r["small"][:, :GROUP_HEADS].reshape(batch, seq, GROUP_HEADS),
             r["mkv"].reshape(batch, seq, 2, GROUP_HEADS, HEAD_DIM),
             r["nkv"].reshape(batch, seq, 4, HEAD_DIM),
             r["nwin"].reshape(batch, seq, 2, HEAD_DIM)[:, seq - keep:],
             h_t.reshape(batch, GROUP_W),
             conv_t[:, 8 - (CONV_W - 1):, :])
    return y, state


def _prompt_trunk(x_prompt, mods, lws):
    batch, seq, _ = x_prompt.shape
    assert seq % max(512, TK) == 0 or seq in (256,)
    tabs = _rope_tables(jnp.arange(seq, dtype=jnp.int32))
    x = x_prompt.reshape(batch * seq, D_MODEL)
    states = []
    for l, lw in enumerate(lws):
        x, st = _prompt_layer(x, mods[l], lw, batch, seq, tabs)
        states.append(st)
    stacked = [jnp.stack([s[i] for s in states], axis=0) for i in range(len(states[0]))]
    return x.reshape(batch, seq, D_MODEL), stacked


def _lru_sample_kernel(zl_ref, h0_ref, cbuf_ref, cw_ref, cb_ref, wa_ref, wx_ref, ba_ref, bx_ref, lam_ref,
                       og_ref, o_ref, ht_ref, cnew_ref, *, n_seq, t_real):
    o_ref[...] = jnp.zeros_like(o_ref)
    xs = [cbuf_ref[k] for k in range(CONV_W - 1)] + [zl_ref[t, :, 0:GROUP_W] for t in range(t_real)]
    h = h0_ref[...]
    for t in range(t_real):
        u = cb_ref[...]
        for k in range(CONV_W):
            u = u + xs[t + k] * cw_ref[k:k + 1, :]
        a, b = _lru_coeffs(u, wa_ref, wx_ref, ba_ref, bx_ref, lam_ref)
        h = a * h + b
        o = h * _gelu_tanh(zl_ref[t, :, GROUP_W:2 * GROUP_W])
        o_ref[t] = _rms_rows(o, og_ref[...])
    ht_ref[...] = h
    for k in range(CONV_W - 1):
        cnew_ref[k] = xs[t_real + k]


def _lru_sample(zl, h0, cbuf, lw, n_seq, t_real):
    assert t_real >= CONV_W - 1
    zl_t = jnp.transpose(zl.reshape(n_seq, TS, 2 * GROUP_W), (1, 0, 2))
    o_t, h_t, conv_t = pl.pallas_call(
        functools.partial(_lru_sample_kernel, n_seq=n_seq, t_real=t_real),
        out_shape=[jax.ShapeDtypeStruct((TS, n_seq, GROUP_W), F32),
                   jax.ShapeDtypeStruct((n_seq, GROUP_W), F32),
                   jax.ShapeDtypeStruct((CONV_W - 1, n_seq, GROUP_W), F32)],
        compiler_params=pltpu.CompilerParams(vmem_limit_bytes=VMEM_LIMIT),
        name="lru_sample",
    )(zl_t, h0, cbuf, lw["cw"], lw["cb"], lw["wa"], lw["wx"], lw["ba"], lw["bx"], lw["lam"], lw["og"])
    return jnp.transpose(o_t, (1, 0, 2)).reshape(n_seq * TS, GROUP_W), h_t, conv_t


def _page_specs(layer, tail, n_pages):
    zeros = (0,) * len(tail)
    return [pl.BlockSpec((1, 1) + tail, functools.partial(lambda j, s, pt: (layer, pt[s, j]) + zeros, j))
            for j in range(n_pages)]


def _seq_spec(width, rows=TS):
    return pl.BlockSpec((rows, width), lambda s, pt: (s, 0))


def _const_spec(shape):
    zeros = (0,) * len(shape)
    return pl.BlockSpec(shape, lambda s, pt: zeros)


def _pad_rows(a, rows):
    return jnp.concatenate([a, jnp.zeros((rows - a.shape[0], a.shape[1]), a.dtype)], axis=0)


def _rep_rows(a, t):
    return jnp.concatenate([jnp.broadcast_to(a[h:h + 1, :], (t, a.shape[1])) for h in range(GROUP_HEADS)], axis=0)


def _sample_pos(past):
    return _rep4(past + lax.broadcasted_iota(jnp.int32, (TS, 1), 0))


def _weighted_values(p, vals):
    o = jnp.zeros((p.shape[0], GROUP_W), F32)
    for j, (kind, v) in enumerate(vals):
        pj = p[:, j * LANES:(j + 1) * LANES]
        o = o + (_dot_nt(pj, v) if kind == "t" else _dot(pj, v))
    return o


def _fox_sample_kernel(pt_ref, q_ref, kvn_ref, lfn_ref, u_ref, og_ref, *refs, n_pages, page):
    kv_refs = refs[:n_pages]
    lf_refs = refs[n_pages:2 * n_pages]
    o_ref = refs[2 * n_pages]
    past = n_pages * page
    qs = _stack_heads(q_ref[...]).astype(BF16)
    u = u_ref[...]
    off = jnp.zeros((GROUP_HEADS, 1), F32)
    logits, fcs, vals = [], [], []
    for j in range(n_pages):
        c = _dot_x3(lf_refs[j][0, 0], u) + off
        off = c[:, page - 1:page]
        fcs.append(c)
        logits.append(_dot(qs, kv_refs[j][0, 0, 0:GROUP_W, :].astype(BF16)))
        vals.append(("t", kv_refs[j][0, 0, GROUP_W:2 * GROUP_W, :].astype(BF16)))
    cn = _dot_x3(lfn_ref[0], u) + off
    fcs.append(cn)
    kvn = _pad_rows(kvn_ref[...], LANES).astype(BF16)
    logits.append(_dot_nt(qs, kvn[:, 0:GROUP_W]))
    vals.append(("n", kvn[:, GROUP_W:2 * GROUP_W]))

    lane = lax.broadcasted_iota(jnp.int32, (1, LANES), 1)
    t_row = lax.broadcasted_iota(jnp.int32, (GROUP_HEADS * TS, 1), 0) % TS
    fq = jnp.sum(jnp.where(lane == t_row, _rep_rows(cn, TS), 0.0), axis=-1, keepdims=True)
    decay = fq - jnp.concatenate([_rep_rows(c, TS) for c in fcs], axis=1)
    s = jnp.concatenate(logits, axis=1) + decay * LOG2E
    kpos = lax.broadcasted_iota(jnp.int32, (1, s.shape[1]), 1)
    p = _masked_softmax(s, kpos <= _sample_pos(past)).astype(BF16)
    o_ref[...] = _rms_rows(_unstack_heads(_weighted_values(p, vals), TS), og_ref[...])


def _fox_sample(page_table, layer, fq, fkv_new, lf_new_t, pool_kv_t, pool_lf_t, og):
    n_seq, n_pages = page_table.shape
    page = pool_kv_t.shape[-1]
    assert page == LANES
    u = jnp.asarray(np.triu(np.ones((page, page), np.float32)), BF16)
    grid_spec = pltpu.PrefetchScalarGridSpec(
        num_scalar_prefetch=1, grid=(n_seq,),
        in_specs=[_seq_spec(GROUP_W), _seq_spec(2 * GROUP_W),
                  pl.BlockSpec((1, GROUP_HEADS, LANES), lambda s, pt: (s, 0, 0)),
                  _const_spec((page, page)), _const_spec((1, GROUP_W))]
                 + _page_specs(layer, (2 * GROUP_W, page), n_pages)
                 + _page_specs(layer, (GROUP_HEADS, page), n_pages),
        out_specs=_seq_spec(GROUP_W))
    return pl.pallas_call(
        functools.partial(_fox_sample_kernel, n_pages=n_pages, page=page),
        grid_spec=grid_spec,
        out_shape=jax.ShapeDtypeStruct((n_seq * TS, GROUP_W), F32),
        compiler_params=_cparams(("arbitrary",)),
        name="fox_sample",
    )(page_table, fq, fkv_new, lf_new_t, u, og, *([pool_kv_t] * n_pages), *([pool_lf_t] * n_pages))


def _moba_sample_kernel(pt_ref, q_ref, kvn_ref, og_ref, *refs, n_pages, page):
    kv_refs = refs[:n_pages]
    o_ref = refs[n_pages]
    past = n_pages * page
    per_blk = MOBA_BLOCK // page
    nb = n_pages // per_blk
    qs32 = _stack_heads(q_ref[...])
    qs = qs32.astype(BF16)
    lane = lax.broadcasted_iota(jnp.int32, (1, LANES), 1)
    logits, vals = [], []
    kmean_t = jnp.zeros((GROUP_W, LANES), F32)
    for n in range(nb):
        ksum = jnp.zeros((GROUP_W, 1), F32)
        for j in range(n * per_blk, (n + 1) * per_blk):
            kt = kv_refs[j][0, 0, 0:GROUP_W, :]
            ksum = ksum + jnp.sum(kt, axis=1, keepdims=True)
            logits.append(_dot(qs, kt.astype(BF16)))
            vals.append(("t", kv_refs[j][0, 0, GROUP_W:2 * GROUP_W, :].astype(BF16)))
        kmean_t = jnp.where(lane == n, ksum * (1.0 / MOBA_BLOCK), kmean_t)
    qh, ql = _split2(qs32)
    kh, kl = _split2(kmean_t)
    sc = _dot(qh, kh) + _dot(qh, kl) + _dot(ql, kh)
    sc = jnp.where(lane < nb, sc, NEG_INF)
    sel = _rank_select(sc, min(MOBA_TOPK, nb + 1), nb)
    masks = []
    for j in range(n_pages):
        n = j // per_blk
        masks.append(jnp.broadcast_to(sel[:, n:n + 1] > 0.5, (GROUP_HEADS * TS, page)))
    kvn = _pad_rows(kvn_ref[...], LANES).astype(BF16)
    logits.append(_dot_nt(qs, kvn[:, 0:GROUP_W]))
    vals.append(("n", kvn[:, GROUP_W:2 * GROUP_W]))
    kpos_new = past + lane
    masks.append(kpos_new <= _sample_pos(past))
    p = _masked_softmax(jnp.concatenate(logits, axis=1), jnp.concatenate(masks, axis=1)).astype(BF16)
    o_ref[...] = _rms_rows(_unstack_heads(_weighted_values(p, vals), TS), og_ref[...])


def _moba_sample(page_table, layer, mq, mkv_new, pool_kv_t, og):
    n_seq, n_pages = page_table.shape
    page = pool_kv_t.shape[-1]
    assert page == LANES and (n_pages * page) % MOBA_BLOCK == 0 and TS <= MOBA_BLOCK
    grid_spec = pltpu.PrefetchScalarGridSpec(
        num_scalar_prefetch=1, grid=(n_seq,),
        in_specs=[_seq_spec(GROUP_W), _seq_spec(2 * GROUP_W), _const_spec((1, GROUP_W))]
                 + _page_specs(layer, (2 * GROUP_W, page), n_pages),
        out_specs=_seq_spec(GROUP_W))
    return pl.pallas_call(
        functools.partial(_moba_sample_kernel, n_pages=n_pages, page=page),
        grid_spec=grid_spec,
        out_shape=jax.ShapeDtypeStruct((n_seq * TS, GROUP_W), F32),
        compiler_params=_cparams(("arbitrary",)),
        name="moba_sample",
    )(page_table, mq, mkv_new, og, *([pool_kv_t] * n_pages))


def _nsa_sample_kernel(pt_ref, qn_ref, qr_ref, small_ref, kvn_ref, winn_ref, win_ref, cpos_ref, cw_ref,
                       g_ref, rep_ref, ovl_ref, gexp_ref, og_ref, *refs, n_pages, page, ns, n_sel):
    pg_refs = refs[:n_pages]
    o_ref = refs[n_pages]
    tok_s, x2_s = refs[n_pages + 1:n_pages + 3]
    past = n_pages * page
    n_chunks = past // CMP_STRIDE
    pos = past + lax.broadcasted_iota(jnp.int32, (TS, 1), 0)
    pos_s = _rep4(pos)
    qn_s = _stack_heads(qn_ref[...]).astype(BF16)
    qr_s = _stack_heads(qr_ref[...]).astype(BF16)
    rep = rep_ref[...]
    rows4 = lambda a: jnp.concatenate([a] * GROUP_HEADS, axis=0)

    for p in range(n_pages):
        tok_s[p * page:(p + 1) * page, :] = pg_refs[p][0, 0, 0:LANES, :].T
    for j in range(CMP_STRIDE):
        x2_s[:, j * LANES:(j + 1) * LANES] = tok_s[pl.ds(j, n_chunks, stride=CMP_STRIDE), :]
    y = _cmp_windows(x2_s[...], cpos_ref, cw_ref, g_ref, n_chunks - 1)
    c16 = _dot(y.astype(BF16), rep).astype(BF16)

    lc = _dot_nt(qn_s, c16[:, 0:GROUP_W])
    cend = lax.broadcasted_iota(jnp.int32, (1, n_chunks), 1) * CMP_STRIDE + (CMP_LEN - 1)
    pc = _masked_softmax(lc, cend <= pos_s)
    o_c = _dot(pc.astype(BF16), c16[:, GROUP_W:2 * GROUP_W])

    pcs = pc[0:TS] + pc[TS:2 * TS] + pc[2 * TS:3 * TS] + pc[3 * TS:4 * TS]
    imp = _dot_x3(pcs, ovl_ref[...])
    jb = lax.broadcasted_iota(jnp.int32, (1, imp.shape[-1]), 1)
    cur = pos // SEL_BLOCK
    imp = jnp.where((jb == 0) | (jb == cur) | (jb == cur - 1), jnp.inf, imp)
    imp = jnp.where((jb * SEL_BLOCK > pos) | (jb >= ns), NEG_INF, imp)
    sel = _rank_select(imp, n_sel, ns)

    lane = lax.broadcasted_iota(jnp.int32, (1, page), 1)
    per_blk = page // SEL_BLOCK
    logits, masks, vals = [], [], []
    for p in range(n_pages):
        logits.append(_dot(qr_s, rows4(pg_refs[p][0, 0, 2 * HEAD_DIM:3 * HEAD_DIM, :]).astype(BF16)))
        vals.append(("t", rows4(pg_refs[p][0, 0, 3 * HEAD_DIM:4 * HEAD_DIM, :]).astype(BF16)))
        chosen = jnp.zeros((TS, page), F32)
        for b in range(per_blk):
            n = p * per_blk + b
            chosen = jnp.where(lane // SEL_BLOCK == b, sel[:, n:n + 1], chosen)
        masks.append(_rep4(chosen > 0.5))
    kvn = _dot(_pad_rows(kvn_ref[:, LANES:2 * LANES], LANES).astype(BF16), rep).astype(BF16)
    logits.append(_dot_nt(qr_s, kvn[:, 0:GROUP_W]))
    vals.append(("n", kvn[:, GROUP_W:2 * GROUP_W]))
    kpos_new = past + lax.broadcasted_iota(jnp.int32, (1, LANES), 1)
    n_new = past // SEL_BLOCK
    chosen_new = (kpos_new // SEL_BLOCK == n_new) & (sel[:, n_new:n_new + 1] > 0.5)
    masks.append(_rep4(chosen_new) & (kpos_new <= pos_s))
    ps = _masked_softmax(jnp.concatenate(logits, axis=1), jnp.concatenate(masks, axis=1)).astype(BF16)
    o_s = _weighted_values(ps, vals)

    wb = win_ref.shape[-1]
    kwt = rows4(win_ref[0, 0, 0:HEAD_DIM, :]).astype(BF16)
    vwt = rows4(win_ref[0, 0, HEAD_DIM:2 * HEAD_DIM, :]).astype(BF16)
    kvwn = _dot(_pad_rows(winn_ref[...], LANES).astype(BF16), rep).astype(BF16)
    kpos = jnp.concatenate([past - wb + lax.broadcasted_iota(jnp.int32, (1, wb), 1), kpos_new], axis=1)
    lw = jnp.concatenate([_dot(qr_s, kwt), _dot_nt(qr_s, kvwn[:, 0:GROUP_W])], axis=1)
    wmask = (kpos >= past - wb) & (kpos <= pos_s) & (kpos >= pos_s - WINDOW)
    pw = _masked_softmax(lw, wmask).astype(BF16)
    o_w = _dot_nt(pw[:, 0:wb], vwt) + _dot(pw[:, wb:wb + LANES], kvwn[:, GROUP_W:2 * GROUP_W])

    o_ref[...] = _nsa_gate_mix(_unstack_heads(o_c, TS), _unstack_heads(o_s, TS), _unstack_heads(o_w, TS),
                               small_ref[...], gexp_ref, og_ref)


def _nsa_sample(page_table, layer, r, win_state_t, pool_kv_t, nw, og):
    n_seq, n_pages = page_table.shape
    page = pool_kv_t.shape[-1]
    past = n_pages * page
    assert page == LANES and TS < CMP_STRIDE and past % MOBA_BLOCK == 0 and TS <= SEL_BLOCK
    n_chunks = past // CMP_STRIDE
    ns = (-(-(past + TS) // MOBA_BLOCK) * MOBA_BLOCK) // SEL_BLOCK
    ns_pad = -(-ns // LANES) * LANES
    ovl = jnp.asarray(_overlap_matrix(n_chunks, n_chunks - 1, ns, ns_pad), BF16)
    gexp = jnp.asarray(_gate_expand_matrix(), BF16)
    width = CMP_STRIDE * LANES
    wb = win_state_t.shape[-1]
    grid_spec = pltpu.PrefetchScalarGridSpec(
        num_scalar_prefetch=1, grid=(n_seq,),
        in_specs=[_seq_spec(GROUP_W), _seq_spec(GROUP_W), _seq_spec(LANES), _seq_spec(GROUP_W), _seq_spec(LANES),
                  pl.BlockSpec((1, 1, 2 * HEAD_DIM, wb), lambda s, pt: (layer, s, 0, 0)),
                  _const_spec((8, width)), _const_spec((width, 2 * LANES)),
                  _const_spec((1, LANES)), _const_spec((LANES, 2 * GROUP_W)),
                  _const_spec((n_chunks, ns_pad)), _const_spec((LANES, 3 * GROUP_W)), _const_spec((1, GROUP_W))]
                 + _page_specs(layer, (GROUP_W, page), n_pages),
        out_specs=_seq_spec(GROUP_W),
        scratch_shapes=[pltpu.VMEM((past, LANES), F32), pltpu.VMEM((n_chunks, width), F32)])
    return pl.pallas_call(
        functools.partial(_nsa_sample_kernel, n_pages=n_pages, page=page, ns=ns, n_sel=min(SEL_TOPK, ns)),
        grid_spec=grid_spec,
        out_shape=jax.ShapeDtypeStruct((n_seq * TS, GROUP_W), F32),
        compiler_params=_cparams(("arbitrary",)),
        name="nsa_sample",
    )(page_table, r["nqn"], r["nqr"], r["small"], r["nkv"], r["nwin"], win_state_t,
      nw["cpos"], nw["cw"], nw["gcmp"], nw["rep"], ovl, gexp, og,
      *([pool_kv_t] * n_pages))


def _sample_layer(x, mod_rows, lw, layer, pools, page_table, n_seq, t_real, tabs):
    rows = n_seq * TS
    tm = min(512, rows)
    sh1, sc1, gt1, sh2, sc2, gt2 = [m.reshape(rows // tm, tm, D_MODEL) for m in jnp.split(mod_rows, 6, axis=-1)]
    r = _in_proj(x, sc1, sh1, lw["g1"], lw["w_in"], tabs, lw["gains"], lw["bf_row"],
                 tiles_per_seq=1, prompt=False, tm=tm)
    o_lru, h_t, conv_t = _lru_sample(r["zl"], pools["lru_h"][layer], jnp.transpose(pools["lru_conv"][layer], (1, 0, 2)),
                                     lw["lru"], n_seq, t_real)
    lf_new = r["small"][:, :GROUP_HEADS].reshape(n_seq, TS, GROUP_HEADS)
    lf_new_t = jnp.pad(jnp.transpose(lf_new, (0, 2, 1)), ((0, 0), (0, 0), (0, LANES - TS)))
    o_fox = _fox_sample(page_table, layer, r["fq"], r["fkv"], lf_new_t, pools["fox_kv_t"], pools["fox_lf_t"],
                        lw["og"][1])
    o_moba = _moba_sample(page_table, layer, r["mq"], r["mkv"], pools["moba_kv_t"], lw["og"][2])
    o_nsa = _nsa_sample(page_table, layer, r, pools["nsa_win_t"], pools["nsa_kv_t"], lw["nsa"], lw["og"][3])
    x1, hf = _out_proj(x, (o_lru, o_fox, o_moba, o_nsa), lw["w_out"], gt1, sc2, sh2, lw["g2"], 1, tm)
    y = _mlp(x1, hf, lw["w_up"], lw["w_down"], gt2, 1, tm, 1024)
    real = lambda a, tail: a.reshape((n_seq, TS) + tail)[:, :t_real]
    win_new = real(r["nwin"], (2, HEAD_DIM))
    win_prev = pools["nsa_win"][layer]
    keep = min(WINDOW, win_prev.shape[1] + t_real)
    win_all = jnp.concatenate([win_prev, win_new], axis=1)
    state = (real(r["fkv"], (2, GROUP_HEADS, HEAD_DIM)), real(r["small"][:, :GROUP_HEADS], (GROUP_HEADS,)),
             real(r["mkv"], (2, GROUP_HEADS, HEAD_DIM)), real(r["nkv"], (4, HEAD_DIM)),
             win_all[:, win_all.shape[1] - keep:], h_t, jnp.transpose(conv_t, (1, 0, 2)))
    return y, state


def _sample_trunk(x_sample, mods, lws, caches, page_table):
    n_seq, t_real, _ = x_sample.shape
    assert t_real <= TS
    depth, n_phys, page = caches["fox_kv"].shape[:3]
    past = page_table.shape[1] * page
    rows = n_seq * TS
    x = jnp.pad(x_sample, ((0, 0), (0, TS - t_real), (0, 0))).reshape(rows, D_MODEL)
    tm = min(512, rows)
    tabs = _rope_tables(past + (jnp.arange(tm, dtype=jnp.int32) % TS))
    pools = dict(
        fox_kv_t=jnp.transpose(caches["fox_kv"], (0, 1, 3, 4, 5, 2)).reshape(depth, n_phys, 2 * GROUP_W, page),
        fox_lf_t=jnp.transpose(caches["fox_logf"], (0, 1, 3, 2)),
        moba_kv_t=jnp.transpose(caches["moba_kv"], (0, 1, 3, 4, 5, 2)).reshape(depth, n_phys, 2 * GROUP_W, page),
        nsa_kv_t=jnp.transpose(caches["nsa_kv"], (0, 1, 3, 4, 2)).reshape(depth, n_phys, GROUP_W, page),
        nsa_win_t=jnp.transpose(caches["nsa_win"], (0, 1, 3, 4, 2)).reshape(depth, n_seq, 2 * HEAD_DIM, -1),
        nsa_win=caches["nsa_win"], lru_h=caches["lru_h"], lru_conv=caches["lru_conv"])
    states = []
    for l, lw in enumerate(lws):
        mod_rows = jnp.repeat(mods[l], TS, axis=0)
        x, st = _sample_layer(x, mod_rows, lw, l, pools, page_table, n_seq, t_real, tabs)
        states.append(st)
    stacked = [jnp.stack([s[i] for s in states], axis=0) for i in range(len(states[0]))]
    return x.reshape(n_seq, TS, D_MODEL)[:, :t_real], stacked


def kernel(x_prompt, x_sample, cache_fox_kv, cache_fox_logf, cache_moba_kv, cache_nsa_kv, state_nsa_win,
           state_lru_h, state_lru_conv, page_table, c_prompt, c_sample, norm_g, w_ada, b_ada, w_in,
           lru_conv_w, lru_conv_b, lru_wa, lru_ba, lru_wx, lru_bx, lru_lambda, fox_bf, fox_qk_g, moba_qk_g,
           nsa_qk_g, nsa_cmp_pos, nsa_cmp_w, out_g, w_out, w_up, w_down):
    p = dict(norm_g=norm_g, w_in=w_in, lru_conv_w=lru_conv_w, lru_conv_b=lru_conv_b, lru_wa=lru_wa,
             lru_ba=lru_ba, lru_wx=lru_wx, lru_bx=lru_bx, lru_lambda=lru_lambda, fox_bf=fox_bf,
             fox_qk_g=fox_qk_g, moba_qk_g=moba_qk_g, nsa_qk_g=nsa_qk_g, nsa_cmp_pos=nsa_cmp_pos,
             nsa_cmp_w=nsa_cmp_w, out_g=out_g, w_out=w_out, w_up=w_up, w_down=w_down)
    depth = w_in.shape[0]
    batch = x_prompt.shape[0]
    dec_batch = x_sample.shape[0]
    n_c = batch + dec_batch
    n_c_pad = -(-n_c // 8) * 8
    c_all = jnp.concatenate([c_prompt, c_sample, jnp.zeros((n_c_pad - n_c, D_MODEL), F32)], axis=0)
    mods = _ada(c_all, w_ada, b_ada)
    lws = [_layer_weights(p, l) for l in range(depth)]
    y_prompt, st_p = _prompt_trunk(x_prompt, [mods[l, :batch] for l in range(depth)], lws)
    caches = dict(fox_kv=cache_fox_kv, fox_logf=cache_fox_logf, moba_kv=cache_moba_kv, nsa_kv=cache_nsa_kv,
                  nsa_win=state_nsa_win, lru_h=state_lru_h, lru_conv=state_lru_conv)
    y_sample, st_s = _sample_trunk(x_sample, [mods[l, batch:n_c] for l in range(depth)], lws, caches, page_table)
    return (y_prompt, y_sample) + tuple(st_p) + tuple(st_s)
```
